```python
import math
import jax, jax.numpy as jnp
from jax import lax
import numpy as np

D_MODEL = 4096
BATCH = 16
SEQ = 256
DEPTH = 2
DEC_BATCH = 8
DEC_SEQ = 2048
PAST_LEN = 256

GRID_W = 64
HEAD_DIM = 128
N_EVEN = (DEPTH + 1) // 2
N_ODD = DEPTH // 2
D_POOL = D_MODEL // 2
POOL_WINDOWS = (2, 4, 8, 16)
N_POOL_GROUPS = len(POOL_WINDOWS)
POOL_GROUP = D_POOL // N_POOL_GROUPS
NA_HEADS = (D_MODEL - D_POOL) // HEAD_DIM
D_NA = NA_HEADS * HEAD_DIM
NA_ROWS = 8
NA_COLS = 16
NA_COL_BLOCK = 16
NA_COL_REGION = 2 * NA_COL_BLOCK
C_HEADS = D_MODEL // HEAD_DIM
C_KV_HEADS = C_HEADS // 4
C_GROUP = C_HEADS // C_KV_HEADS
C_WINDOW = 128
C_BLOCK = 128
D_FF = ((8 * D_MODEL // 3 + 255) // 256) * 256
N_SUB = 3
ROPE_BASE = 10000.0
EPS = 1e-6
Q_BLOCK = 128
NEG = -1e30

kernel_name = "hybrid_pool_natten_swa_diffusion_step"


def rms_norm(x, g):
    xf = x.astype(jnp.float32)
    y = xf * lax.rsqrt(jnp.mean(xf * xf, axis=-1, keepdims=True) + EPS)
    return (y * g.astype(jnp.float32)).astype(x.dtype)


def modulation(cvec, w, b):
    m = jax.nn.silu(cvec) @ w + b
    return m.reshape(cvec.shape[0], 1, 3 * N_SUB, D_MODEL)


def adaln(x, g, m, s):
    return rms_norm(x, g) * (1 + m[:, :, 3 * s + 1]) + m[:, :, 3 * s]


def swiglu(h, w_in, w_out):
    gt, up = jnp.split(h @ w_in, 2, axis=-1)
    return (jax.nn.silu(gt) * up) @ w_out


def ffn_sublayer(x, g, m, s, w_in, w_out):
    return x + 0.5 * m[:, :, 3 * s + 2] * swiglu(adaln(x, g, m, s), w_in, w_out)


def rope_2d(x):
    L = x.shape[1]
    t = jnp.arange(L)
    half = HEAD_DIM // 2
    quarter = half // 2
    freqs = ROPE_BASE ** (-jnp.arange(quarter, dtype=jnp.float32) * 2.0 / half)
    bshape = (1, L) + (1,) * (x.ndim - 3) + (quarter,)
    xf = x.astype(jnp.float32)

    def rot(xa, pos):
        ang = (pos.astype(jnp.float32)[:, None] * freqs[None]).reshape(bshape)
        cos, sin = jnp.cos(ang), jnp.sin(ang)
        x1, x2 = xa[..., :quarter], xa[..., quarter:]
        return jnp.concatenate([x1 * cos - x2 * sin, x2 * cos + x1 * sin], axis=-1)

    out = jnp.concatenate([rot(xf[..., :half], t // GRID_W), rot(xf[..., half:], t % GRID_W)], axis=-1)
    return out.astype(x.dtype)


def sink_softmax(s, sink):
    if sink is None:
        return jax.nn.softmax(s, axis=-1)
    sk = jnp.broadcast_to(sink.astype(jnp.float32), s.shape[:-1] + (1,))
    return jax.nn.softmax(jnp.concatenate([sk, s], axis=-1), axis=-1)[..., 1:]


def dense_attention(q, k, v, sink):
    B, Lq, Hkv, G, d = q.shape
    scale = d ** -0.5
    qb = jnp.swapaxes(q.reshape(B, Lq // Q_BLOCK, Q_BLOCK, Hkv, G, d), 0, 1)

    def block(qi):
        s = jnp.einsum("bqhgd,bkhd->bhgqk", qi, k, preferred_element_type=jnp.float32) * scale
        p = sink_softmax(s, sink)
        return jnp.einsum("bhgqk,bkhd->bqhgd", p.astype(v.dtype), v)

    o = lax.map(block, qb)
    return jnp.swapaxes(o, 0, 1).reshape(B, Lq, Hkv, G, d)


def pool_mixer(u, pool_w, pool_scale):
    B, L, _ = u.shape
    uf = u.astype(jnp.float32).reshape(B, L, N_POOL_GROUPS, POOL_GROUP)
    cs = jnp.pad(jnp.cumsum(uf, axis=1), ((0, 0), (1, 0), (0, 0), (0, 0)))
    t = jnp.arange(L)
    means = []
    for gi, w in enumerate(POOL_WINDOWS):
        lo = jnp.clip(t - w // 2, 0, L)
        hi = jnp.clip(t - w // 2 + w, 0, L)
        cnt = (hi - lo).astype(jnp.float32)[None, :, None]
        means.append((cs[:, hi, gi] - cs[:, lo, gi]) / cnt)
    diff = (jnp.stack(means, axis=2) - uf).astype(u.dtype)
    y = jnp.einsum("blgc,gce->blge", diff, pool_w).reshape(B, L, D_POOL)
    return y * pool_scale


def neighbourhood_attention(q, k, v, ck, cv, rel_bias):
    B, L, H, d = q.shape
    rows = L // GRID_W
    kr = min(NA_ROWS, rows)
    ncb = GRID_W // NA_COL_BLOCK
    scale = d ** -0.5
    qg = q.reshape(B, rows, ncb, NA_COL_BLOCK, H, d)
    kg = k.reshape(B, rows, GRID_W, H, d)
    vg = v.reshape(B, rows, GRID_W, H, d)
    qcol = jnp.arange(GRID_W).reshape(ncb, NA_COL_BLOCK)
    reg_start = jnp.clip(jnp.arange(ncb) * NA_COL_BLOCK - NA_COLS // 2, 0, GRID_W - NA_COL_REGION)
    kcol = reg_start[:, None] + jnp.arange(NA_COL_REGION)
    win_start = jnp.clip(qcol - NA_COLS // 2, 0, GRID_W - NA_COLS)
    kc = kcol[:, None, :]
    col_valid = (kc >= win_start[..., None]) & (kc < win_start[..., None] + NA_COLS)
    dcol_idx = jnp.clip(kc - qcol[..., None] + NA_COLS - 1, 0, 2 * NA_COLS - 2)
    valid = col_valid[None, None, :, :, None, :]

    def row_block(r):
        rs = jnp.clip(r - kr // 2, 0, rows - kr)
        kw = lax.dynamic_slice_in_dim(kg, rs, kr, axis=1)[:, :, kcol]
        vw = lax.dynamic_slice_in_dim(vg, rs, kr, axis=1)[:, :, kcol]
        qr = lax.dynamic_index_in_dim(qg, r, axis=1, keepdims=False)
        drow_idx = rs - r + jnp.arange(kr) + NA_ROWS - 1
        bias = rel_bias[:, drow_idx][:, :, dcol_idx]
        bias = jnp.transpose(bias, (0, 2, 3, 1, 4)).astype(jnp.float32)
        s_loc = jnp.einsum("bnqhd,binchd->bhnqic", qr, kw, preferred_element_type=jnp.float32) * scale
        s_loc = jnp.where(valid, s_loc + bias[None], NEG)
        s_loc = s_loc.reshape(B, H, ncb, NA_COL_BLOCK, kr * NA_COL_REGION)
        s_ctx = jnp.einsum("bnqhd,bkhd->bhnqk", qr, ck, preferred_element_type=jnp.float32) * scale
        p = jax.nn.softmax(jnp.concatenate([s_loc, s_ctx], axis=-1), axis=-1).astype(v.dtype)
        nloc = kr * NA_COL_REGION
        p_loc = p[..., :nloc].reshape(B, H, ncb, NA_COL_BLOCK, kr, NA_COL_REGION)
        return (jnp.einsum("bhnqic,binchd->bnqhd", p_loc, vw)
                + jnp.einsum("bhnqk,bkhd->bnqhd", p[..., nloc:], cv))

    o = lax.map(row_block, jnp.arange(rows))
    return jnp.swapaxes(o, 0, 1).reshape(B, L, H, d)


def window_attention(q, k, v, ck, cv, sink):
    B, L, Hkv, G, d = q.shape
    nb = L // C_BLOCK
    span = C_BLOCK + 2 * C_WINDOW
    scale = d ** -0.5
    pad = ((0, 0), (C_WINDOW, C_WINDOW), (0, 0), (0, 0))
    kp, vp = jnp.pad(k, pad), jnp.pad(v, pad)
    qb = q.reshape(B, nb, C_BLOCK, Hkv, G, d)

    def block(i):
        qi = lax.dynamic_index_in_dim(qb, i, axis=1, keepdims=False)
        ki = lax.dynamic_slice_in_dim(kp, i * C_BLOCK, span, axis=1)
        vi = lax.dynamic_slice_in_dim(vp, i * C_BLOCK, span, axis=1)
        qpos = i * C_BLOCK + jnp.arange(C_BLOCK)
        kpos = i * C_BLOCK - C_WINDOW + jnp.arange(span)
        valid = (jnp.abs(kpos[None, :] - qpos[:, None]) <= C_WINDOW) & (kpos >= 0)[None] & (kpos < L)[None]
        s_loc = jnp.einsum("bqhgd,bkhd->bhgqk", qi, ki, preferred_element_type=jnp.float32) * scale
        s_loc = jnp.where(valid, s_loc, NEG)
        s_ctx = jnp.einsum("bqhgd,bkhd->bhgqk", qi, ck, preferred_element_type=jnp.float32) * scale
        p = sink_softmax(jnp.concatenate([s_loc, s_ctx], axis=-1), sink).astype(v.dtype)
        return (jnp.einsum("bhgqk,bkhd->bqhgd", p[..., :span], vi)
                + jnp.einsum("bhgqk,bkhd->bqhgd", p[..., span:], cv))

    o = lax.map(block, jnp.arange(nb))
    return jnp.swapaxes(o, 0, 1).reshape(B, L, Hkv, G, d)


def ab_mixer(h, w_in, pool_w, pool_scale, q_g, k_g, rel_bias, w_out, ctx_kv):
    B, L, _ = h.shape
    u, q, k, v = jnp.split(h @ w_in, [D_POOL, D_POOL + D_NA, D_POOL + 2 * D_NA], axis=-1)
    q = rms_norm(q.reshape(B, L, NA_HEADS, HEAD_DIM), q_g)
    k = rms_norm(k.reshape(B, L, NA_HEADS, HEAD_DIM), k_g)
    v = v.reshape(B, L, NA_HEADS, HEAD_DIM)
    y_pool = pool_mixer(u, pool_w, pool_scale)
    if ctx_kv is None:
        o = dense_attention(q[:, :, :, None], k, v, None)[:, :, :, 0]
    else:
        o = neighbourhood_attention(q, k, v, ctx_kv[0], ctx_kv[1], rel_bias)
    y = jnp.concatenate([y_pool, o.reshape(B, L, D_NA)], axis=-1) @ w_out
    return y, k, v


def win_mixer(h, w_in, q_g, k_g, sink, w_out, ctx_kv):
    B, L, _ = h.shape
    dkv = C_KV_HEADS * HEAD_DIM
    q, k, v = jnp.split(h @ w_in, [D_MODEL, D_MODEL + dkv], axis=-1)
    q = rms_norm(q.reshape(B, L, C_KV_HEADS, C_GROUP, HEAD_DIM), q_g)
    k = rms_norm(k.reshape(B, L, C_KV_HEADS, HEAD_DIM), k_g)
    v = v.reshape(B, L, C_KV_HEADS, HEAD_DIM)
    sink_b = sink.reshape(1, C_KV_HEADS, C_GROUP, 1, 1)
    if ctx_kv is None:
        o = dense_attention(q, k, v, sink_b)
    else:
        o = window_attention(rope_2d(q), rope_2d(k), v, ctx_kv[0], ctx_kv[1], sink_b)
    return o.reshape(B, L, D_MODEL) @ w_out, k, v


def setup_inputs(seed: int = 0) -> dict:
    key = jax.random.key(seed)
    ks = jax.random.split(key, 32)

    def nrm(i, shape, s):
        return jax.random.normal(ks[i], shape, jnp.float32) * s

    d_in_ab = D_POOL + 3 * D_NA
    d_in_win = D_MODEL + 2 * C_KV_HEADS * HEAD_DIM
    return {
        "x_prompt": nrm(0, (BATCH, SEQ, D_MODEL), 1.0),
        "x_sample": nrm(1, (DEC_BATCH, DEC_SEQ, D_MODEL), 1.0),
        "cache_na_k": nrm(2, (DEC_BATCH, N_EVEN, PAST_LEN, NA_HEADS, HEAD_DIM), 1.0),
        "cache_na_v": nrm(3, (DEC_BATCH, N_EVEN, PAST_LEN, NA_HEADS, HEAD_DIM), 1.0),
        "cache_win_k": nrm(4, (DEC_BATCH, N_ODD, PAST_LEN, C_KV_HEADS, HEAD_DIM), 1.0),
        "cache_win_v": nrm(5, (DEC_BATCH, N_ODD, PAST_LEN, C_KV_HEADS, HEAD_DIM), 1.0),
        "c": nrm(6, (DEC_BATCH, D_MODEL), 1.0),
        "c_ctx": nrm(7, (D_MODEL,), 1.0),
        "norm_g": 1.0 + nrm(8, (DEPTH, N_SUB, D_MODEL), 0.02),
        "mod_w": nrm(9, (DEPTH, D_MODEL, 3 * N_SUB * D_MODEL), 0.5 * D_MODEL ** -0.5),
        "mod_b": nrm(10, (DEPTH, 3 * N_SUB * D_MODEL), 0.02),
        "ffn_w_in": nrm(11, (DEPTH, 2, D_MODEL, 2 * D_FF), D_MODEL ** -0.5),
        "ffn_w_out": nrm(12, (DEPTH, 2, D_FF, D_MODEL), D_FF ** -0.5),
        "ab_w_in": nrm(13, (N_EVEN, D_MODEL, d_in_ab), D_MODEL ** -0.5),
        "pool_w": nrm(14, (N_EVEN, N_POOL_GROUPS, POOL_GROUP, POOL_GROUP), POOL_GROUP ** -0.5),
        "pool_scale": 1.0 + nrm(15, (N_EVEN, D_POOL), 0.02),
        "na_q_g": 1.0 + nrm(16, (N_EVEN, HEAD_DIM), 0.02),
        "na_k_g": 1.0 + nrm(17, (N_EVEN, HEAD_DIM), 0.02),
        "na_rel_bias": nrm(18, (N_EVEN, NA_HEADS, 2 * NA_ROWS - 1, 2 * NA_COLS - 1), 0.5),
        "ab_w_out": nrm(19, (N_EVEN, D_POOL + D_NA, D_MODEL), (D_POOL + D_NA) ** -0.5),
        "win_w_in": nrm(20, (N_ODD, D_MODEL, d_in_win), D_MODEL ** -0.5),
        "win_q_g": 1.0 + nrm(21, (N_ODD, HEAD_DIM), 0.02),
        "win_k_g": 1.0 + nrm(22, (N_ODD, HEAD_DIM), 0.02),
        "win_sink": nrm(23, (N_ODD, C_HEADS), 1.0),
        "win_w_out": nrm(24, (N_ODD, D_MODEL, D_MODEL), D_MODEL ** -0.5),
    }


def reference(x_prompt, x_sample, cache_na_k, cache_na_v, cache_win_k, cache_win_v, c, c_ctx,
              norm_g, mod_w, mod_b, ffn_w_in, ffn_w_out, ab_w_in, pool_w, pool_scale,
              na_q_g, na_k_g, na_rel_bias, ab_w_out, win_w_in, win_q_g, win_k_g, win_sink, win_w_out):
    yp, ys = x_prompt, x_sample
    na_k, na_v, win_k, win_v = [], [], [], []
    for li in range(DEPTH):
        m_ctx = modulation(c_ctx[None], mod_w[li], mod_b[li])
        m_lat = modulation(c, mod_w[li], mod_b[li])
        yp = ffn_sublayer(yp, norm_g[li, 0], m_ctx, 0, ffn_w_in[li, 0], ffn_w_out[li, 0])
        ys = ffn_sublayer(ys, norm_g[li, 0], m_lat, 0, ffn_w_in[li, 0], ffn_w_out[li, 0])
        hp = adaln(yp, norm_g[li, 1], m_ctx, 1)
        hs = adaln(ys, norm_g[li, 1], m_lat, 1)
        j = li // 2
        if li % 2 == 0:
            wts = (ab_w_in[j], pool_w[j], pool_scale[j], na_q_g[j], na_k_g[j], na_rel_bias[j], ab_w_out[j])
            op, kc, vc = ab_mixer(hp, *wts, None)
            os_, _, _ = ab_mixer(hs, *wts, (cache_na_k[:, j], cache_na_v[:, j]))
            na_k.append(kc)
            na_v.append(vc)
        else:
            wts = (win_w_in[j], win_q_g[j], win_k_g[j], win_sink[j], win_w_out[j])
            op, kc, vc = win_mixer(hp, *wts, None)
            os_, _, _ = win_mixer(hs, *wts, (cache_win_k[:, j], cache_win_v[:, j]))
            win_k.append(kc)
            win_v.append(vc)
        yp = yp + m_ctx[:, :, 5] * op
        ys = ys + m_lat[:, :, 5] * os_
        yp = ffn_sublayer(yp, norm_g[li, 2], m_ctx, 2, ffn_w_in[li, 1], ffn_w_out[li, 1])
        ys = ffn_sublayer(ys, norm_g[li, 2], m_lat, 2, ffn_w_in[li, 1], ffn_w_out[li, 1])
    return (yp, ys, jnp.stack(na_k, axis=1), jnp.stack(na_v, axis=1), jnp.stack(win_k, axis=1), jnp.stack(win_v, axis=1))
```

```python
import functools

import jax
import jax.numpy as jnp
from jax import lax
from jax.experimental import pallas as pl
from jax.experimental.pallas import tpu as pltpu

EPS = 1e-6
NEG = -1e30
HEAD_DIM = 128
GRID_W = 64
POOL_WINDOWS = (2, 4, 8, 16)
NA_ROWS = 8
NA_COLS = 16
C_WINDOW = 128
C_BLOCK = 128
ROPE_BASE = 10000.0
N_SUB = 3
N_MOD = 3 * N_SUB

SEQ_BLOCK = 2048
NA_QROWS = 4
NA_KROWS = 12
MOD_ROWS = 16
MIB = 1024 * 1024
BF16 = jnp.bfloat16
F32 = jnp.float32


def _cparams(sem, vmem_mib):
    return pltpu.CompilerParams(dimension_semantics=sem, vmem_limit_bytes=vmem_mib * MIB)


def _mod_kernel(c_ref, w_ref, b_ref, o_ref):
    c = c_ref[...]
    a = (c * jax.nn.sigmoid(c)).astype(BF16)
    w = w_ref[...].astype(BF16)
    o_ref[...] = jnp.dot(a, w, preferred_element_type=F32) + b_ref[...]


def _modulation(c_all, mod_w, mod_b, tn=512):
    depth, d, n = mod_w.shape
    return pl.pallas_call(
        _mod_kernel,
        grid=(depth, n // tn),
        in_specs=[
            pl.BlockSpec((MOD_ROWS, d), lambda l, j: (0, 0)),
            pl.BlockSpec((None, d, tn), lambda l, j: (l, 0, j)),
            pl.BlockSpec((None, 1, tn), lambda l, j: (l, 0, j)),
        ],
        out_specs=pl.BlockSpec((None, MOD_ROWS, tn), lambda l, j: (l, 0, j)),
        out_shape=jax.ShapeDtypeStruct((depth, MOD_ROWS, n), F32),
        compiler_params=_cparams(("arbitrary", "arbitrary"), 40),
        name="modulation",
    )(c_all, mod_w, mod_b.reshape(depth, 1, n))


def _adaln_rows(x_ref, mod_ref, g_ref, h_ref, sub, rows):
    shift = mod_ref[0, 3 * sub:3 * sub + 1, :]
    scale1 = 1.0 + mod_ref[0, 3 * sub + 1:3 * sub + 2, :]
    gain = g_ref[...]

    def body(c, carry):
        r = pl.multiple_of(c * rows, rows)
        x = x_ref[pl.ds(r, rows), :]
        ms = jnp.mean(x * x, axis=-1, keepdims=True)
        y = (x * lax.rsqrt(ms + EPS)) * gain
        h_ref[pl.ds(r, rows), :] = (y * scale1 + shift).astype(h_ref.dtype)
        return carry

    lax.fori_loop(0, x_ref.shape[0] // rows, body, 0)


def _swiglu_kernel(x_ref, mod_ref, g_ref, wg_ref, wu_ref, o_ref, h_ref, *, sub):
    @pl.when(pl.program_id(1) == 0)
    def _():
        _adaln_rows(x_ref, mod_ref, g_ref, h_ref, sub, 128)

    h = h_ref[...]
    gate = jnp.dot(h, wg_ref[...], preferred_element_type=F32)
    up = jnp.dot(h, wu_ref[...], preferred_element_type=F32)
    o_ref[...] = ((gate * jax.nn.sigmoid(gate)) * up).astype(o_ref.dtype)


def _ffn_in(x, mod, gain, w_in, *, sub, tm, tn):
    t, d = x.shape
    dff = w_in.shape[1] // 2
    nj = dff // tn
    per = SEQ_BLOCK // tm
    return pl.pallas_call(
        functools.partial(_swiglu_kernel, sub=sub),
        grid=(t // tm, nj),
        in_specs=[
            pl.BlockSpec((tm, d), lambda i, j: (i, 0), pipeline_mode=pl.Buffered(1)),
            pl.BlockSpec((1, N_MOD, d), lambda i, j: (i // per, 0, 0)),
            pl.BlockSpec((1, d), lambda i, j: (0, 0)),
            pl.BlockSpec((d, tn), lambda i, j: (0, j)),
            pl.BlockSpec((d, tn), lambda i, j: (0, j + nj)),
        ],
        out_specs=pl.BlockSpec((tm, tn), lambda i, j: (i, j)),
        out_shape=jax.ShapeDtypeStruct((t, dff), BF16),
        scratch_shapes=[pltpu.VMEM((tm, d), BF16)],
        compiler_params=_cparams(("arbitrary", "arbitrary"), 48),
        name="ffn_in",
    )(x, mod, gain, w_in, w_in)


def _headnorm_kernel(x_ref, mod_ref, g_ref, w_ref, hg_ref, o_ref, h_ref, *, sub, norm_tiles):
    j = pl.program_id(1)

    @pl.when(j == 0)
    def _():
        _adaln_rows(x_ref, mod_ref, g_ref, h_ref, sub, 128)

    y = jnp.dot(h_ref[...], w_ref[...], preferred_element_type=F32)

    @pl.when(j < norm_tiles)
    def _():
        for c in range(y.shape[1] // HEAD_DIM):
            sl = slice(c * HEAD_DIM, (c + 1) * HEAD_DIM)
            yc = y[:, sl]
            ms = jnp.mean(yc * yc, axis=-1, keepdims=True)
            o_ref[:, sl] = (yc * lax.rsqrt(ms + EPS)) * hg_ref[:, sl]

    @pl.when(j >= norm_tiles)
    def _():
        o_ref[...] = y


def _proj_in(x, mod, gain, w, head_gain, *, sub, tm, tn, norm_cols):
    t, d = x.shape
    n = w.shape[1]
    per = SEQ_BLOCK // tm
    return pl.pallas_call(
        functools.partial(_headnorm_kernel, sub=sub, norm_tiles=norm_cols // tn),
        grid=(t // tm, n // tn),
        in_specs=[
            pl.BlockSpec((tm, d), lambda i, j: (i, 0), pipeline_mode=pl.Buffered(1)),
            pl.BlockSpec((1, N_MOD, d), lambda i, j: (i // per, 0, 0)),
            pl.BlockSpec((1, d), lambda i, j: (0, 0)),
            pl.BlockSpec((d, tn), lambda i, j: (0, j)),
            pl.BlockSpec((1, tn), lambda i, j: (0, j)),
        ],
        out_specs=pl.BlockSpec((tm, tn), lambda i, j: (i, j)),
        out_shape=jax.ShapeDtypeStruct((t, n), F32),
        scratch_shapes=[pltpu.VMEM((tm, d), BF16)],
        compiler_params=_cparams(("arbitrary", "arbitrary"), 48),
        name="proj_in",
    )(x, mod, gain, w, head_gain)


def _resid_kernel(*refs, n_parts, gate_row, mult):
    a_refs = refs[:n_parts]
    w_refs = refs[n_parts:2 * n_parts]
    x_ref, mod_ref, o_ref = refs[2 * n_parts:]
    acc = jnp.dot(a_refs[0][...], w_refs[0][...], preferred_element_type=F32)
    for a_ref, w_ref in zip(a_refs[1:], w_refs[1:]):
        acc = acc + jnp.dot(a_ref[...], w_ref[...], preferred_element_type=F32)
    gate = mod_ref[0, gate_row:gate_row + 1, :]
    if mult != 1.0:
        gate = mult * gate
    o_ref[...] = x_ref[...] + gate * acc


def _matmul_resid(parts, w, x, mod, *, gate_row, mult, tm, tn, single_buffer_a, vmem_mib):
    t, d = x.shape
    per = SEQ_BLOCK // tm
    a_mode = dict(pipeline_mode=pl.Buffered(1)) if single_buffer_a else {}
    in_specs, offs, off = [], [], 0
    for a in parts:
        in_specs.append(pl.BlockSpec((tm, a.shape[1]), lambda i, j: (i, 0), **a_mode))
        offs.append(off)
        off += a.shape[1]
    for a, o in zip(parts, offs):
        k = a.shape[1]
        in_specs.append(pl.BlockSpec((k, tn), functools.partial(lambda i, j, kb: (kb, j), kb=o // k)))
    in_specs += [
        pl.BlockSpec((tm, tn), lambda i, j: (i, j)),
        pl.BlockSpec((1, N_MOD, tn), lambda i, j: (i // per, 0, j)),
    ]
    return pl.pallas_call(
        functools.partial(_resid_kernel, n_parts=len(parts), gate_row=gate_row, mult=mult),
        grid=(t // tm, d // tn),
        in_specs=in_specs,
        out_specs=pl.BlockSpec((tm, tn), lambda i, j: (i, j)),
        out_shape=jax.ShapeDtypeStruct((t, d), F32),
        compiler_params=_cparams(("arbitrary", "arbitrary"), vmem_mib),
        name="matmul_resid",
    )(*parts, *([w] * len(parts)), x, mod)


def _pool_kernel(u_ref, w_ref, s_ref, o_ref, *, n_sample_blocks, prompt_seq):
    i = pl.program_id(0)
    g = pl.program_id(1)
    rows = u_ref.shape[0]
    seq_len = jnp.where(i < n_sample_blocks, rows, prompt_seq)
    pos = lax.broadcasted_iota(jnp.int32, (rows, 1), 0) & (seq_len - 1)

    for gi, win in enumerate(POOL_WINDOWS):
        @pl.when(g == gi)
        def _(win=win):
            u = u_ref[...]
            acc = jnp.zeros_like(u)
            cnt = jnp.zeros((rows, 1), F32)
            for k in range(-(win // 2), win - win // 2):
                nb = pos + k
                valid = (nb >= 0) & (nb < seq_len)
                shifted = u if k == 0 else pltpu.roll(u, (-k) % rows, 0)
                acc = acc + jnp.where(valid, shifted, 0.0)
                cnt = cnt + valid.astype(F32)
            diff = (acc / cnt - u).astype(BF16)
            y = jnp.dot(diff, w_ref[...], preferred_element_type=F32)
            o_ref[...] = (y * s_ref[...]).astype(o_ref.dtype)


def _pool_mixer(proj, pool_w, pool_scale, *, u_col, n_sample_blocks, prompt_seq):
    t = proj.shape[0]
    n_groups, cg, _ = pool_w.shape
    g0 = u_col // cg
    return pl.pallas_call(
        functools.partial(_pool_kernel, n_sample_blocks=n_sample_blocks, prompt_seq=prompt_seq),
        grid=(t // SEQ_BLOCK, n_groups),
        in_specs=[
            pl.BlockSpec((SEQ_BLOCK, cg), lambda i, g: (i, g0 + g)),
            pl.BlockSpec((None, cg, cg), lambda i, g: (g, 0, 0)),
            pl.BlockSpec((1, cg), lambda i, g: (0, g)),
        ],
        out_specs=pl.BlockSpec((SEQ_BLOCK, cg), lambda i, g: (i, g)),
        out_shape=jax.ShapeDtypeStruct((t, n_groups * cg), BF16),
        compiler_params=_cparams(("arbitrary", "arbitrary"), 48),
        name="pool_mixer",
    )(proj, pool_w, pool_scale)


def _qkt(q, k):
    return lax.dot_general(q, k, (((1,), (1,)), ((), ())), preferred_element_type=F32)


def _softmax_pv(s_list, v_list, sink=None):
    m = s_list[0].max(axis=-1, keepdims=True)
    for s in s_list[1:]:
        m = jnp.maximum(m, s.max(axis=-1, keepdims=True))
    if sink is not None:
        m = jnp.maximum(m, sink)
    l = None
    o = None
    for s, v in zip(s_list, v_list):
        p = jnp.exp(s - m)
        ls = p.sum(axis=-1, keepdims=True)
        os_ = jnp.dot(p.astype(BF16), v, preferred_element_type=F32)
        l = ls if l is None else l + ls
        o = os_ if o is None else o + os_
    if sink is not None:
        l = l + jnp.exp(sink - m)
    return o / l


def _dense_attn_kernel(*refs, n_kv, groups, has_sink):
    if has_sink:
        sink_ref, q_ref, k_ref, v_ref, o_ref = refs
    else:
        q_ref, k_ref, v_ref, o_ref = refs
    seq = q_ref.shape[0]
    scale = HEAD_DIM ** -0.5
    for h in range(n_kv):
        ks = slice(h * HEAD_DIM, (h + 1) * HEAD_DIM)
        k = k_ref[:, ks].astype(BF16)
        v = v_ref[:, ks].astype(BF16)
        qs = [q_ref[:, (h * groups + g) * HEAD_DIM:(h * groups + g + 1) * HEAD_DIM] for g in range(groups)]
        q = (qs[0] if groups == 1 else jnp.concatenate(qs, axis=0)).astype(BF16)
        s = _qkt(q, k) * scale
        sink = None
        if has_sink:
            sink = jnp.concatenate(
                [jnp.full((seq, 1), sink_ref[h * groups + g], F32) for g in range(groups)], axis=0)
        o = _softmax_pv([s], [v], sink)
        for g in range(groups):
            c0 = (h * groups + g) * HEAD_DIM
            o_ref[:, c0:c0 + HEAD_DIM] = o[g * seq:(g + 1) * seq].astype(o_ref.dtype)


def _dense_attention(proj, sink, *, row0, n_seq, seq, q_col, k_col, v_col, n_kv, groups):
    dq = n_kv * groups * HEAD_DIM
    dkv = n_kv * HEAD_DIM
    rb = row0 // seq
    in_specs = [
        pl.BlockSpec((seq, dq), lambda b: (rb + b, q_col // dq)),
        pl.BlockSpec((seq, dkv), lambda b: (rb + b, k_col // dkv)),
        pl.BlockSpec((seq, dkv), lambda b: (rb + b, v_col // dkv)),
    ]
    args = [proj, proj, proj]
    if sink is not None:
        in_specs = [pl.BlockSpec(memory_space=pltpu.SMEM)] + in_specs
        args = [sink] + args
    return pl.pallas_call(
        functools.partial(_dense_attn_kernel, n_kv=n_kv, groups=groups, has_sink=sink is not None),
        grid=(n_seq,),
        in_specs=in_specs,
        out_specs=pl.BlockSpec((seq, dq), lambda b: (b, 0)),
        out_shape=jax.ShapeDtypeStruct((n_seq * seq, dq), BF16),
        compiler_params=_cparams(("arbitrary",), 32),
        name="dense_attention",
    )(*args)


def _na_block_start(qb, rows):
    return min(max(qb * NA_QROWS - NA_ROWS // 2, 0), rows - NA_KROWS)


def _na_bias_tables(rel_bias, rows):
    n_qb = rows // NA_QROWS
    tables = []
    for qb in (0, 1, n_qb - 1):
        r = qb * NA_QROWS + jnp.arange(NA_QROWS)[:, None, None, None]
        qc = jnp.arange(GRID_W)[None, :, None, None]
        kr = _na_block_start(qb, rows) + jnp.arange(NA_KROWS)[None, None, :, None]
        kc = jnp.arange(GRID_W)[None, None, None, :]
        rs = jnp.clip(r - NA_ROWS // 2, 0, rows - NA_ROWS)
        ws = jnp.clip(qc - NA_COLS // 2, 0, GRID_W - NA_COLS)
        valid = (kr >= rs) & (kr < rs + NA_ROWS) & (kc >= ws) & (kc < ws + NA_COLS)
        drow = jnp.clip(kr - r + NA_ROWS - 1, 0, 2 * NA_ROWS - 2)
        dcol = jnp.clip(kc - qc + NA_COLS - 1, 0, 2 * NA_COLS - 2)
        drow, dcol, valid = jnp.broadcast_arrays(drow, dcol, valid)
        b = jnp.where(valid[None], rel_bias[:, drow, dcol].astype(F32), NEG)
        tables.append(b.reshape(rel_bias.shape[0], NA_QROWS * GRID_W, NA_KROWS * GRID_W))
    return jnp.stack(tables, axis=1)


def _na_kernel(q_ref, k_ref, v_ref, ck_ref, cv_ref, bias_ref, o_ref, kb_ref, vb_ref):
    rows = q_ref.shape[0] // GRID_W
    n_qb = rows // NA_QROWS
    qn = NA_QROWS * GRID_W
    kn = NA_KROWS * GRID_W
    scale = HEAD_DIM ** -0.5
    kb_ref[...] = k_ref[...].astype(BF16)
    vb_ref[...] = v_ref[...].astype(BF16)
    ck = ck_ref[0].astype(BF16)
    cv = cv_ref[0].astype(BF16)
    for qb in range(n_qb):
        k0 = _na_block_start(qb, rows) * GRID_W
        kind = 0 if qb == 0 else (2 if qb == n_qb - 1 else 1)
        q = q_ref[qb * qn:(qb + 1) * qn, :].astype(BF16)
        s_loc = _qkt(q, kb_ref[k0:k0 + kn, :]) * scale + bias_ref[0, kind]
        s_ctx = _qkt(q, ck) * scale
        o = _softmax_pv([s_loc, s_ctx], [vb_ref[k0:k0 + kn, :], cv])
        o_ref[qb * qn:(qb + 1) * qn, :] = o.astype(o_ref.dtype)


def _na_attention(proj, ck, cv, bias, *, n_batch, seq, n_heads, q_col, k_col, v_col):
    qn, kn = NA_QROWS * GRID_W, NA_KROWS * GRID_W
    qo, ko, vo = q_col // HEAD_DIM, k_col // HEAD_DIM, v_col // HEAD_DIM
    return pl.pallas_call(
        _na_kernel,
        grid=(n_heads, n_batch),
        in_specs=[
            pl.BlockSpec((seq, HEAD_DIM), lambda h, b: (b, qo + h)),
            pl.BlockSpec((seq, HEAD_DIM), lambda h, b: (b, ko + h)),
            pl.BlockSpec((seq, HEAD_DIM), lambda h, b: (b, vo + h)),
            pl.BlockSpec((1, ck.shape[1], HEAD_DIM), lambda h, b: (b, 0, h)),
            pl.BlockSpec((1, cv.shape[1], HEAD_DIM), lambda h, b: (b, 0, h)),
            pl.BlockSpec((1, 3, qn, kn), lambda h, b: (h, 0, 0, 0)),
        ],
        out_specs=pl.BlockSpec((seq, HEAD_DIM), lambda h, b: (b, h)),
        out_shape=jax.ShapeDtypeStruct((n_batch * seq, n_heads * HEAD_DIM), BF16),
        scratch_shapes=[pltpu.VMEM((seq, HEAD_DIM), BF16), pltpu.VMEM((seq, HEAD_DIM), BF16)],
        compiler_params=_cparams(("arbitrary", "arbitrary"), 32),
        name="na_attention",
    )(proj, proj, proj, ck, cv, bias)


def _rope_tables(seq):
    half = HEAD_DIM // 2
    quarter = half // 2
    t = jnp.arange(seq)
    freqs = ROPE_BASE ** (-jnp.arange(quarter, dtype=F32) * 2.0 / half)
    ang_r = (t // GRID_W).astype(F32)[:, None] * freqs[None]
    ang_c = (t % GRID_W).astype(F32)[:, None] * freqs[None]
    cos = jnp.concatenate([jnp.cos(ang_r)] * 2 + [jnp.cos(ang_c)] * 2, axis=-1)
    sin = jnp.concatenate([-jnp.sin(ang_r), jnp.sin(ang_r), -jnp.sin(ang_c), jnp.sin(ang_c)], axis=-1)
    return cos, sin


def _rope(x, cos, sin):
    quarter = HEAD_DIM // 4
    lane = lax.broadcasted_iota(jnp.int32, x.shape, 1)
    first = (lane & (2 * quarter - 1)) < quarter
    partner = jnp.where(first, pltpu.roll(x, HEAD_DIM - quarter, 1), pltpu.roll(x, quarter, 1))
    return x * cos + partner * sin


def _win_kernel(sink_ref, q_ref, k_ref, v_ref, ck_ref, cv_ref, cos_ref, sin_ref, o_ref, kb_ref, vb_ref,
                *, groups):
    hkv = pl.program_id(1)
    seq = q_ref.shape[0]
    span = C_BLOCK + 2 * C_WINDOW
    scale = HEAD_DIM ** -0.5
    kb_ref[...] = _rope(k_ref[...], cos_ref[...], sin_ref[...]).astype(BF16)
    vb_ref[...] = v_ref[...].astype(BF16)
    ck = ck_ref[0].astype(BF16)
    cv = cv_ref[0].astype(BF16)
    sink = jnp.concatenate(
        [jnp.full((C_BLOCK, 1), sink_ref[hkv * groups + g], F32) for g in range(groups)], axis=0)
    qrow = lax.broadcasted_iota(jnp.int32, (groups * C_BLOCK, span), 0) & (C_BLOCK - 1)
    kcol = lax.broadcasted_iota(jnp.int32, (groups * C_BLOCK, span), 1)
    for i in range(seq // C_BLOCK):
        q0 = i * C_BLOCK
        k0 = min(max(q0 - C_WINDOW, 0), seq - span)
        cos = cos_ref[q0:q0 + C_BLOCK, :]
        sin = sin_ref[q0:q0 + C_BLOCK, :]
        q = jnp.concatenate(
            [_rope(q_ref[q0:q0 + C_BLOCK, g * HEAD_DIM:(g + 1) * HEAD_DIM], cos, sin) for g in range(groups)],
            axis=0).astype(BF16)
        valid = jnp.abs(kcol - qrow + (k0 - q0)) <= C_WINDOW
        s_loc = jnp.where(valid, _qkt(q, kb_ref[k0:k0 + span, :]) * scale, NEG)
        s_ctx = _qkt(q, ck) * scale
        o = _softmax_pv([s_loc, s_ctx], [vb_ref[k0:k0 + span, :], cv], sink)
        for g in range(groups):
            o_ref[q0:q0 + C_BLOCK, g * HEAD_DIM:(g + 1) * HEAD_DIM] = (
                o[g * C_BLOCK:(g + 1) * C_BLOCK].astype(o_ref.dtype))


def _win_attention(proj, ck, cv, sink, *, n_batch, seq, n_kv, groups, k_col, v_col):
    cos, sin = _rope_tables(seq)
    ko, vo = k_col // HEAD_DIM, v_col // HEAD_DIM
    gw = groups * HEAD_DIM
    return pl.pallas_call(
        functools.partial(_win_kernel, groups=groups),
        grid=(n_batch, n_kv),
        in_specs=[
            pl.BlockSpec(memory_space=pltpu.SMEM),
            pl.BlockSpec((seq, gw), lambda b, h: (b, h)),
            pl.BlockSpec((seq, HEAD_DIM), lambda b, h: (b, ko + h)),
            pl.BlockSpec((seq, HEAD_DIM), lambda b, h: (b, vo + h)),
            pl.BlockSpec((1, ck.shape[1], HEAD_DIM), lambda b, h: (b, 0, h)),
            pl.BlockSpec((1, cv.shape[1], HEAD_DIM), lambda b, h: (b, 0, h)),
            pl.BlockSpec((seq, HEAD_DIM), lambda b, h: (0, 0)),
            pl.BlockSpec((seq, HEAD_DIM), lambda b, h: (0, 0)),
        ],
        out_specs=pl.BlockSpec((seq, gw), lambda b, h: (b, h)),
        out_shape=jax.ShapeDtypeStruct((n_batch * seq, n_kv * gw), BF16),
        scratch_shapes=[pltpu.VMEM((seq, HEAD_DIM), BF16), pltpu.VMEM((seq, HEAD_DIM), BF16)],
        compiler_params=_cparams(("arbitrary", "arbitrary"), 32),
        name="win_attention",
    )(sink, proj, proj, proj, ck, cv, cos, sin)


def kernel(x_prompt, x_sample, cache_na_k, cache_na_v, cache_win_k, cache_win_v, c, c_ctx,
           norm_g, mod_w, mod_b, ffn_w_in, ffn_w_out, ab_w_in, pool_w, pool_scale,
           na_q_g, na_k_g, na_rel_bias, ab_w_out, win_w_in, win_q_g, win_k_g, win_sink, win_w_out):
    n_prompt, seq, d = x_prompt.shape
    n_dec, dec_seq, _ = x_sample.shape
    depth = norm_g.shape[0]
    assert dec_seq == SEQ_BLOCK and SEQ_BLOCK % seq == 0 and (n_prompt * seq) % SEQ_BLOCK == 0
    assert n_dec + 1 <= MOD_ROWS
    ts, tp = n_dec * dec_seq, n_prompt * seq
    n_prompt_blocks = tp // SEQ_BLOCK
    rows = dec_seq // GRID_W
    tm = 1024

    x = jnp.concatenate([x_sample.reshape(ts, d), x_prompt.reshape(tp, d)], axis=0)

    c_all = jnp.concatenate([c, c_ctx[None], jnp.zeros((MOD_ROWS - n_dec - 1, d), F32)], axis=0)
    m_all = _modulation(c_all, mod_w, mod_b).reshape(depth, MOD_ROWS, N_MOD, d)
    mod = jnp.concatenate(
        [m_all[:, :n_dec], jnp.broadcast_to(m_all[:, n_dec:n_dec + 1], (depth, n_prompt_blocks, N_MOD, d))], axis=1)

    ffn_in_w = ffn_w_in.astype(BF16)
    ffn_out_w = ffn_w_out.astype(BF16)

    def ffn(x, li, which, sub):
        hidden = _ffn_in(x, mod[li], norm_g[li, sub][None], ffn_in_w[li, which], sub=sub, tm=tm, tn=256)
        return _matmul_resid([hidden], ffn_out_w[li, which], x, mod[li], gate_row=3 * sub + 2, mult=0.5,
                             tm=tm, tn=256, single_buffer_a=True, vmem_mib=48)

    na_k, na_v, win_k, win_v = [], [], [], []
    for li in range(depth):
        x = ffn(x, li, 0, 0)
        j = li // 2
        if li % 2 == 0:
            n_heads = cache_na_k.shape[3]
            d_na = n_heads * HEAD_DIM
            d_pool = ab_w_in.shape[2] - 3 * d_na
            w_in = ab_w_in[j].astype(BF16)
            w_in = jnp.concatenate([w_in[:, d_pool:d_pool + 2 * d_na], w_in[:, :d_pool], w_in[:, d_pool + 2 * d_na:]],
                                   axis=1)
            head_gain = jnp.concatenate(
                [jnp.tile(na_q_g[j], n_heads), jnp.tile(na_k_g[j], n_heads), jnp.ones((d_pool + d_na,), F32)])[None]
            q_col, k_col, u_col, v_col = 0, d_na, 2 * d_na, 2 * d_na + d_pool
            proj = _proj_in(x, mod[li], norm_g[li, 1][None], w_in, head_gain, sub=1, tm=tm, tn=512,
                            norm_cols=2 * d_na)
            y_pool = _pool_mixer(proj, pool_w[j].astype(BF16), pool_scale[j][None], u_col=u_col,
                                 n_sample_blocks=n_dec, prompt_seq=seq)
            o_p = _dense_attention(proj, None, row0=ts, n_seq=n_prompt, seq=seq, q_col=q_col, k_col=k_col,
                                   v_col=v_col, n_kv=n_heads, groups=1)
            bias = _na_bias_tables(na_rel_bias[j], rows)
            o_s = _na_attention(proj, cache_na_k[:, j].reshape(n_dec, -1, d_na),
                                cache_na_v[:, j].reshape(n_dec, -1, d_na), bias, n_batch=n_dec, seq=dec_seq,
                                n_heads=n_heads, q_col=q_col, k_col=k_col, v_col=v_col)
            o = jnp.concatenate([o_s, o_p], axis=0)
            x = _matmul_resid([y_pool, o], ab_w_out[j].astype(BF16), x, mod[li], gate_row=5, mult=1.0,
                              tm=tm, tn=512, single_buffer_a=False, vmem_mib=48)
            na_k.append(proj[ts:, k_col:k_col + d_na].reshape(n_prompt, seq, n_heads, HEAD_DIM))
            na_v.append(proj[ts:, v_col:v_col + d_na].reshape(n_prompt, seq, n_heads, HEAD_DIM))
        else:
            n_kv = cache_win_k.shape[3]
            dkv = n_kv * HEAD_DIM
            groups = d // dkv
            head_gain = jnp.concatenate(
                [jnp.tile(win_q_g[j], n_kv * groups), jnp.tile(win_k_g[j], n_kv), jnp.ones((dkv,), F32)])[None]
            proj = _proj_in(x, mod[li], norm_g[li, 1][None], win_w_in[j].astype(BF16), head_gain, sub=1, tm=tm,
                            tn=512 if (d + dkv) % 512 == 0 else 256, norm_cols=d + dkv)
            o_p = _dense_attention(proj, win_sink[j], row0=ts, n_seq=n_prompt, seq=seq, q_col=0, k_col=d,
                                   v_col=d + dkv, n_kv=n_kv, groups=groups)
            o_s = _win_attention(proj, cache_win_k[:, j].reshape(n_dec, -1, dkv),
                                 cache_win_v[:, j].reshape(n_dec, -1, dkv), win_sink[j], n_batch=n_dec,
                                 seq=dec_seq, n_kv=n_kv, groups=groups, k_col=d, v_col=d + dkv)
            o = jnp.concatenate([o_s, o_p], axis=0)
            x = _matmul_resid([o], win_w_out[j].astype(BF16), x, mod[li], gate_row=5, mult=1.0,
                              tm=tm, tn=512, single_buffer_a=False, vmem_mib=48)
            win_k.append(proj[ts:, d:d + dkv].reshape(n_prompt, seq, n_kv, HEAD_DIM))
            win_v.append(proj[ts:, d + dkv:].reshape(n_prompt, seq, n_kv, HEAD_DIM))
        x = ffn(x, li, 1, 2)

    ys = x[:ts].reshape(n_dec, dec_seq, d)
    yp = x[ts:].reshape(n_prompt, seq, d)
    return (yp, ys, jnp.stack(na_k, axis=1), jnp.stack(na_v, axis=1), jnp.stack(win_k, axis=1),
            jnp.stack(win_v, axis=1))
```

```python
import functools

import jax
import jax.numpy as jnp
from jax import lax
from jax.experimental import pallas as pl
from jax.experimental.pallas import tpu as pltpu

EPS = 1e-6
NEG = -1e30
HEAD_DIM = 128
LANES = 128
GRID_W = 64
POOL_WINDOWS = (2, 4, 8, 16)
NA_ROWS = 8
NA_COLS = 16
C_WINDOW = 128
C_BLOCK = 128
ROPE_BASE = 10000.0
N_SUB = 3
N_MOD = 3 * N_SUB

SEQ_BLOCK = 2048
ROW_TILE = 1024
STAT_ROWS = 128
NORM_ROWS = 16
NORM_COLS = 1024
NA_QROWS = 4
NA_KROWS = 12
MOD_ROWS = 16
MIB = 1024 * 1024
BF16 = jnp.bfloat16
F32 = jnp.float32


def _cparams(sem, vmem_mib):
    return pltpu.CompilerParams(dimension_semantics=sem, vmem_limit_bytes=vmem_mib * MIB)


def _alias_args(alias, n_inputs):
    if alias is None:
        return [], [], {}
    return [alias], [pl.BlockSpec(memory_space=pl.ANY)], {n_inputs: 0}


def _mod_kernel(c_ref, w_ref, b_ref, o_ref):
    c = c_ref[...]
    a = (c * jax.nn.sigmoid(c)).astype(BF16)
    w = w_ref[...].astype(BF16)
    o_ref[...] = jnp.dot(a, w, preferred_element_type=F32) + b_ref[...]


def _modulation(c_all, mod_w, mod_b, tn=512):
    depth, d, n = mod_w.shape
    return pl.pallas_call(
        _mod_kernel,
        grid=(depth, n // tn),
        in_specs=[
            pl.BlockSpec((MOD_ROWS, d), lambda l, j: (0, 0)),
            pl.BlockSpec((None, d, tn), lambda l, j: (l, 0, j)),
            pl.BlockSpec((None, 1, tn), lambda l, j: (l, 0, j)),
        ],
        out_specs=pl.BlockSpec((None, MOD_ROWS, tn), lambda l, j: (l, 0, j)),
        out_shape=jax.ShapeDtypeStruct((depth, MOD_ROWS, n), F32),
        compiler_params=_cparams(("arbitrary", "arbitrary"), 40),
        name="modulation",
    )(c_all, mod_w, mod_b.reshape(depth, 1, n))


def _adaln_rows(x_ref, ada_ref, h_ref, r_ref):
    rows, d = x_ref.shape
    inv_d = 1.0 / d

    def stats(c, carry):
        r = pl.multiple_of(c * STAT_ROWS, STAT_ROWS)
        x = x_ref[pl.ds(r, STAT_ROWS), :]
        ms = jnp.sum(x * x, axis=-1, keepdims=True) * inv_d
        r_ref[pl.ds(r, STAT_ROWS), :] = jnp.broadcast_to(lax.rsqrt(ms + EPS), (STAT_ROWS, LANES))
        return carry

    def scale(c, carry):
        r = pl.multiple_of(c * NORM_ROWS, NORM_ROWS)
        rinv = jnp.concatenate([r_ref[pl.ds(r, NORM_ROWS), :]] * (NORM_COLS // LANES), axis=1)
        for c0 in range(0, d, NORM_COLS):
            cs = slice(c0, c0 + NORM_COLS)
            gain = jnp.concatenate([ada_ref[0, 0, :, cs]] * (NORM_ROWS // 8), axis=0)
            shift = jnp.concatenate([ada_ref[0, 1, :, cs]] * (NORM_ROWS // 8), axis=0)
            x = x_ref[pl.ds(r, NORM_ROWS), cs]
            h_ref[pl.ds(r, NORM_ROWS), cs] = ((x * rinv) * gain + shift).astype(h_ref.dtype)
        return carry

    lax.fori_loop(0, rows // STAT_ROWS, stats, 0)
    lax.fori_loop(0, rows // NORM_ROWS, scale, 0, unroll=2)


def _swiglu_kernel(x_ref, ada_ref, wg_ref, wu_ref, *rest):
    o_ref, h_ref, r_ref = rest[-3:]

    @pl.when(pl.program_id(1) == 0)
    def _():
        _adaln_rows(x_ref, ada_ref, h_ref, r_ref)

    h = h_ref[...]
    gate = jnp.dot(h, wg_ref[...].astype(BF16), preferred_element_type=F32)
    up = jnp.dot(h, wu_ref[...].astype(BF16), preferred_element_type=F32)
    o_ref[...] = ((gate * jax.nn.sigmoid(gate)) * up).astype(o_ref.dtype)


def _ffn_in(x, ada, w_in, w_sel, *, n_tiles, out_tile0, out_rows, alias=None, tn=256):
    tm = ROW_TILE
    d = x.shape[1]
    dff = w_in.shape[1] // 2
    nj = dff // tn
    per = SEQ_BLOCK // tm
    extra, extra_specs, aliases = _alias_args(alias, 4)
    return pl.pallas_call(
        _swiglu_kernel,
        grid=(n_tiles, nj),
        in_specs=[
            pl.BlockSpec((tm, d), lambda i, j: (i, 0), pipeline_mode=pl.Buffered(1)),
            pl.BlockSpec((1, 2, 8, d), lambda i, j: ((i + out_tile0) // per, 0, 0, 0)),
            pl.BlockSpec((d, tn), lambda i, j: (w_sel, j)),
            pl.BlockSpec((d, tn), lambda i, j: (w_sel, j + nj)),
        ] + extra_specs,
        out_specs=pl.BlockSpec((tm, tn), lambda i, j: (i + out_tile0, j)),
        out_shape=jax.ShapeDtypeStruct((out_rows, dff), BF16),
        scratch_shapes=[pltpu.VMEM((tm, d), BF16), pltpu.VMEM((tm, LANES), F32)],
        input_output_aliases=aliases,
        compiler_params=_cparams(("arbitrary", "arbitrary"), 56),
        name="ffn_in",
    )(x, ada, w_in, w_in, *extra)


def _headnorm_kernel(x_ref, ada_ref, w_ref, hg_ref, o_ref, h_ref, r_ref, *, norm_tiles):
    j = pl.program_id(1)

    @pl.when(j == 0)
    def _():
        _adaln_rows(x_ref, ada_ref, h_ref, r_ref)

    y = jnp.dot(h_ref[...], w_ref[...], preferred_element_type=F32)

    @pl.when(j < norm_tiles)
    def _():
        for c in range(y.shape[1] // HEAD_DIM):
            sl = slice(c * HEAD_DIM, (c + 1) * HEAD_DIM)
            yc = y[:, sl]
            ms = jnp.mean(yc * yc, axis=-1, keepdims=True)
            o_ref[:, sl] = (yc * lax.rsqrt(ms + EPS)) * hg_ref[:, sl]

    @pl.when(j >= norm_tiles)
    def _():
        o_ref[...] = y


def _proj_in(x, ada, w, head_gain, *, tn, norm_cols):
    tm = ROW_TILE
    t, d = x.shape
    n = w.shape[1]
    per = SEQ_BLOCK // tm
    return pl.pallas_call(
        functools.partial(_headnorm_kernel, norm_tiles=norm_cols // tn),
        grid=(t // tm, n // tn),
        in_specs=[
            pl.BlockSpec((tm, d), lambda i, j: (i, 0), pipeline_mode=pl.Buffered(1)),
            pl.BlockSpec((1, 2, 8, d), lambda i, j: (i // per, 0, 0, 0)),
            pl.BlockSpec((d, tn), lambda i, j: (0, j)),
            pl.BlockSpec((1, tn), lambda i, j: (0, j)),
        ],
        out_specs=pl.BlockSpec((tm, tn), lambda i, j: (i, j)),
        out_shape=jax.ShapeDtypeStruct((t, n), F32),
        scratch_shapes=[pltpu.VMEM((tm, d), BF16), pltpu.VMEM((tm, LANES), F32)],
        compiler_params=_cparams(("arbitrary", "arbitrary"), 48),
        name="proj_in",
    )(x, ada, w, head_gain)


def _resid_kernel(*refs, n_parts, gate_row, mult):
    a_refs = refs[:n_parts]
    w_refs = refs[n_parts:2 * n_parts]
    x_ref, mod_ref = refs[2 * n_parts:2 * n_parts + 2]
    o_ref = refs[-1]
    acc = jnp.dot(a_refs[0][...], w_refs[0][...], preferred_element_type=F32)
    for a_ref, w_ref in zip(a_refs[1:], w_refs[1:]):
        acc = acc + jnp.dot(a_ref[...], w_ref[...], preferred_element_type=F32)
    gate = mod_ref[0, gate_row:gate_row + 1, :]
    if mult != 1.0:
        gate = mult * gate
    o_ref[...] = x_ref[...] + gate * acc


def _matmul_resid(parts, w, x, mod, *, gate_row, mult, tn, single_buffer_a, n_tiles, a_tile0, x_tile0,
                  out_tile0, out_rows, w_sel=0, alias=None):
    tm = ROW_TILE
    d = w.shape[1]
    per = SEQ_BLOCK // tm
    a_mode = dict(pipeline_mode=pl.Buffered(1)) if single_buffer_a else {}
    k_total = sum(a.shape[1] for a in parts)
    in_specs, w_specs, off = [], [], w_sel * k_total
    for a in parts:
        k = a.shape[1]
        assert off % k == 0
        in_specs.append(pl.BlockSpec((tm, k), lambda i, j: (i + a_tile0, 0), **a_mode))
        w_specs.append(pl.BlockSpec((k, tn), functools.partial(lambda i, j, kb: (kb, j), kb=off // k)))
        off += k
    in_specs += w_specs + [
        pl.BlockSpec((tm, tn), lambda i, j: (i + x_tile0, j)),
        pl.BlockSpec((1, N_MOD, tn), lambda i, j: ((i + a_tile0) // per, 0, j)),
    ]
    extra, extra_specs, aliases = _alias_args(alias, len(in_specs))
    return pl.pallas_call(
        functools.partial(_resid_kernel, n_parts=len(parts), gate_row=gate_row, mult=mult),
        grid=(n_tiles, d // tn),
        in_specs=in_specs + extra_specs,
        out_specs=pl.BlockSpec((tm, tn), lambda i, j: (i + out_tile0, j)),
        out_shape=jax.ShapeDtypeStruct((out_rows, d), F32),
        input_output_aliases=aliases,
        compiler_params=_cparams(("arbitrary", "arbitrary"), 48),
        name="matmul_resid",
    )(*parts, *([w] * len(parts)), x, mod, *extra)


def _pool_kernel(u_ref, w_ref, s_ref, o_ref, *, n_sample_blocks, prompt_seq):
    i = pl.program_id(0)
    g = pl.program_id(1)
    rows = u_ref.shape[0]
    seq_len = jnp.where(i < n_sample_blocks, rows, prompt_seq)
    pos = lax.broadcasted_iota(jnp.int32, (rows, 1), 0) & (seq_len - 1)

    for gi, win in enumerate(POOL_WINDOWS):
        @pl.when(g == gi)
        def _(win=win):
            u = u_ref[...]
            acc = jnp.zeros_like(u)
            cnt = jnp.zeros((rows, 1), F32)
            for k in range(-(win // 2), win - win // 2):
                nb = pos + k
                valid = (nb >= 0) & (nb < seq_len)
                shifted = u if k == 0 else pltpu.roll(u, (-k) % rows, 0)
                acc = acc + jnp.where(valid, shifted, 0.0)
                cnt = cnt + valid.astype(F32)
            diff = (acc / cnt - u).astype(BF16)
            y = jnp.dot(diff, w_ref[...], preferred_element_type=F32)
            o_ref[...] = (y * s_ref[...]).astype(o_ref.dtype)


def _pool_mixer(proj, pool_w, pool_scale, *, u_col, n_sample_blocks, prompt_seq):
    t = proj.shape[0]
    n_groups, cg, _ = pool_w.shape
    g0 = u_col // cg
    return pl.pallas_call(
        functools.partial(_pool_kernel, n_sample_blocks=n_sample_blocks, prompt_seq=prompt_seq),
        grid=(t // SEQ_BLOCK, n_groups),
        in_specs=[
            pl.BlockSpec((SEQ_BLOCK, cg), lambda i, g: (i, g0 + g)),
            pl.BlockSpec((None, cg, cg), lambda i, g: (g, 0, 0)),
            pl.BlockSpec((1, cg), lambda i, g: (0, g)),
        ],
        out_specs=pl.BlockSpec((SEQ_BLOCK, cg), lambda i, g: (i, g)),
        out_shape=jax.ShapeDtypeStruct((t, n_groups * cg), BF16),
        compiler_params=_cparams(("arbitrary", "arbitrary"), 48),
        name="pool_mixer",
    )(proj, pool_w, pool_scale)


def _qkt(q, k):
    return lax.dot_general(q, k, (((1,), (1,)), ((), ())), preferred_element_type=F32)


def _softmax_pv(s_list, v_list, sink=None):
    m = s_list[0].max(axis=-1, keepdims=True)
    for s in s_list[1:]:
        m = jnp.maximum(m, s.max(axis=-1, keepdims=True))
    if sink is not None:
        m = jnp.maximum(m, sink)
    l = None
    o = None
    for s, v in zip(s_list, v_list):
        p = jnp.exp(s - m)
        ls = p.sum(axis=-1, keepdims=True)
        os_ = jnp.dot(p.astype(BF16), v, preferred_element_type=F32)
        l = ls if l is None else l + ls
        o = os_ if o is None else o + os_
    if sink is not None:
        l = l + jnp.exp(sink - m)
    return o / l


def _dense_attn_kernel(*refs, n_kv, groups, has_sink):
    if has_sink:
        sink_ref, q_ref, k_ref, v_ref = refs[:4]
    else:
        q_ref, k_ref, v_ref = refs[:3]
    o_ref, nk_ref, nv_ref = refs[-3:]
    seq = q_ref.shape[0]
    scale = HEAD_DIM ** -0.5
    nk_ref[...] = k_ref[...]
    nv_ref[...] = v_ref[...]
    for h in range(n_kv):
        ks = slice(h * HEAD_DIM, (h + 1) * HEAD_DIM)
        k = k_ref[:, ks].astype(BF16)
        v = v_ref[:, ks].astype(BF16)
        qs = [q_ref[:, (h * groups + g) * HEAD_DIM:(h * groups + g + 1) * HEAD_DIM] for g in range(groups)]
        q = (qs[0] if groups == 1 else jnp.concatenate(qs, axis=0)).astype(BF16)
        s = _qkt(q, k) * scale
        sink = None
        if has_sink:
            sink = jnp.concatenate(
                [jnp.full((seq, 1), sink_ref[h * groups + g], F32) for g in range(groups)], axis=0)
        o = _softmax_pv([s], [v], sink)
        for g in range(groups):
            c0 = (h * groups + g) * HEAD_DIM
            o_ref[:, c0:c0 + HEAD_DIM] = o[g * seq:(g + 1) * seq].astype(o_ref.dtype)


def _dense_attention(proj, sink, o_buf, *, row0, n_seq, seq, q_col, k_col, v_col, n_kv, groups):
    dq = n_kv * groups * HEAD_DIM
    dkv = n_kv * HEAD_DIM
    rb = row0 // seq
    in_specs = [
        pl.BlockSpec((seq, dq), lambda b: (rb + b, q_col // dq)),
        pl.BlockSpec((seq, dkv), lambda b: (rb + b, k_col // dkv)),
        pl.BlockSpec((seq, dkv), lambda b: (rb + b, v_col // dkv)),
    ]
    args = [proj, proj, proj]
    if sink is not None:
        in_specs = [pl.BlockSpec(memory_space=pltpu.SMEM)] + in_specs
        args = [sink] + args
    extra, extra_specs, aliases = _alias_args(o_buf, len(args))
    kv_shape = jax.ShapeDtypeStruct((n_seq * seq, dkv), F32)
    return pl.pallas_call(
        functools.partial(_dense_attn_kernel, n_kv=n_kv, groups=groups, has_sink=sink is not None),
        grid=(n_seq,),
        in_specs=in_specs + extra_specs,
        out_specs=[
            pl.BlockSpec((seq, dq), lambda b: (rb + b, 0)),
            pl.BlockSpec((seq, dkv), lambda b: (b, 0)),
            pl.BlockSpec((seq, dkv), lambda b: (b, 0)),
        ],
        out_shape=[jax.ShapeDtypeStruct(o_buf.shape, o_buf.dtype), kv_shape, kv_shape],
        input_output_aliases=aliases,
        compiler_params=_cparams(("arbitrary",), 32),
        name="dense_attention",
    )(*args, *extra)


def _na_block_start(qb, rows):
    return min(max(qb * NA_QROWS - NA_ROWS // 2, 0), rows - NA_KROWS)


def _na_build_bias(rb_ref, bias_ref, rows):
    n_qb = rows // NA_QROWS
    qc = lax.broadcasted_iota(jnp.int32, (GRID_W, LANES), 0)
    lane = lax.broadcasted_iota(jnp.int32, (GRID_W, LANES), 1)
    kc = lane & (GRID_W - 1)
    ws = jnp.clip(qc - NA_COLS // 2, 0, GRID_W - NA_COLS)
    col_valid = (kc >= ws) & (kc < ws + NA_COLS)
    first_half = lane < GRID_W
    for kind, qb in enumerate((0, 1, n_qb - 1)):
        k_start = _na_block_start(qb, rows)
        for a in range(NA_QROWS):
            r = qb * NA_QROWS + a
            rs = min(max(r - NA_ROWS // 2, 0), rows - NA_ROWS)
            for pair in range(NA_KROWS // 2):
                vec = None
                valid = None
                for half in range(2):
                    kr = k_start + 2 * pair + half
                    if not rs <= kr < rs + NA_ROWS:
                        continue
                    drow = kr - r + NA_ROWS - 1
                    piece = pltpu.roll(rb_ref[0, drow:drow + 1, :], (half * GRID_W - (NA_COLS - 1)) % LANES, 1)
                    vec = piece if vec is None else vec + piece
                    hv = first_half if half == 0 else jnp.logical_not(first_half)
                    valid = hv if valid is None else jnp.logical_or(valid, hv)
                dst = (kind, slice(a * GRID_W, (a + 1) * GRID_W), slice(pair * LANES, (pair + 1) * LANES))
                if vec is None:
                    bias_ref[dst] = jnp.full((GRID_W, LANES), NEG, F32)
                else:
                    toeplitz = pltpu.roll(jnp.broadcast_to(vec, (GRID_W, LANES)), 0, 1, stride=1, stride_axis=0)
                    bias_ref[dst] = jnp.where(valid & col_valid, toeplitz, NEG)


def _na_kernel(q_ref, k_ref, v_ref, ck_ref, cv_ref, rb_ref, o_ref, kb_ref, vb_ref, bias_ref):
    rows = q_ref.shape[0] // GRID_W
    n_qb = rows // NA_QROWS
    qn = NA_QROWS * GRID_W
    kn = NA_KROWS * GRID_W
    scale = HEAD_DIM ** -0.5

    @pl.when(pl.program_id(1) == 0)
    def _():
        _na_build_bias(rb_ref, bias_ref, rows)

    kb_ref[...] = k_ref[...].astype(BF16)
    vb_ref[...] = v_ref[...].astype(BF16)
    ck = ck_ref[0].astype(BF16)
    cv = cv_ref[0].astype(BF16)
    for qb in range(n_qb):
        k0 = _na_block_start(qb, rows) * GRID_W
        kind = 0 if qb == 0 else (2 if qb == n_qb - 1 else 1)
        q = q_ref[qb * qn:(qb + 1) * qn, :].astype(BF16)
        s_loc = _qkt(q, kb_ref[k0:k0 + kn, :]) * scale + bias_ref[kind]
        s_ctx = _qkt(q, ck) * scale
        o = _softmax_pv([s_loc, s_ctx], [vb_ref[k0:k0 + kn, :], cv])
        o_ref[qb * qn:(qb + 1) * qn, :] = o.astype(o_ref.dtype)


def _na_attention(proj, ck, cv, rel_bias, *, n_batch, seq, n_heads, q_col, k_col, v_col):
    qn, kn = NA_QROWS * GRID_W, NA_KROWS * GRID_W
    qo, ko, vo = q_col // HEAD_DIM, k_col // HEAD_DIM, v_col // HEAD_DIM
    rb = jnp.zeros((n_heads, 2 * NA_ROWS, LANES), F32).at[:, :2 * NA_ROWS - 1, :2 * NA_COLS - 1].set(rel_bias)
    return pl.pallas_call(
        _na_kernel,
        grid=(n_heads, n_batch),
        in_specs=[
            pl.BlockSpec((seq, HEAD_DIM), lambda h, b: (b, qo + h)),
            pl.BlockSpec((seq, HEAD_DIM), lambda h, b: (b, ko + h)),
            pl.BlockSpec((seq, HEAD_DIM), lambda h, b: (b, vo + h)),
            pl.BlockSpec((1, ck.shape[1], HEAD_DIM), lambda h, b: (b, 0, h)),
            pl.BlockSpec((1, cv.shape[1], HEAD_DIM), lambda h, b: (b, 0, h)),
            pl.BlockSpec((1, 2 * NA_ROWS, LANES), lambda h, b: (h, 0, 0)),
        ],
        out_specs=pl.BlockSpec((seq, HEAD_DIM), lambda h, b: (b, h)),
        out_shape=jax.ShapeDtypeStruct((proj.shape[0], n_heads * HEAD_DIM), BF16),
        scratch_shapes=[pltpu.VMEM((seq, HEAD_DIM), BF16), pltpu.VMEM((seq, HEAD_DIM), BF16),
                        pltpu.VMEM((3, qn, kn), F32)],
        compiler_params=_cparams(("arbitrary", "arbitrary"), 32),
        name="na_attention",
    )(proj, proj, proj, ck, cv, rb)


def _rope_tables(seq):
    half = HEAD_DIM // 2
    quarter = half // 2
    t = jnp.arange(seq)
    freqs = ROPE_BASE ** (-jnp.arange(quarter, dtype=F32) * 2.0 / half)
    ang_r = (t // GRID_W).astype(F32)[:, None] * freqs[None]
    ang_c = (t % GRID_W).astype(F32)[:, None] * freqs[None]
    cos = jnp.concatenate([jnp.cos(ang_r)] * 2 + [jnp.cos(ang_c)] * 2, axis=-1)
    sin = jnp.concatenate([-jnp.sin(ang_r), jnp.sin(ang_r), -jnp.sin(ang_c), jnp.sin(ang_c)], axis=-1)
    return cos, sin


def _rope(x, cos, sin):
    quarter = HEAD_DIM // 4
    lane = lax.broadcasted_iota(jnp.int32, x.shape, 1)
    first = (lane & (2 * quarter - 1)) < quarter
    partner = jnp.where(first, pltpu.roll(x, HEAD_DIM - quarter, 1), pltpu.roll(x, quarter, 1))
    return x * cos + partner * sin


def _win_kernel(sink_ref, q_ref, k_ref, v_ref, ck_ref, cv_ref, cos_ref, sin_ref, o_ref, kb_ref, vb_ref,
                *, groups):
    hkv = pl.program_id(1)
    seq = q_ref.shape[0]
    span = C_BLOCK + 2 * C_WINDOW
    scale = HEAD_DIM ** -0.5
    kb_ref[...] = _rope(k_ref[...], cos_ref[...], sin_ref[...]).astype(BF16)
    vb_ref[...] = v_ref[...].astype(BF16)
    ck = ck_ref[0].astype(BF16)
    cv = cv_ref[0].astype(BF16)
    sink = jnp.concatenate(
        [jnp.full((C_BLOCK, 1), sink_ref[hkv * groups + g], F32) for g in range(groups)], axis=0)
    qrow = lax.broadcasted_iota(jnp.int32, (groups * C_BLOCK, span), 0) & (C_BLOCK - 1)
    kcol = lax.broadcasted_iota(jnp.int32, (groups * C_BLOCK, span), 1)
    for i in range(seq // C_BLOCK):
        q0 = i * C_BLOCK
        k0 = min(max(q0 - C_WINDOW, 0), seq - span)
        cos = cos_ref[q0:q0 + C_BLOCK, :]
        sin = sin_ref[q0:q0 + C_BLOCK, :]
        q = jnp.concatenate(
            [_rope(q_ref[q0:q0 + C_BLOCK, g * HEAD_DIM:(g + 1) * HEAD_DIM], cos, sin) for g in range(groups)],
            axis=0).astype(BF16)
        valid = jnp.abs(kcol - qrow + (k0 - q0)) <= C_WINDOW
        s_loc = jnp.where(valid, _qkt(q, kb_ref[k0:k0 + span, :]) * scale, NEG)
        s_ctx = _qkt(q, ck) * scale
        o = _softmax_pv([s_loc, s_ctx], [vb_ref[k0:k0 + span, :], cv], sink)
        for g in range(groups):
            o_ref[q0:q0 + C_BLOCK, g * HEAD_DIM:(g + 1) * HEAD_DIM] = (
                o[g * C_BLOCK:(g + 1) * C_BLOCK].astype(o_ref.dtype))


def _win_attention(proj, ck, cv, sink, *, n_batch, seq, n_kv, groups, k_col, v_col):
    cos, sin = _rope_tables(seq)
    ko, vo = k_col // HEAD_DIM, v_col // HEAD_DIM
    gw = groups * HEAD_DIM
    return pl.pallas_call(
        functools.partial(_win_kernel, groups=groups),
        grid=(n_batch, n_kv),
        in_specs=[
            pl.BlockSpec(memory_space=pltpu.SMEM),
            pl.BlockSpec((seq, gw), lambda b, h: (b, h)),
            pl.BlockSpec((seq, HEAD_DIM), lambda b, h: (b, ko + h)),
            pl.BlockSpec((seq, HEAD_DIM), lambda b, h: (b, vo + h)),
            pl.BlockSpec((1, ck.shape[1], HEAD_DIM), lambda b, h: (b, 0, h)),
            pl.BlockSpec((1, cv.shape[1], HEAD_DIM), lambda b, h: (b, 0, h)),
            pl.BlockSpec((seq, HEAD_DIM), lambda b, h: (0, 0)),
            pl.BlockSpec((seq, HEAD_DIM), lambda b, h: (0, 0)),
        ],
        out_specs=pl.BlockSpec((seq, gw), lambda b, h: (b, h)),
        out_shape=jax.ShapeDtypeStruct((proj.shape[0], n_kv * gw), BF16),
        scratch_shapes=[pltpu.VMEM((seq, HEAD_DIM), BF16), pltpu.VMEM((seq, HEAD_DIM), BF16)],
        compiler_params=_cparams(("arbitrary", "arbitrary"), 32),
        name="win_attention",
    )(sink, proj, proj, proj, ck, cv, cos, sin)


def kernel(x_prompt, x_sample, cache_na_k, cache_na_v, cache_win_k, cache_win_v, c, c_ctx,
           norm_g, mod_w, mod_b, ffn_w_in, ffn_w_out, ab_w_in, pool_w, pool_scale,
           na_q_g, na_k_g, na_rel_bias, ab_w_out, win_w_in, win_q_g, win_k_g, win_sink, win_w_out):
    n_prompt, seq, d = x_prompt.shape
    n_dec, dec_seq, _ = x_sample.shape
    depth = norm_g.shape[0]
    assert dec_seq == SEQ_BLOCK and SEQ_BLOCK % seq == 0 and (n_prompt * seq) % SEQ_BLOCK == 0
    assert n_dec + 1 <= MOD_ROWS
    ts, tp = n_dec * dec_seq, n_prompt * seq
    t = ts + tp
    n_prompt_blocks = tp // SEQ_BLOCK
    s_tiles, p_tiles = ts // ROW_TILE, tp // ROW_TILE

    c_all = jnp.concatenate([c, c_ctx[None], jnp.zeros((MOD_ROWS - n_dec - 1, d), F32)], axis=0)
    m_all = _modulation(c_all, mod_w, mod_b).reshape(depth, MOD_ROWS, N_MOD, d)
    mod = jnp.concatenate(
        [m_all[:, :n_dec], jnp.broadcast_to(m_all[:, n_dec:n_dec + 1], (depth, n_prompt_blocks, N_MOD, d))], axis=1)

    def ada(li, sub):
        m = mod[li]
        rows = jnp.stack([norm_g[li, sub][None] * (1.0 + m[:, 3 * sub + 1]), m[:, 3 * sub]], axis=1)
        return jnp.broadcast_to(rows[:, :, None, :], rows.shape[:2] + (8, d))

    n_ffn = ffn_w_in.shape[1]
    ffn_in_w = ffn_w_in.reshape(depth * n_ffn * d, -1)
    ffn_out_w = ffn_w_out.astype(BF16).reshape(-1, d)
    xs2, xp2 = x_sample.reshape(ts, d), x_prompt.reshape(tp, d)

    def ffn_in(x_src, li, which, sub, **kw):
        return _ffn_in(x_src, ada(li, sub), ffn_in_w, li * n_ffn + which, **kw)

    def ffn_out(hidden, x_src, li, which, sub, **kw):
        return _matmul_resid([hidden], ffn_out_w, x_src, mod[li], gate_row=3 * sub + 2, mult=0.5,
                             tn=256, single_buffer_a=True, w_sel=li * n_ffn + which, **kw)

    def ffn(x, li, which, sub):
        hidden = ffn_in(x, li, which, sub, n_tiles=s_tiles + p_tiles, out_tile0=0, out_rows=t)
        return ffn_out(hidden, x, li, which, sub, n_tiles=s_tiles + p_tiles, a_tile0=0, x_tile0=0,
                       out_tile0=0, out_rows=t)

    na_k, na_v, win_k, win_v = [], [], [], []
    x = None
    for li in range(depth):
        if li == 0:
            hidden = ffn_in(xs2, 0, 0, 0, n_tiles=s_tiles, out_tile0=0, out_rows=t)
            hidden = ffn_in(xp2, 0, 0, 0, n_tiles=p_tiles, out_tile0=s_tiles, out_rows=t, alias=hidden)
            x = ffn_out(hidden, xs2, 0, 0, 0, n_tiles=s_tiles, a_tile0=0, x_tile0=0, out_tile0=0, out_rows=t)
            x = ffn_out(hidden, xp2, 0, 0, 0, n_tiles=p_tiles, a_tile0=s_tiles, x_tile0=0, out_tile0=s_tiles,
                        out_rows=t, alias=x)
        else:
            x = ffn(x, li, 0, 0)
        j = li // 2
        if li % 2 == 0:
            n_heads = cache_na_k.shape[3]
            d_na = n_heads * HEAD_DIM
            d_pool = ab_w_in.shape[2] - 3 * d_na
            w_in = ab_w_in[j].astype(BF16)
            w_in = jnp.concatenate([w_in[:, d_pool:d_pool + 2 * d_na], w_in[:, :d_pool], w_in[:, d_pool + 2 * d_na:]],
                                   axis=1)
            head_gain = jnp.concatenate(
                [jnp.tile(na_q_g[j], n_heads), jnp.tile(na_k_g[j], n_heads), jnp.ones((d_pool + d_na,), F32)])[None]
            q_col, k_col, u_col, v_col = 0, d_na, 2 * d_na, 2 * d_na + d_pool
            proj = _proj_in(x, ada(li, 1), w_in, head_gain, tn=512, norm_cols=2 * d_na)
            y_pool = _pool_mixer(proj, pool_w[j].astype(BF16), pool_scale[j][None], u_col=u_col,
                                 n_sample_blocks=n_dec, prompt_seq=seq)
            o = _na_attention(proj, cache_na_k[:, j].reshape(n_dec, -1, d_na),
                              cache_na_v[:, j].reshape(n_dec, -1, d_na), na_rel_bias[j], n_batch=n_dec, seq=dec_seq,
                              n_heads=n_heads, q_col=q_col, k_col=k_col, v_col=v_col)
            o, new_k, new_v = _dense_attention(proj, None, o, row0=ts, n_seq=n_prompt, seq=seq, q_col=q_col,
                                               k_col=k_col, v_col=v_col, n_kv=n_heads, groups=1)
            parts, w_out = [y_pool, o], ab_w_out[j].astype(BF16)
            na_k.append(new_k.reshape(n_prompt, seq, n_heads, HEAD_DIM))
            na_v.append(new_v.reshape(n_prompt, seq, n_heads, HEAD_DIM))
        else:
            n_kv = cache_win_k.shape[3]
            dkv = n_kv * HEAD_DIM
            groups = d // dkv
            head_gain = jnp.concatenate(
                [jnp.tile(win_q_g[j], n_kv * groups), jnp.tile(win_k_g[j], n_kv), jnp.ones((dkv,), F32)])[None]
            proj = _proj_in(x, ada(li, 1), win_w_in[j].astype(BF16), head_gain,
                            tn=512 if (d + dkv) % 512 == 0 else 256, norm_cols=d + dkv)
            o = _win_attention(proj, cache_win_k[:, j].reshape(n_dec, -1, dkv),
                               cache_win_v[:, j].reshape(n_dec, -1, dkv), win_sink[j], n_batch=n_dec,
                               seq=dec_seq, n_kv=n_kv, groups=groups, k_col=d, v_col=d + dkv)
            o, new_k, new_v = _dense_attention(proj, win_sink[j], o, row0=ts, n_seq=n_prompt, seq=seq, q_col=0,
                                               k_col=d, v_col=d + dkv, n_kv=n_kv, groups=groups)
            parts, w_out = [o], win_w_out[j].astype(BF16)
            win_k.append(new_k.reshape(n_prompt, seq, n_kv, HEAD_DIM))
            win_v.append(new_v.reshape(n_prompt, seq, n_kv, HEAD_DIM))
        x = _matmul_resid(parts, w_out, x, mod[li], gate_row=5, mult=1.0, tn=512, single_buffer_a=False,
                          n_tiles=s_tiles + p_tiles, a_tile0=0, x_tile0=0, out_tile0=0, out_rows=t)
        if li < depth - 1:
            x = ffn(x, li, 1, 2)

    li = depth - 1
    hidden = ffn_in(x, li, 1, 2, n_tiles=s_tiles + p_tiles, out_tile0=0, out_rows=t)
    ys = ffn_out(hidden, x, li, 1, 2, n_tiles=s_tiles, a_tile0=0, x_tile0=0, out_tile0=0, out_rows=ts)
    yp = ffn_out(hidden, x, li, 1, 2, n_tiles=p_tiles, a_tile0=s_tiles, x_tile0=s_tiles, out_tile0=0, out_rows=tp)
    return (yp.reshape(n_prompt, seq, d), ys.reshape(n_dec, dec_seq, d), jnp.stack(na_k, axis=1),
            jnp.stack(na_v, axis=1), jnp.stack(win_k, axis=1), jnp.stack(win_v, axis=1))
```

```python
import functools

import jax
import jax.numpy as jnp
from jax import lax
from jax.experimental import pallas as pl
from jax.experimental.pallas import tpu as pltpu

EPS = 1e-6
NEG = -1e30
HEAD_DIM = 128
LANES = 128
GRID_W = 64
POOL_WINDOWS = (2, 4, 8, 16)
NA_ROWS = 8
NA_COLS = 16
C_WINDOW = 128
C_BLOCK = 128
ROPE_BASE = 10000.0
N_SUB = 3
N_MOD = 3 * N_SUB

SEQ_BLOCK = 2048
ROW_TILE = 1024
IN_TILE = 2048
PRO_ROWS = 256
STAT_ROWS = 128
NORM_ROWS = 16
NORM_COLS = 1024
NA_QROWS = 4
NA_KROWS = 12
MOD_ROWS = 16
MIB = 1024 * 1024
BF16 = jnp.bfloat16
F32 = jnp.float32


def _cparams(sem, vmem_mib):
    return pltpu.CompilerParams(dimension_semantics=sem, vmem_limit_bytes=vmem_mib * MIB)


def _alias_args(alias, n_inputs):
    if alias is None:
        return [], [], {}
    return [alias], [pl.BlockSpec(memory_space=pl.ANY)], {n_inputs: 0}


def _mod_kernel(c_ref, w_ref, b_ref, o_ref):
    c = c_ref[...]
    a = (c * jax.nn.sigmoid(c)).astype(BF16)
    w = w_ref[...].astype(BF16)
    o_ref[...] = jnp.dot(a, w, preferred_element_type=F32) + b_ref[...]


def _modulation(c_all, mod_w, mod_b, tn=512):
    depth, d, n = mod_w.shape
    return pl.pallas_call(
        _mod_kernel,
        grid=(depth, n // tn),
        in_specs=[
            pl.BlockSpec((MOD_ROWS, d), lambda l, j: (0, 0)),
            pl.BlockSpec((None, d, tn), lambda l, j: (l, 0, j)),
            pl.BlockSpec((None, 1, tn), lambda l, j: (l, 0, j)),
        ],
        out_specs=pl.BlockSpec((None, MOD_ROWS, tn), lambda l, j: (l, 0, j)),
        out_shape=jax.ShapeDtypeStruct((depth, MOD_ROWS, n), F32),
        compiler_params=_cparams(("arbitrary", "arbitrary"), 40),
        name="modulation",
    )(c_all, mod_w, mod_b.reshape(depth, 1, n))


def _adaln_rows(x_ref, ada_ref, h_ref, r_ref, row0):
    rows, d = x_ref.shape
    inv_d = 1.0 / d

    def stats(c, carry):
        r = pl.multiple_of(c * STAT_ROWS, STAT_ROWS)
        x = x_ref[pl.ds(r, STAT_ROWS), :]
        ms = jnp.sum(x * x, axis=-1, keepdims=True) * inv_d
        r_ref[pl.ds(r, STAT_ROWS), :] = jnp.broadcast_to(lax.rsqrt(ms + EPS), (STAT_ROWS, LANES))
        return carry

    def scale(c, carry):
        r = pl.multiple_of(c * NORM_ROWS, NORM_ROWS)
        rinv = jnp.concatenate([r_ref[pl.ds(r, NORM_ROWS), :]] * (NORM_COLS // LANES), axis=1)
        for c0 in range(0, d, NORM_COLS):
            cs = slice(c0, c0 + NORM_COLS)
            gain = jnp.concatenate([ada_ref[0, 0, :, cs]] * (NORM_ROWS // 8), axis=0)
            shift = jnp.concatenate([ada_ref[0, 1, :, cs]] * (NORM_ROWS // 8), axis=0)
            x = x_ref[pl.ds(r, NORM_ROWS), cs]
            h_ref[pl.ds(pl.multiple_of(row0 + r, NORM_ROWS), NORM_ROWS), cs] = (
                (x * rinv) * gain + shift).astype(h_ref.dtype)
        return carry

    lax.fori_loop(0, rows // STAT_ROWS, stats, 0)
    lax.fori_loop(0, rows // NORM_ROWS, scale, 0, unroll=2)


def _as_bf16(w):
    return w if w.dtype == BF16 else w.astype(BF16)


def _adaln_specs(d, tile0):
    n_pro = IN_TILE // PRO_ROWS
    per = SEQ_BLOCK // IN_TILE
    return n_pro, [
        pl.BlockSpec((PRO_ROWS, d), lambda i, j: (i * n_pro + jnp.minimum(j, n_pro - 1), 0)),
        pl.BlockSpec((1, 2, 8, d), lambda i, j: ((i + tile0) // per, 0, 0, 0)),
    ]


def _swiglu_kernel(x_ref, ada_ref, wg_ref, wu_ref, *rest, n_pro):
    o_ref, h_ref, r_ref = rest[-3:]
    j = pl.program_id(1)

    @pl.when(j < n_pro)
    def _():
        _adaln_rows(x_ref, ada_ref, h_ref, r_ref, j * PRO_ROWS)

    @pl.when(j >= n_pro)
    def _():
        h = h_ref[...]
        gate = jnp.dot(h, _as_bf16(wg_ref[...]), preferred_element_type=F32)
        up = jnp.dot(h, _as_bf16(wu_ref[...]), preferred_element_type=F32)
        o_ref[...] = ((gate * jax.nn.sigmoid(gate)) * up).astype(o_ref.dtype)


def _ffn_in(x, ada, w_in, w_sel, *, n_rows, out_row0, out_rows, alias=None, tn=256):
    tm = IN_TILE
    d = x.shape[1]
    dff = w_in.shape[1] // 2
    nj = dff // tn
    tile0 = out_row0 // tm
    n_pro, specs = _adaln_specs(d, tile0)
    extra, extra_specs, aliases = _alias_args(alias, 4)

    def wcol(j):
        return jnp.maximum(j - n_pro, 0)

    return pl.pallas_call(
        functools.partial(_swiglu_kernel, n_pro=n_pro),
        grid=(n_rows // tm, n_pro + nj),
        in_specs=specs + [
            pl.BlockSpec((d, tn), lambda i, j: (w_sel, wcol(j))),
            pl.BlockSpec((d, tn), lambda i, j: (w_sel, wcol(j) + nj)),
        ] + extra_specs,
        out_specs=pl.BlockSpec((tm, tn), lambda i, j: (i + tile0, wcol(j))),
        out_shape=jax.ShapeDtypeStruct((out_rows, dff), BF16),
        scratch_shapes=[pltpu.VMEM((tm, d), BF16), pltpu.VMEM((PRO_ROWS, LANES), F32)],
        input_output_aliases=aliases,
        compiler_params=_cparams(("arbitrary", "arbitrary"), 56),
        name="ffn_in",
    )(x, ada, w_in, w_in, *extra)


def _headnorm_kernel(x_ref, ada_ref, w_ref, hg_ref, o_ref, h_ref, r_ref, *, n_pro, norm_tiles):
    j = pl.program_id(1)

    @pl.when(j < n_pro)
    def _():
        _adaln_rows(x_ref, ada_ref, h_ref, r_ref, j * PRO_ROWS)

    @pl.when((j >= n_pro) & (j < n_pro + norm_tiles))
    def _():
        y = jnp.dot(h_ref[...], _as_bf16(w_ref[...]), preferred_element_type=F32)
        for c in range(y.shape[1] // HEAD_DIM):
            sl = slice(c * HEAD_DIM, (c + 1) * HEAD_DIM)
            yc = y[:, sl]
            ms = jnp.mean(yc * yc, axis=-1, keepdims=True)
            o_ref[:, sl] = (yc * lax.rsqrt(ms + EPS)) * hg_ref[:, sl]

    @pl.when(j >= n_pro + norm_tiles)
    def _():
        o_ref[...] = jnp.dot(h_ref[...], _as_bf16(w_ref[...]), preferred_element_type=F32)


def _proj_in(x, ada, w, head_gain, *, tn, norm_cols, col_perm=None):
    tm = IN_TILE
    t, d = x.shape
    n = w.shape[1]
    n_pro, specs = _adaln_specs(d, 0)
    if col_perm is None:
        col_perm = lambda c: c

    def wcol(j):
        return jnp.maximum(j - n_pro, 0)

    return pl.pallas_call(
        functools.partial(_headnorm_kernel, n_pro=n_pro, norm_tiles=norm_cols // tn),
        grid=(t // tm, n_pro + n // tn),
        in_specs=specs + [
            pl.BlockSpec((d, tn), lambda i, j: (0, col_perm(wcol(j)))),
            pl.BlockSpec((1, tn), lambda i, j: (0, wcol(j))),
        ],
        out_specs=pl.BlockSpec((tm, tn), lambda i, j: (i, wcol(j))),
        out_shape=jax.ShapeDtypeStruct((t, n), F32),
        scratch_shapes=[pltpu.VMEM((tm, d), BF16), pltpu.VMEM((PRO_ROWS, LANES), F32)],
        compiler_params=_cparams(("arbitrary", "arbitrary"), 56),
        name="proj_in",
    )(x, ada, w, head_gain)


def _resid_kernel(*refs, n_parts, gate_row, mult):
    a_refs = refs[:n_parts]
    w_refs = refs[n_parts:2 * n_parts]
    x_ref, mod_ref = refs[2 * n_parts:2 * n_parts + 2]
    o_ref = refs[-1]
    acc = jnp.dot(a_refs[0][...], _as_bf16(w_refs[0][...]), preferred_element_type=F32)
    for a_ref, w_ref in zip(a_refs[1:], w_refs[1:]):
        acc = acc + jnp.dot(a_ref[...], _as_bf16(w_ref[...]), preferred_element_type=F32)
    gate = mod_ref[0, gate_row:gate_row + 1, :]
    if mult != 1.0:
        gate = mult * gate
    o_ref[...] = x_ref[...] + gate * acc


def _matmul_resid(parts, w, x, mod, *, gate_row, mult, tn, single_buffer_a, n_tiles, a_tile0, x_tile0,
                  out_tile0, out_rows, vmem_mib, w_sel=0, alias=None):
    tm = ROW_TILE
    d = w.shape[1]
    per = SEQ_BLOCK // tm
    a_mode = dict(pipeline_mode=pl.Buffered(1)) if single_buffer_a else {}
    k_total = sum(a.shape[1] for a in parts)
    in_specs, w_specs, off = [], [], w_sel * k_total
    for a in parts:
        k = a.shape[1]
        assert off % k == 0
        in_specs.append(pl.BlockSpec((tm, k), lambda i, j: (i + a_tile0, 0), **a_mode))
        w_specs.append(pl.BlockSpec((k, tn), functools.partial(lambda i, j, kb: (kb, j), kb=off // k)))
        off += k
    in_specs += w_specs + [
        pl.BlockSpec((tm, tn), lambda i, j: (i + x_tile0, j)),
        pl.BlockSpec((1, N_MOD, tn), lambda i, j: ((i + a_tile0) // per, 0, j)),
    ]
    extra, extra_specs, aliases = _alias_args(alias, len(in_specs))
    return pl.pallas_call(
        functools.partial(_resid_kernel, n_parts=len(parts), gate_row=gate_row, mult=mult),
        grid=(n_tiles, d // tn),
        in_specs=in_specs + extra_specs,
        out_specs=pl.BlockSpec((tm, tn), lambda i, j: (i + out_tile0, j)),
        out_shape=jax.ShapeDtypeStruct((out_rows, d), F32),
        input_output_aliases=aliases,
        compiler_params=_cparams(("arbitrary", "arbitrary"), vmem_mib),
        name="matmul_resid",
    )(*parts, *([w] * len(parts)), x, mod, *extra)


def _pool_kernel(u_ref, w_ref, s_ref, o_ref, *, n_sample_blocks, prompt_seq):
    i = pl.program_id(0)
    g = pl.program_id(1)
    rows = u_ref.shape[0]
    seq_len = jnp.where(i < n_sample_blocks, rows, prompt_seq)
    pos = lax.broadcasted_iota(jnp.int32, (rows, 1), 0) & (seq_len - 1)

    for gi, win in enumerate(POOL_WINDOWS):
        @pl.when(g == gi)
        def _(win=win):
            u = u_ref[...]
            acc = jnp.zeros_like(u)
            cnt = jnp.zeros((rows, 1), F32)
            for k in range(-(win // 2), win - win // 2):
                nb = pos + k
                valid = (nb >= 0) & (nb < seq_len)
                shifted = u if k == 0 else pltpu.roll(u, (-k) % rows, 0)
                acc = acc + jnp.where(valid, shifted, 0.0)
                cnt = cnt + valid.astype(F32)
            diff = (acc / cnt - u).astype(BF16)
            y = jnp.dot(diff, w_ref[...], preferred_element_type=F32)
            o_ref[...] = (y * s_ref[...]).astype(o_ref.dtype)


def _pool_mixer(proj, pool_w, pool_scale, *, u_col, n_sample_blocks, prompt_seq):
    t = proj.shape[0]
    n_groups, cg, _ = pool_w.shape
    g0 = u_col // cg
    return pl.pallas_call(
        functools.partial(_pool_kernel, n_sample_blocks=n_sample_blocks, prompt_seq=prompt_seq),
        grid=(t // SEQ_BLOCK, n_groups),
        in_specs=[
            pl.BlockSpec((SEQ_BLOCK, cg), lambda i, g: (i, g0 + g)),
            pl.BlockSpec((None, cg, cg), lambda i, g: (g, 0, 0)),
            pl.BlockSpec((1, cg), lambda i, g: (0, g)),
        ],
        out_specs=pl.BlockSpec((SEQ_BLOCK, cg), lambda i, g: (i, g)),
        out_shape=jax.ShapeDtypeStruct((t, n_groups * cg), BF16),
        compiler_params=_cparams(("arbitrary", "arbitrary"), 48),
        name="pool_mixer",
    )(proj, pool_w, pool_scale)


def _qkt(q, k):
    return lax.dot_general(q, k, (((1,), (1,)), ((), ())), preferred_element_type=F32)


def _softmax_pv(s_list, v_list, sink=None):
    m = s_list[0].max(axis=-1, keepdims=True)
    for s in s_list[1:]:
        m = jnp.maximum(m, s.max(axis=-1, keepdims=True))
    if sink is not None:
        m = jnp.maximum(m, sink)
    l = None
    o = None
    for s, v in zip(s_list, v_list):
        p = jnp.exp(s - m)
        ls = p.sum(axis=-1, keepdims=True)
        os_ = jnp.dot(p.astype(BF16), v, preferred_element_type=F32)
        l = ls if l is None else l + ls
        o = os_ if o is None else o + os_
    if sink is not None:
        l = l + jnp.exp(sink - m)
    return o / l


def _dense_attn_kernel(*refs, n_kv, groups, has_sink):
    if has_sink:
        sink_ref, q_ref, k_ref, v_ref = refs[:4]
    else:
        q_ref, k_ref, v_ref = refs[:3]
    o_ref, nk_ref, nv_ref = refs[-3:]
    seq = q_ref.shape[0]
    scale = HEAD_DIM ** -0.5
    nk_ref[...] = k_ref[...]
    nv_ref[...] = v_ref[...]
    for h in range(n_kv):
        ks = slice(h * HEAD_DIM, (h + 1) * HEAD_DIM)
        k = k_ref[:, ks].astype(BF16)
        v = v_ref[:, ks].astype(BF16)
        qs = [q_ref[:, (h * groups + g) * HEAD_DIM:(h * groups + g + 1) * HEAD_DIM] for g in range(groups)]
        q = (qs[0] if groups == 1 else jnp.concatenate(qs, axis=0)).astype(BF16)
        s = _qkt(q, k) * scale
        sink = None
        if has_sink:
            sink = jnp.concatenate(
                [jnp.full((seq, 1), sink_ref[h * groups + g], F32) for g in range(groups)], axis=0)
        o = _softmax_pv([s], [v], sink)
        for g in range(groups):
            c0 = (h * groups + g) * HEAD_DIM
            o_ref[:, c0:c0 + HEAD_DIM] = o[g * seq:(g + 1) * seq].astype(o_ref.dtype)


def _dense_attention(proj, sink, o_buf, *, row0, n_seq, seq, q_col, k_col, v_col, n_kv, groups):
    dq = n_kv * groups * HEAD_DIM
    dkv = n_kv * HEAD_DIM
    rb = row0 // seq
    in_specs = [
        pl.BlockSpec((seq, dq), lambda b: (rb + b, q_col // dq)),
        pl.BlockSpec((seq, dkv), lambda b: (rb + b, k_col // dkv)),
        pl.BlockSpec((seq, dkv), lambda b: (rb + b, v_col // dkv)),
    ]
    args = [proj, proj, proj]
    if sink is not None:
        in_specs = [pl.BlockSpec(memory_space=pltpu.SMEM)] + in_specs
        args = [sink] + args
    extra, extra_specs, aliases = _alias_args(o_buf, len(args))
    kv_shape = jax.ShapeDtypeStruct((n_seq * seq, dkv), F32)
    return pl.pallas_call(
        functools.partial(_dense_attn_kernel, n_kv=n_kv, groups=groups, has_sink=sink is not None),
        grid=(n_seq,),
        in_specs=in_specs + extra_specs,
        out_specs=[
            pl.BlockSpec((seq, dq), lambda b: (rb + b, 0)),
            pl.BlockSpec((seq, dkv), lambda b: (b, 0)),
            pl.BlockSpec((seq, dkv), lambda b: (b, 0)),
        ],
        out_shape=[jax.ShapeDtypeStruct(o_buf.shape, o_buf.dtype), kv_shape, kv_shape],
        input_output_aliases=aliases,
        compiler_params=_cparams(("arbitrary",), 32),
        name="dense_attention",
    )(*args, *extra)


def _na_block_start(qb, rows):
    return min(max(qb * NA_QROWS - NA_ROWS // 2, 0), rows - NA_KROWS)


def _na_build_bias(rb_ref, bias_ref, rows):
    n_qb = rows // NA_QROWS
    qc = lax.broadcasted_iota(jnp.int32, (GRID_W, LANES), 0)
    lane = lax.broadcasted_iota(jnp.int32, (GRID_W, LANES), 1)
    kc = lane & (GRID_W - 1)
    ws = jnp.clip(qc - NA_COLS // 2, 0, GRID_W - NA_COLS)
    col_valid = (kc >= ws) & (kc < ws + NA_COLS)
    first_half = lane < GRID_W
    for kind, qb in enumerate((0, 1, n_qb - 1)):
        k_start = _na_block_start(qb, rows)
        for a in range(NA_QROWS):
            r = qb * NA_QROWS + a
            rs = min(max(r - NA_ROWS // 2, 0), rows - NA_ROWS)
            for pair in range(NA_KROWS // 2):
                vec = None
                valid = None
                for half in range(2):
                    kr = k_start + 2 * pair + half
                    if not rs <= kr < rs + NA_ROWS:
                        continue
                    drow = kr - r + NA_ROWS - 1
                    piece = pltpu.roll(rb_ref[0, drow:drow + 1, :], (half * GRID_W - (NA_COLS - 1)) % LANES, 1)
                    vec = piece if vec is None else vec + piece
                    hv = first_half if half == 0 else jnp.logical_not(first_half)
                    valid = hv if valid is None else jnp.logical_or(valid, hv)
                dst = (kind, slice(a * GRID_W, (a + 1) * GRID_W), slice(pair * LANES, (pair + 1) * LANES))
                if vec is None:
                    bias_ref[dst] = jnp.full((GRID_W, LANES), NEG, F32)
                else:
                    toeplitz = pltpu.roll(jnp.broadcast_to(vec, (GRID_W, LANES)), 0, 1, stride=1, stride_axis=0)
                    bias_ref[dst] = jnp.where(valid & col_valid, toeplitz, NEG)


def _na_kernel(q_ref, k_ref, v_ref, ck_ref, cv_ref, rb_ref, o_ref, kb_ref, vb_ref, bias_ref):
    rows = q_ref.shape[0] // GRID_W
    n_qb = rows // NA_QROWS
    qn = NA_QROWS * GRID_W
    kn = NA_KROWS * GRID_W
    scale = HEAD_DIM ** -0.5

    @pl.when(pl.program_id(1) == 0)
    def _():
        _na_build_bias(rb_ref, bias_ref, rows)

    kb_ref[...] = k_ref[...].astype(BF16)
    vb_ref[...] = v_ref[...].astype(BF16)
    ck = ck_ref[0].astype(BF16)
    cv = cv_ref[0].astype(BF16)
    for qb in range(n_qb):
        k0 = _na_block_start(qb, rows) * GRID_W
        kind = 0 if qb == 0 else (2 if qb == n_qb - 1 else 1)
        q = q_ref[qb * qn:(qb + 1) * qn, :].astype(BF16)
        s_loc = _qkt(q, kb_ref[k0:k0 + kn, :]) * scale + bias_ref[kind]
        s_ctx = _qkt(q, ck) * scale
        o = _softmax_pv([s_loc, s_ctx], [vb_ref[k0:k0 + kn, :], cv])
        o_ref[qb * qn:(qb + 1) * qn, :] = o.astype(o_ref.dtype)


def _na_attention(proj, ck, cv, rel_bias, *, n_batch, seq, n_heads, q_col, k_col, v_col):
    qn, kn = NA_QROWS * GRID_W, NA_KROWS * GRID_W
    qo, ko, vo = q_col // HEAD_DIM, k_col // HEAD_DIM, v_col // HEAD_DIM
    rb = jnp.zeros((n_heads, 2 * NA_ROWS, LANES), F32).at[:, :2 * NA_ROWS - 1, :2 * NA_COLS - 1].set(rel_bias)
    return pl.pallas_call(
        _na_kernel,
        grid=(n_heads, n_batch),
        in_specs=[
            pl.BlockSpec((seq, HEAD_DIM), lambda h, b: (b, qo + h)),
            pl.BlockSpec((seq, HEAD_DIM), lambda h, b: (b, ko + h)),
            pl.BlockSpec((seq, HEAD_DIM), lambda h, b: (b, vo + h)),
            pl.BlockSpec((1, ck.shape[1], HEAD_DIM), lambda h, b: (b, 0, h)),
            pl.BlockSpec((1, cv.shape[1], HEAD_DIM), lambda h, b: (b, 0, h)),
            pl.BlockSpec((1, 2 * NA_ROWS, LANES), lambda h, b: (h, 0, 0)),
        ],
        out_specs=pl.BlockSpec((seq, HEAD_DIM), lambda h, b: (b, h)),
        out_shape=jax.ShapeDtypeStruct((proj.shape[0], n_heads * HEAD_DIM), BF16),
        scratch_shapes=[pltpu.VMEM((seq, HEAD_DIM), BF16), pltpu.VMEM((seq, HEAD_DIM), BF16),
                        pltpu.VMEM((3, qn, kn), F32)],
        compiler_params=_cparams(("arbitrary", "arbitrary"), 32),
        name="na_attention",
    )(proj, proj, proj, ck, cv, rb)


def _rope_tables(seq):
    half = HEAD_DIM // 2
    quarter = half // 2
    t = jnp.arange(seq)
    freqs = ROPE_BASE ** (-jnp.arange(quarter, dtype=F32) * 2.0 / half)
    ang_r = (t // GRID_W).astype(F32)[:, None] * freqs[None]
    ang_c = (t % GRID_W).astype(F32)[:, None] * freqs[None]
    cos = jnp.concatenate([jnp.cos(ang_r)] * 2 + [jnp.cos(ang_c)] * 2, axis=-1)
    sin = jnp.concatenate([-jnp.sin(ang_r), jnp.sin(ang_r), -jnp.sin(ang_c), jnp.sin(ang_c)], axis=-1)
    return cos, sin


def _rope(x, cos, sin):
    quarter = HEAD_DIM // 4
    lane = lax.broadcasted_iota(jnp.int32, x.shape, 1)
    first = (lane & (2 * quarter - 1)) < quarter
    partner = jnp.where(first, pltpu.roll(x, HEAD_DIM - quarter, 1), pltpu.roll(x, quarter, 1))
    return x * cos + partner * sin


def _win_kernel(sink_ref, q_ref, k_ref, v_ref, ck_ref, cv_ref, cos_ref, sin_ref, o_ref, kb_ref, vb_ref,
                *, groups):
    hkv = pl.program_id(1)
    seq = q_ref.shape[0]
    span = C_BLOCK + 2 * C_WINDOW
    scale = HEAD_DIM ** -0.5
    kb_ref[...] = _rope(k_ref[...], cos_ref[...], sin_ref[...]).astype(BF16)
    vb_ref[...] = v_ref[...].astype(BF16)
    ck = ck_ref[0].astype(BF16)
    cv = cv_ref[0].astype(BF16)
    sink = jnp.concatenate(
        [jnp.full((C_BLOCK, 1), sink_ref[hkv * groups + g], F32) for g in range(groups)], axis=0)
    qrow = lax.broadcasted_iota(jnp.int32, (groups * C_BLOCK, span), 0) & (C_BLOCK - 1)
    kcol = lax.broadcasted_iota(jnp.int32, (groups * C_BLOCK, span), 1)
    for i in range(seq // C_BLOCK):
        q0 = i * C_BLOCK
        k0 = min(max(q0 - C_WINDOW, 0), seq - span)
        cos = cos_ref[q0:q0 + C_BLOCK, :]
        sin = sin_ref[q0:q0 + C_BLOCK, :]
        q = jnp.concatenate(
            [_rope(q_ref[q0:q0 + C_BLOCK, g * HEAD_DIM:(g + 1) * HEAD_DIM], cos, sin) for g in range(groups)],
            axis=0).astype(BF16)
        valid = jnp.abs(kcol - qrow + (k0 - q0)) <= C_WINDOW
        s_loc = jnp.where(valid, _qkt(q, kb_ref[k0:k0 + span, :]) * scale, NEG)
        s_ctx = _qkt(q, ck) * scale
        o = _softmax_pv([s_loc, s_ctx], [vb_ref[k0:k0 + span, :], cv], sink)
        for g in range(groups):
            o_ref[q0:q0 + C_BLOCK, g * HEAD_DIM:(g + 1) * HEAD_DIM] = (
                o[g * C_BLOCK:(g + 1) * C_BLOCK].astype(o_ref.dtype))


def _win_attention(proj, ck, cv, sink, *, n_batch, seq, n_kv, groups, k_col, v_col):
    cos, sin = _rope_tables(seq)
    ko, vo = k_col // HEAD_DIM, v_col // HEAD_DIM
    gw = groups * HEAD_DIM
    return pl.pallas_call(
        functools.partial(_win_kernel, groups=groups),
        grid=(n_batch, n_kv),
        in_specs=[
            pl.BlockSpec(memory_space=pltpu.SMEM),
            pl.BlockSpec((seq, gw), lambda b, h: (b, h)),
            pl.BlockSpec((seq, HEAD_DIM), lambda b, h: (b, ko + h)),
            pl.BlockSpec((seq, HEAD_DIM), lambda b, h: (b, vo + h)),
            pl.BlockSpec((1, ck.shape[1], HEAD_DIM), lambda b, h: (b, 0, h)),
            pl.BlockSpec((1, cv.shape[1], HEAD_DIM), lambda b, h: (b, 0, h)),
            pl.BlockSpec((seq, HEAD_DIM), lambda b, h: (0, 0)),
            pl.BlockSpec((seq, HEAD_DIM), lambda b, h: (0, 0)),
        ],
        out_specs=pl.BlockSpec((seq, gw), lambda b, h: (b, h)),
        out_shape=jax.ShapeDtypeStruct((proj.shape[0], n_kv * gw), BF16),
        scratch_shapes=[pltpu.VMEM((seq, HEAD_DIM), BF16), pltpu.VMEM((seq, HEAD_DIM), BF16)],
        compiler_params=_cparams(("arbitrary", "arbitrary"), 32),
        name="win_attention",
    )(sink, proj, proj, proj, ck, cv, cos, sin)


def kernel(x_prompt, x_sample, cache_na_k, cache_na_v, cache_win_k, cache_win_v, c, c_ctx,
           norm_g, mod_w, mod_b, ffn_w_in, ffn_w_out, ab_w_in, pool_w, pool_scale,
           na_q_g, na_k_g, na_rel_bias, ab_w_out, win_w_in, win_q_g, win_k_g, win_sink, win_w_out):
    n_prompt, seq, d = x_prompt.shape
    n_dec, dec_seq, _ = x_sample.shape
    depth = norm_g.shape[0]
    assert dec_seq == SEQ_BLOCK and SEQ_BLOCK % seq == 0 and (n_prompt * seq) % SEQ_BLOCK == 0
    assert n_dec + 1 <= MOD_ROWS
    ts, tp = n_dec * dec_seq, n_prompt * seq
    t = ts + tp
    n_prompt_blocks = tp // SEQ_BLOCK
    s_tiles, p_tiles = ts // ROW_TILE, tp // ROW_TILE

    c_all = jnp.concatenate([c, c_ctx[None], jnp.zeros((MOD_ROWS - n_dec - 1, d), F32)], axis=0)
    m_all = _modulation(c_all, mod_w, mod_b).reshape(depth, MOD_ROWS, N_MOD, d)
    mod = jnp.concatenate(
        [m_all[:, :n_dec], jnp.broadcast_to(m_all[:, n_dec:n_dec + 1], (depth, n_prompt_blocks, N_MOD, d))], axis=1)

    def ada(li, sub):
        m = mod[li]
        rows = jnp.stack([norm_g[li, sub][None] * (1.0 + m[:, 3 * sub + 1]), m[:, 3 * sub]], axis=1)
        return jnp.broadcast_to(rows[:, :, None, :], rows.shape[:2] + (8, d))

    n_ffn = ffn_w_in.shape[1]
    ffn_in_w = ffn_w_in.reshape(depth * n_ffn * d, -1)
    ffn_out_w = ffn_w_out.astype(BF16).reshape(-1, d)
    xs2, xp2 = x_sample.reshape(ts, d), x_prompt.reshape(tp, d)

    def ffn_in(x_src, li, which, sub, **kw):
        return _ffn_in(x_src, ada(li, sub), ffn_in_w, li * n_ffn + which, **kw)

    def ffn_out(hidden, x_src, li, which, sub, **kw):
        return _matmul_resid([hidden], ffn_out_w, x_src, mod[li], gate_row=3 * sub + 2, mult=0.5,
                             tn=512, single_buffer_a=True, vmem_mib=60, w_sel=li * n_ffn + which, **kw)

    def ffn(x, li, which, sub):
        hidden = ffn_in(x, li, which, sub, n_rows=t, out_row0=0, out_rows=t)
        return ffn_out(hidden, x, li, which, sub, n_tiles=s_tiles + p_tiles, a_tile0=0, x_tile0=0,
                       out_tile0=0, out_rows=t)

    na_k, na_v, win_k, win_v = [], [], [], []
    x = None
    for li in range(depth):
        if li == 0:
            hidden = ffn_in(xs2, 0, 0, 0, n_rows=ts, out_row0=0, out_rows=t)
            hidden = ffn_in(xp2, 0, 0, 0, n_rows=tp, out_row0=ts, out_rows=t, alias=hidden)
            x = ffn_out(hidden, xs2, 0, 0, 0, n_tiles=s_tiles, a_tile0=0, x_tile0=0, out_tile0=0, out_rows=t)
            x = ffn_out(hidden, xp2, 0, 0, 0, n_tiles=p_tiles, a_tile0=s_tiles, x_tile0=0, out_tile0=s_tiles,
                        out_rows=t, alias=x)
        else:
            x = ffn(x, li, 0, 0)
        j = li // 2
        if li % 2 == 0:
            n_heads = cache_na_k.shape[3]
            d_na = n_heads * HEAD_DIM
            d_pool = ab_w_in.shape[2] - 3 * d_na
            tn = 512
            qk_t, pool_t = 2 * d_na // tn, d_pool // tn

            def col_perm(c):
                return jnp.where(c < qk_t, c + pool_t, jnp.where(c < qk_t + pool_t, c - qk_t, c))

            head_gain = jnp.concatenate(
                [jnp.tile(na_q_g[j], n_heads), jnp.tile(na_k_g[j], n_heads), jnp.ones((d_pool + d_na,), F32)])[None]
            q_col, k_col, u_col, v_col = 0, d_na, 2 * d_na, 2 * d_na + d_pool
            proj = _proj_in(x, ada(li, 1), ab_w_in[j], head_gain, tn=tn, norm_cols=2 * d_na, col_perm=col_perm)
            y_pool = _pool_mixer(proj, pool_w[j].astype(BF16), pool_scale[j][None], u_col=u_col,
                                 n_sample_blocks=n_dec, prompt_seq=seq)
            o = _na_attention(proj, cache_na_k[:, j].reshape(n_dec, -1, d_na),
                              cache_na_v[:, j].reshape(n_dec, -1, d_na), na_rel_bias[j], n_batch=n_dec, seq=dec_seq,
                              n_heads=n_heads, q_col=q_col, k_col=k_col, v_col=v_col)
            o, new_k, new_v = _dense_attention(proj, None, o, row0=ts, n_seq=n_prompt, seq=seq, q_col=q_col,
                                               k_col=k_col, v_col=v_col, n_kv=n_heads, groups=1)
            parts, w_out = [y_pool, o], ab_w_out[j]
            na_k.append(new_k.reshape(n_prompt, seq, n_heads, HEAD_DIM))
            na_v.append(new_v.reshape(n_prompt, seq, n_heads, HEAD_DIM))
        else:
            n_kv = cache_win_k.shape[3]
            dkv = n_kv * HEAD_DIM
            groups = d // dkv
            head_gain = jnp.concatenate(
                [jnp.tile(win_q_g[j], n_kv * groups), jnp.tile(win_k_g[j], n_kv), jnp.ones((dkv,), F32)])[None]
            proj = _proj_in(x, ada(li, 1), win_w_in[j], head_gain,
                            tn=512 if (d + dkv) % 512 == 0 else 256, norm_cols=d + dkv)
            o = _win_attention(proj, cache_win_k[:, j].reshape(n_dec, -1, dkv),
                               cache_win_v[:, j].reshape(n_dec, -1, dkv), win_sink[j], n_batch=n_dec,
                               seq=dec_seq, n_kv=n_kv, groups=groups, k_col=d, v_col=d + dkv)
            o, new_k, new_v = _dense_attention(proj, win_sink[j], o, row0=ts, n_seq=n_prompt, seq=seq, q_col=0,
                                               k_col=d, v_col=d + dkv, n_kv=n_kv, groups=groups)
            parts, w_out = [o], win_w_out[j]
            win_k.append(new_k.reshape(n_prompt, seq, n_kv, HEAD_DIM))
            win_v.append(new_v.reshape(n_prompt, seq, n_kv, HEAD_DIM))
        x = _matmul_resid(parts, w_out, x, mod[li], gate_row=5, mult=1.0, tn=512, single_buffer_a=False, vmem_mib=56,
                          n_tiles=s_tiles + p_tiles, a_tile0=0, x_tile0=0, out_tile0=0, out_rows=t)
        if li < depth - 1:
            x = ffn(x, li, 1, 2)

    li = depth - 1
    hidden = ffn_in(x, li, 1, 2, n_rows=t, out_row0=0, out_rows=t)
    ys = ffn_out(hidden, x, li, 1, 2, n_tiles=s_tiles, a_tile0=0, x_tile0=0, out_tile0=0, out_rows=ts)
    yp = ffn_out(hidden, x, li, 1, 2, n_tiles=p_tiles, a_tile0=s_tiles, x_tile0=s_tiles, out_tile0=0, out_rows=tp)
    return (yp.reshape(n_prompt, seq, d), ys.reshape(n_dec, dec_seq, d), jnp.stack(na_k, axis=1),
            jnp.stack(na_v, axis=1), jnp.stack(win_k, axis=1), jnp.stack(win_v, axis=1))
```

```python
import functools

import jax
import jax.numpy as jnp
from jax import lax
from jax.experimental import pallas as pl
from jax.experimental.pallas import tpu as pltpu

EPS = 1e-6
NEG = -1e30
LOG2E = 1.4426950408889634
HEAD_DIM = 128
LANES = 128
GRID_W = 64
POOL_WINDOWS = (2, 4, 8, 16)
NA_ROWS = 8
NA_COLS = 16
C_WINDOW = 128
C_BLOCK = 128
ROPE_BASE = 10000.0
N_SUB = 3
N_MOD = 3 * N_SUB

SEQ_BLOCK = 2048
ROW_TILE = 1024
IN_TILE = 2048
PRO_ROWS = 256
PROJ_COLS = 256
STAT_ROWS = 128
NORM_ROWS = 16
NORM_COLS = 1024
NA_QROWS = 4
NA_KROWS = 12
MOD_ROWS = 16
MIB = 1024 * 1024
BF16 = jnp.bfloat16
F32 = jnp.float32


def _cparams(sem, vmem_mib):
    return pltpu.CompilerParams(dimension_semantics=sem, vmem_limit_bytes=vmem_mib * MIB)


def _alias_args(alias, n_inputs):
    if alias is None:
        return [], [], {}
    return [alias], [pl.BlockSpec(memory_space=pl.ANY)], {n_inputs: 0}


def _mod_kernel(c_ref, w_ref, b_ref, o_ref):
    c = c_ref[...]
    a = (c * jax.nn.sigmoid(c)).astype(BF16)
    w = w_ref[...].astype(BF16)
    o_ref[...] = jnp.dot(a, w, preferred_element_type=F32) + b_ref[...]


def _modulation(c_all, mod_w, mod_b, tn=512):
    depth, d, n = mod_w.shape
    return pl.pallas_call(
        _mod_kernel,
        grid=(depth, n // tn),
        in_specs=[
            pl.BlockSpec((MOD_ROWS, d), lambda l, j: (0, 0)),
            pl.BlockSpec((None, d, tn), lambda l, j: (l, 0, j)),
            pl.BlockSpec((None, 1, tn), lambda l, j: (l, 0, j)),
        ],
        out_specs=pl.BlockSpec((None, MOD_ROWS, tn), lambda l, j: (l, 0, j)),
        out_shape=jax.ShapeDtypeStruct((depth, MOD_ROWS, n), F32),
        compiler_params=_cparams(("arbitrary", "arbitrary"), 40),
        name="modulation",
    )(c_all, mod_w, mod_b.reshape(depth, 1, n))


def _adaln_rows(x_ref, ada_ref, h_ref, r_ref, row0):
    rows, d = x_ref.shape
    inv_d = 1.0 / d

    def stats(c, carry):
        r = pl.multiple_of(c * STAT_ROWS, STAT_ROWS)
        x = x_ref[pl.ds(r, STAT_ROWS), :]
        ms = jnp.sum(x * x, axis=-1, keepdims=True) * inv_d
        r_ref[pl.ds(r, STAT_ROWS), :] = jnp.broadcast_to(lax.rsqrt(ms + EPS), (STAT_ROWS, LANES))
        return carry

    def scale(c, carry):
        r = pl.multiple_of(c * NORM_ROWS, NORM_ROWS)
        rinv = jnp.concatenate([r_ref[pl.ds(r, NORM_ROWS), :]] * (NORM_COLS // LANES), axis=1)
        for c0 in range(0, d, NORM_COLS):
            cs = slice(c0, c0 + NORM_COLS)
            gain = jnp.concatenate([ada_ref[0, 0, :, cs]] * (NORM_ROWS // 8), axis=0)
            shift = jnp.concatenate([ada_ref[0, 1, :, cs]] * (NORM_ROWS // 8), axis=0)
            x = x_ref[pl.ds(r, NORM_ROWS), cs]
            h_ref[pl.ds(pl.multiple_of(row0 + r, NORM_ROWS), NORM_ROWS), cs] = (
                (x * rinv) * gain + shift).astype(h_ref.dtype)
        return carry

    lax.fori_loop(0, rows // STAT_ROWS, stats, 0)
    lax.fori_loop(0, rows // NORM_ROWS, scale, 0, unroll=2)


def _as_bf16(w):
    return w if w.dtype == BF16 else w.astype(BF16)


def _adaln_specs(d, tile0):
    n_pro = IN_TILE // PRO_ROWS
    per = SEQ_BLOCK // IN_TILE
    return n_pro, [
        pl.BlockSpec((PRO_ROWS, d), lambda i, j: (i * n_pro + jnp.minimum(j, n_pro - 1), 0)),
        pl.BlockSpec((1, 2, 8, d), lambda i, j: ((i + tile0) // per, 0, 0, 0)),
    ]


def _swiglu_kernel(x_ref, ada_ref, wg_ref, wu_ref, *rest, n_pro):
    o_ref, h_ref, r_ref = rest[-3:]
    j = pl.program_id(1)

    @pl.when(j < n_pro)
    def _():
        _adaln_rows(x_ref, ada_ref, h_ref, r_ref, j * PRO_ROWS)

    @pl.when(j >= n_pro)
    def _():
        h = h_ref[...]
        gate = jnp.dot(h, _as_bf16(wg_ref[...]), preferred_element_type=F32)
        up = jnp.dot(h, _as_bf16(wu_ref[...]), preferred_element_type=F32)
        o_ref[...] = ((gate * jax.nn.sigmoid(gate)) * up).astype(o_ref.dtype)


def _ffn_in(x, ada, w_in, w_sel, *, n_rows, out_row0, out_rows, alias=None, tn=256):
    tm = IN_TILE
    d = x.shape[1]
    dff = w_in.shape[1] // 2
    nj = dff // tn
    tile0 = out_row0 // tm
    n_pro, specs = _adaln_specs(d, tile0)
    extra, extra_specs, aliases = _alias_args(alias, 4)

    def wcol(j):
        return jnp.maximum(j - n_pro, 0)

    return pl.pallas_call(
        functools.partial(_swiglu_kernel, n_pro=n_pro),
        grid=(n_rows // tm, n_pro + nj),
        in_specs=specs + [
            pl.BlockSpec((d, tn), lambda i, j: (w_sel, wcol(j))),
            pl.BlockSpec((d, tn), lambda i, j: (w_sel, wcol(j) + nj)),
        ] + extra_specs,
        out_specs=pl.BlockSpec((tm, tn), lambda i, j: (i + tile0, wcol(j))),
        out_shape=jax.ShapeDtypeStruct((out_rows, dff), BF16),
        scratch_shapes=[pltpu.VMEM((tm, d), BF16), pltpu.VMEM((PRO_ROWS, LANES), F32)],
        input_output_aliases=aliases,
        compiler_params=_cparams(("arbitrary", "arbitrary"), 56),
        name="ffn_in",
    )(x, ada, w_in, w_in, *extra)


def _headnorm_kernel(x_ref, ada_ref, w_ref, hg_ref, o_ref, h_ref, r_ref, *, n_pro, norm_tiles):
    j = pl.program_id(1)

    @pl.when(j < n_pro)
    def _():
        _adaln_rows(x_ref, ada_ref, h_ref, r_ref, j * PRO_ROWS)

    @pl.when((j >= n_pro) & (j < n_pro + norm_tiles))
    def _():
        for c0 in range(0, w_ref.shape[1], PROJ_COLS):
            y = jnp.dot(h_ref[...], _as_bf16(w_ref[:, c0:c0 + PROJ_COLS]), preferred_element_type=F32)
            for c in range(c0, c0 + PROJ_COLS, HEAD_DIM):
                sl = slice(c, c + HEAD_DIM)
                yc = y[:, c - c0:c - c0 + HEAD_DIM]
                ms = jnp.mean(yc * yc, axis=-1, keepdims=True)
                o_ref[:, sl] = (yc * lax.rsqrt(ms + EPS)) * hg_ref[:, sl]

    @pl.when(j >= n_pro + norm_tiles)
    def _():
        for c0 in range(0, w_ref.shape[1], PROJ_COLS):
            sl = slice(c0, c0 + PROJ_COLS)
            o_ref[:, sl] = jnp.dot(h_ref[...], _as_bf16(w_ref[:, sl]), preferred_element_type=F32)


def _proj_in(x, ada, w, head_gain, *, tn, norm_cols, col_perm=None):
    tm = IN_TILE
    t, d = x.shape
    n = w.shape[1]
    n_pro, specs = _adaln_specs(d, 0)
    if col_perm is None:
        col_perm = lambda c: c

    def wcol(j):
        return jnp.maximum(j - n_pro, 0)

    return pl.pallas_call(
        functools.partial(_headnorm_kernel, n_pro=n_pro, norm_tiles=norm_cols // tn),
        grid=(t // tm, n_pro + n // tn),
        in_specs=specs + [
            pl.BlockSpec((d, tn), lambda i, j: (0, col_perm(wcol(j)))),
            pl.BlockSpec((1, tn), lambda i, j: (0, wcol(j))),
        ],
        out_specs=pl.BlockSpec((tm, tn), lambda i, j: (i, wcol(j))),
        out_shape=jax.ShapeDtypeStruct((t, n), F32),
        scratch_shapes=[pltpu.VMEM((tm, d), BF16), pltpu.VMEM((PRO_ROWS, LANES), F32)],
        compiler_params=_cparams(("arbitrary", "arbitrary"), 56),
        name="proj_in",
    )(x, ada, w, head_gain)


def _resid_kernel(*refs, n_parts, gate_row, mult):
    a_refs = refs[:n_parts]
    w_refs = refs[n_parts:2 * n_parts]
    x_ref, mod_ref = refs[2 * n_parts:2 * n_parts + 2]
    o_ref = refs[-1]
    acc = jnp.dot(a_refs[0][...], _as_bf16(w_refs[0][...]), preferred_element_type=F32)
    for a_ref, w_ref in zip(a_refs[1:], w_refs[1:]):
        acc = acc + jnp.dot(a_ref[...], _as_bf16(w_ref[...]), preferred_element_type=F32)
    gate = mod_ref[0, gate_row:gate_row + 1, :]
    if mult != 1.0:
        gate = mult * gate
    o_ref[...] = x_ref[...] + gate * acc


def _matmul_resid(parts, w, x, mod, *, gate_row, mult, tm, tn, single_buffer_a, n_rows, a_row0, x_row0,
                  out_row0, out_rows, vmem_mib, w_sel=0, alias=None):
    n_tiles, a_tile0, x_tile0, out_tile0 = n_rows // tm, a_row0 // tm, x_row0 // tm, out_row0 // tm
    d = w.shape[1]
    per = SEQ_BLOCK // tm
    a_mode = dict(pipeline_mode=pl.Buffered(1)) if single_buffer_a else {}
    k_total = sum(a.shape[1] for a in parts)
    in_specs, w_specs, off = [], [], w_sel * k_total
    for a in parts:
        k = a.shape[1]
        assert off % k == 0
        in_specs.append(pl.BlockSpec((tm, k), lambda i, j: (i + a_tile0, 0), **a_mode))
        w_specs.append(pl.BlockSpec((k, tn), functools.partial(lambda i, j, kb: (kb, j), kb=off // k)))
        off += k
    in_specs += w_specs + [
        pl.BlockSpec((tm, tn), lambda i, j: (i + x_tile0, j)),
        pl.BlockSpec((1, N_MOD, tn), lambda i, j: ((i + a_tile0) // per, 0, j)),
    ]
    extra, extra_specs, aliases = _alias_args(alias, len(in_specs))
    return pl.pallas_call(
        functools.partial(_resid_kernel, n_parts=len(parts), gate_row=gate_row, mult=mult),
        grid=(n_tiles, d // tn),
        in_specs=in_specs + extra_specs,
        out_specs=pl.BlockSpec((tm, tn), lambda i, j: (i + out_tile0, j)),
        out_shape=jax.ShapeDtypeStruct((out_rows, d), F32),
        input_output_aliases=aliases,
        compiler_params=_cparams(("arbitrary", "arbitrary"), vmem_mib),
        name="matmul_resid",
    )(*parts, *([w] * len(parts)), x, mod, *extra)


def _pool_kernel(u_ref, w_ref, s_ref, o_ref, *, n_sample_blocks, prompt_seq):
    i = pl.program_id(0)
    g = pl.program_id(1)
    rows = u_ref.shape[0]
    seq_len = jnp.where(i < n_sample_blocks, rows, prompt_seq)
    pos = lax.broadcasted_iota(jnp.int32, (rows, 1), 0) & (seq_len - 1)

    for gi, win in enumerate(POOL_WINDOWS):
        @pl.when(g == gi)
        def _(win=win):
            u = u_ref[...]
            acc = jnp.zeros_like(u)
            cnt = jnp.zeros((rows, 1), F32)
            for k in range(-(win // 2), win - win // 2):
                nb = pos + k
                valid = (nb >= 0) & (nb < seq_len)
                shifted = u if k == 0 else pltpu.roll(u, (-k) % rows, 0)
                acc = acc + jnp.where(valid, shifted, 0.0)
                cnt = cnt + valid.astype(F32)
            diff = (acc / cnt - u).astype(BF16)
            y = jnp.dot(diff, w_ref[...], preferred_element_type=F32)
            o_ref[...] = (y * s_ref[...]).astype(o_ref.dtype)


def _pool_mixer(proj, pool_w, pool_scale, *, u_col, n_sample_blocks, prompt_seq):
    t = proj.shape[0]
    n_groups, cg, _ = pool_w.shape
    g0 = u_col // cg
    return pl.pallas_call(
        functools.partial(_pool_kernel, n_sample_blocks=n_sample_blocks, prompt_seq=prompt_seq),
        grid=(t // SEQ_BLOCK, n_groups),
        in_specs=[
            pl.BlockSpec((SEQ_BLOCK, cg), lambda i, g: (i, g0 + g)),
            pl.BlockSpec((None, cg, cg), lambda i, g: (g, 0, 0)),
            pl.BlockSpec((1, cg), lambda i, g: (0, g)),
        ],
        out_specs=pl.BlockSpec((SEQ_BLOCK, cg), lambda i, g: (i, g)),
        out_shape=jax.ShapeDtypeStruct((t, n_groups * cg), BF16),
        compiler_params=_cparams(("arbitrary", "arbitrary"), 48),
        name="pool_mixer",
    )(proj, pool_w, pool_scale)


def _qkt(q, k):
    return lax.dot_general(q, k, (((1,), (1,)), ((), ())), preferred_element_type=F32)


def _with_ones(v):
    return jnp.concatenate([v.astype(BF16), jnp.ones(v.shape, BF16)], axis=1)


def _softmax_pv(s_list, v_list, sink=None):
    tiles = [s[:, c:c + LANES] for s in s_list for c in range(0, s.shape[1], LANES)]
    mt = tiles[0]
    for tile in tiles[1:]:
        mt = jnp.maximum(mt, tile)
    m = mt.max(axis=-1, keepdims=True)
    if sink is not None:
        m = jnp.maximum(m, sink)
    acc = None
    for s, v in zip(s_list, v_list):
        part = jnp.dot(jnp.exp2(s - m).astype(BF16), v, preferred_element_type=F32)
        acc = part if acc is None else acc + part
    o, l = acc[:, :HEAD_DIM], acc[:, HEAD_DIM:]
    if sink is not None:
        l = l + jnp.exp2(sink - m)
    return o / l


def _dense_attn_kernel(*refs, n_kv, groups, has_sink):
    if has_sink:
        sink_ref, q_ref, k_ref, v_ref = refs[:4]
    else:
        q_ref, k_ref, v_ref = refs[:3]
    o_ref, nk_ref, nv_ref = refs[-3:]
    seq = q_ref.shape[0]
    qscale = HEAD_DIM ** -0.5 * LOG2E
    nk_ref[...] = k_ref[...]
    nv_ref[...] = v_ref[...]
    for h in range(n_kv):
        ks = slice(h * HEAD_DIM, (h + 1) * HEAD_DIM)
        k = k_ref[:, ks].astype(BF16)
        v = _with_ones(v_ref[:, ks])
        qs = [q_ref[:, (h * groups + g) * HEAD_DIM:(h * groups + g + 1) * HEAD_DIM] for g in range(groups)]
        q = ((qs[0] if groups == 1 else jnp.concatenate(qs, axis=0)) * qscale).astype(BF16)
        s = _qkt(q, k)
        sink = None
        if has_sink:
            sink = jnp.concatenate(
                [jnp.full((seq, 1), sink_ref[h * groups + g] * LOG2E, F32) for g in range(groups)], axis=0)
        o = _softmax_pv([s], [v], sink)
        for g in range(groups):
            c0 = (h * groups + g) * HEAD_DIM
            o_ref[:, c0:c0 + HEAD_DIM] = o[g * seq:(g + 1) * seq].astype(o_ref.dtype)


def _dense_attention(proj, sink, o_buf, *, row0, n_seq, seq, q_col, k_col, v_col, n_kv, groups):
    dq = n_kv * groups * HEAD_DIM
    dkv = n_kv * HEAD_DIM
    rb = row0 // seq
    in_specs = [
        pl.BlockSpec((seq, dq), lambda b: (rb + b, q_col // dq)),
        pl.BlockSpec((seq, dkv), lambda b: (rb + b, k_col // dkv)),
        pl.BlockSpec((seq, dkv), lambda b: (rb + b, v_col // dkv)),
    ]
    args = [proj, proj, proj]
    if sink is not None:
        in_specs = [pl.BlockSpec(memory_space=pltpu.SMEM)] + in_specs
        args = [sink] + args
    extra, extra_specs, aliases = _alias_args(o_buf, len(args))
    kv_shape = jax.ShapeDtypeStruct((n_seq * seq, dkv), F32)
    return pl.pallas_call(
        functools.partial(_dense_attn_kernel, n_kv=n_kv, groups=groups, has_sink=sink is not None),
        grid=(n_seq,),
        in_specs=in_specs + extra_specs,
        out_specs=[
            pl.BlockSpec((seq, dq), lambda b: (rb + b, 0)),
            pl.BlockSpec((seq, dkv), lambda b: (b, 0)),
            pl.BlockSpec((seq, dkv), lambda b: (b, 0)),
        ],
        out_shape=[jax.ShapeDtypeStruct(o_buf.shape, o_buf.dtype), kv_shape, kv_shape],
        input_output_aliases=aliases,
        compiler_params=_cparams(("arbitrary",), 32),
        name="dense_attention",
    )(*args, *extra)


def _na_block_start(qb, rows):
    return min(max(qb * NA_QROWS - NA_ROWS // 2, 0), rows - NA_KROWS)


def _na_build_bias(rb_ref, bias_ref, rows):
    n_qb = rows // NA_QROWS
    qc = lax.broadcasted_iota(jnp.int32, (GRID_W, LANES), 0)
    lane = lax.broadcasted_iota(jnp.int32, (GRID_W, LANES), 1)
    kc = lane & (GRID_W - 1)
    ws = jnp.clip(qc - NA_COLS // 2, 0, GRID_W - NA_COLS)
    col_valid = (kc >= ws) & (kc < ws + NA_COLS)
    first_half = lane < GRID_W
    for kind, qb in enumerate((0, 1, n_qb - 1)):
        k_start = _na_block_start(qb, rows)
        for a in range(NA_QROWS):
            r = qb * NA_QROWS + a
            rs = min(max(r - NA_ROWS // 2, 0), rows - NA_ROWS)
            for pair in range(NA_KROWS // 2):
                vec = None
                valid = None
                for half in range(2):
                    kr = k_start + 2 * pair + half
                    if not rs <= kr < rs + NA_ROWS:
                        continue
                    drow = kr - r + NA_ROWS - 1
                    piece = pltpu.roll(rb_ref[0, drow:drow + 1, :], (half * GRID_W - (NA_COLS - 1)) % LANES, 1)
                    vec = piece if vec is None else vec + piece
                    hv = first_half if half == 0 else jnp.logical_not(first_half)
                    valid = hv if valid is None else jnp.logical_or(valid, hv)
                dst = (kind, slice(a * GRID_W, (a + 1) * GRID_W), slice(pair * LANES, (pair + 1) * LANES))
                if vec is None:
                    bias_ref[dst] = jnp.full((GRID_W, LANES), NEG, F32)
                else:
                    toeplitz = pltpu.roll(jnp.broadcast_to(vec, (GRID_W, LANES)), 0, 1, stride=1, stride_axis=0)
                    bias_ref[dst] = jnp.where(valid & col_valid, toeplitz * LOG2E, NEG)


def _na_kernel(q_ref, k_ref, v_ref, ck_ref, cv_ref, rb_ref, o_ref, kb_ref, vb_ref, bias_ref):
    rows = q_ref.shape[0] // GRID_W
    n_qb = rows // NA_QROWS
    qn = NA_QROWS * GRID_W
    kn = NA_KROWS * GRID_W
    qscale = HEAD_DIM ** -0.5 * LOG2E

    @pl.when(pl.program_id(1) == 0)
    def _():
        _na_build_bias(rb_ref, bias_ref, rows)

    kb_ref[...] = k_ref[...].astype(BF16)
    vb_ref[...] = _with_ones(v_ref[...])
    ck = ck_ref[0].astype(BF16)
    cv = _with_ones(cv_ref[0])
    for qb in range(n_qb):
        k0 = _na_block_start(qb, rows) * GRID_W
        kind = 0 if qb == 0 else (2 if qb == n_qb - 1 else 1)
        q = (q_ref[qb * qn:(qb + 1) * qn, :] * qscale).astype(BF16)
        s_loc = _qkt(q, kb_ref[k0:k0 + kn, :]) + bias_ref[kind]
        s_ctx = _qkt(q, ck)
        o = _softmax_pv([s_loc, s_ctx], [vb_ref[k0:k0 + kn, :], cv])
        o_ref[qb * qn:(qb + 1) * qn, :] = o.astype(o_ref.dtype)


def _na_attention(proj, ck, cv, rel_bias, *, n_batch, seq, n_heads, q_col, k_col, v_col):
    qn, kn = NA_QROWS * GRID_W, NA_KROWS * GRID_W
    qo, ko, vo = q_col // HEAD_DIM, k_col // HEAD_DIM, v_col // HEAD_DIM
    rb = jnp.zeros((n_heads, 2 * NA_ROWS, LANES), F32).at[:, :2 * NA_ROWS - 1, :2 * NA_COLS - 1].set(rel_bias)
    return pl.pallas_call(
        _na_kernel,
        grid=(n_heads, n_batch),
        in_specs=[
            pl.BlockSpec((seq, HEAD_DIM), lambda h, b: (b, qo + h)),
            pl.BlockSpec((seq, HEAD_DIM), lambda h, b: (b, ko + h)),
            pl.BlockSpec((seq, HEAD_DIM), lambda h, b: (b, vo + h)),
            pl.BlockSpec((1, ck.shape[1], HEAD_DIM), lambda h, b: (b, 0, h)),
            pl.BlockSpec((1, cv.shape[1], HEAD_DIM), lambda h, b: (b, 0, h)),
            pl.BlockSpec((1, 2 * NA_ROWS, LANES), lambda h, b: (h, 0, 0)),
        ],
        out_specs=pl.BlockSpec((seq, HEAD_DIM), lambda h, b: (b, h)),
        out_shape=jax.ShapeDtypeStruct((proj.shape[0], n_heads * HEAD_DIM), BF16),
        scratch_shapes=[pltpu.VMEM((seq, HEAD_DIM), BF16), pltpu.VMEM((seq, 2 * HEAD_DIM), BF16),
                        pltpu.VMEM((3, qn, kn), F32)],
        compiler_params=_cparams(("arbitrary", "arbitrary"), 32),
        name="na_attention",
    )(proj, proj, proj, ck, cv, rb)


def _rope_tables(seq):
    half = HEAD_DIM // 2
    quarter = half // 2
    t = jnp.arange(seq)
    freqs = ROPE_BASE ** (-jnp.arange(quarter, dtype=F32) * 2.0 / half)
    ang_r = (t // GRID_W).astype(F32)[:, None] * freqs[None]
    ang_c = (t % GRID_W).astype(F32)[:, None] * freqs[None]
    cos = jnp.concatenate([jnp.cos(ang_r)] * 2 + [jnp.cos(ang_c)] * 2, axis=-1)
    sin = jnp.concatenate([-jnp.sin(ang_r), jnp.sin(ang_r), -jnp.sin(ang_c), jnp.sin(ang_c)], axis=-1)
    return cos, sin


def _rope(x, cos, sin):
    quarter = HEAD_DIM // 4
    lane = lax.broadcasted_iota(jnp.int32, x.shape, 1)
    first = (lane & (2 * quarter - 1)) < quarter
    partner = jnp.where(first, pltpu.roll(x, HEAD_DIM - quarter, 1), pltpu.roll(x, quarter, 1))
    return x * cos + partner * sin


def _win_kernel(sink_ref, q_ref, k_ref, v_ref, ck_ref, cv_ref, rope_ref, o_ref, kb_ref, vb_ref, mask_ref,
                *, groups):
    hkv = pl.program_id(1)
    seq = q_ref.shape[0]
    span = C_BLOCK + 2 * C_WINDOW
    n_blocks = seq // C_BLOCK

    def key_start(i):
        return min(max(i * C_BLOCK - C_WINDOW, 0), seq - span)

    @pl.when((pl.program_id(0) == 0) & (hkv == 0))
    def _():
        qrow = lax.broadcasted_iota(jnp.int32, (groups * C_BLOCK, span), 0) & (C_BLOCK - 1)
        kcol = lax.broadcasted_iota(jnp.int32, (groups * C_BLOCK, span), 1)
        for kind, i in enumerate((0, 1, n_blocks - 1)):
            valid = jnp.abs(kcol - qrow + (key_start(i) - i * C_BLOCK)) <= C_WINDOW
            mask_ref[kind] = jnp.where(valid, 0.0, NEG)

    kb_ref[...] = _rope(k_ref[...], rope_ref[0], rope_ref[1]).astype(BF16)
    vb_ref[...] = _with_ones(v_ref[...])
    ck = ck_ref[0].astype(BF16)
    cv = _with_ones(cv_ref[0])
    sink = jnp.concatenate(
        [jnp.full((C_BLOCK, 1), sink_ref[hkv * groups + g] * LOG2E, F32) for g in range(groups)], axis=0)
    for i in range(n_blocks):
        q0 = i * C_BLOCK
        k0 = key_start(i)
        kind = 0 if i == 0 else (2 if i == n_blocks - 1 else 1)
        cos = rope_ref[2, q0:q0 + C_BLOCK, :]
        sin = rope_ref[3, q0:q0 + C_BLOCK, :]
        q = jnp.concatenate(
            [_rope(q_ref[q0:q0 + C_BLOCK, g * HEAD_DIM:(g + 1) * HEAD_DIM], cos, sin) for g in range(groups)],
            axis=0).astype(BF16)
        s_loc = _qkt(q, kb_ref[k0:k0 + span, :]) + mask_ref[kind]
        s_ctx = _qkt(q, ck)
        o = _softmax_pv([s_loc, s_ctx], [vb_ref[k0:k0 + span, :], cv], sink)
        for g in range(groups):
            o_ref[q0:q0 + C_BLOCK, g * HEAD_DIM:(g + 1) * HEAD_DIM] = (
                o[g * C_BLOCK:(g + 1) * C_BLOCK].astype(o_ref.dtype))


def _win_attention(proj, ck, cv, sink, *, n_batch, seq, n_kv, groups, k_col, v_col):
    cos, sin = _rope_tables(seq)
    qscale = HEAD_DIM ** -0.5 * LOG2E
    rope = jnp.stack([cos, sin, cos * qscale, sin * qscale])
    ko, vo = k_col // HEAD_DIM, v_col // HEAD_DIM
    gw = groups * HEAD_DIM
    span = C_BLOCK + 2 * C_WINDOW
    return pl.pallas_call(
        functools.partial(_win_kernel, groups=groups),
        grid=(n_batch, n_kv),
        in_specs=[
            pl.BlockSpec(memory_space=pltpu.SMEM),
            pl.BlockSpec((seq, gw), lambda b, h: (b, h)),
            pl.BlockSpec((seq, HEAD_DIM), lambda b, h: (b, ko + h)),
            pl.BlockSpec((seq, HEAD_DIM), lambda b, h: (b, vo + h)),
            pl.BlockSpec((1, ck.shape[1], HEAD_DIM), lambda b, h: (b, 0, h)),
            pl.BlockSpec((1, cv.shape[1], HEAD_DIM), lambda b, h: (b, 0, h)),
            pl.BlockSpec((4, seq, HEAD_DIM), lambda b, h: (0, 0, 0)),
        ],
        out_specs=pl.BlockSpec((seq, gw), lambda b, h: (b, h)),
        out_shape=jax.ShapeDtypeStruct((proj.shape[0], n_kv * gw), BF16),
        scratch_shapes=[pltpu.VMEM((seq, HEAD_DIM), BF16), pltpu.VMEM((seq, 2 * HEAD_DIM), BF16),
                        pltpu.VMEM((3, groups * C_BLOCK, span), F32)],
        compiler_params=_cparams(("arbitrary", "arbitrary"), 32),
        name="win_attention",
    )(sink, proj, proj, proj, ck, cv, rope)


def kernel(x_prompt, x_sample, cache_na_k, cache_na_v, cache_win_k, cache_win_v, c, c_ctx,
           norm_g, mod_w, mod_b, ffn_w_in, ffn_w_out, ab_w_in, pool_w, pool_scale,
           na_q_g, na_k_g, na_rel_bias, ab_w_out, win_w_in, win_q_g, win_k_g, win_sink, win_w_out):
    n_prompt, seq, d = x_prompt.shape
    n_dec, dec_seq, _ = x_sample.shape
    depth = norm_g.shape[0]
    assert dec_seq == SEQ_BLOCK and SEQ_BLOCK % seq == 0 and (n_prompt * seq) % SEQ_BLOCK == 0
    assert n_dec + 1 <= MOD_ROWS
    ts, tp = n_dec * dec_seq, n_prompt * seq
    t = ts + tp
    n_prompt_blocks = tp // SEQ_BLOCK

    c_all = jnp.concatenate([c, c_ctx[None], jnp.zeros((MOD_ROWS - n_dec - 1, d), F32)], axis=0)
    m_all = _modulation(c_all, mod_w, mod_b).reshape(depth, MOD_ROWS, N_MOD, d)
    mod = jnp.concatenate(
        [m_all[:, :n_dec], jnp.broadcast_to(m_all[:, n_dec:n_dec + 1], (depth, n_prompt_blocks, N_MOD, d))], axis=1)

    def ada(li, sub):
        m = mod[li]
        rows = jnp.stack([norm_g[li, sub][None] * (1.0 + m[:, 3 * sub + 1]), m[:, 3 * sub]], axis=1)
        return jnp.broadcast_to(rows[:, :, None, :], rows.shape[:2] + (8, d))

    n_ffn = ffn_w_in.shape[1]
    ffn_in_w = ffn_w_in.reshape(depth * n_ffn * d, -1)
    ffn_out_w = ffn_w_out.astype(BF16).reshape(-1, d)
    xs2, xp2 = x_sample.reshape(ts, d), x_prompt.reshape(tp, d)

    def ffn_in(x_src, li, which, sub, **kw):
        return _ffn_in(x_src, ada(li, sub), ffn_in_w, li * n_ffn + which, **kw)

    def ffn_out(hidden, x_src, li, which, sub, **kw):
        return _matmul_resid([hidden], ffn_out_w, x_src, mod[li], gate_row=3 * sub + 2, mult=0.5,
                             tm=512, tn=512, single_buffer_a=False, vmem_mib=56, w_sel=li * n_ffn + which, **kw)

    def ffn(x, li, which, sub):
        hidden = ffn_in(x, li, which, sub, n_rows=t, out_row0=0, out_rows=t)
        return ffn_out(hidden, x, li, which, sub, n_rows=t, a_row0=0, x_row0=0, out_row0=0, out_rows=t)

    na_k, na_v, win_k, win_v = [], [], [], []
    x = None
    for li in range(depth):
        if li == 0:
            hidden = ffn_in(xs2, 0, 0, 0, n_rows=ts, out_row0=0, out_rows=t)
            hidden = ffn_in(xp2, 0, 0, 0, n_rows=tp, out_row0=ts, out_rows=t, alias=hidden)
            x = ffn_out(hidden, xs2, 0, 0, 0, n_rows=ts, a_row0=0, x_row0=0, out_row0=0, out_rows=t)
            x = ffn_out(hidden, xp2, 0, 0, 0, n_rows=tp, a_row0=ts, x_row0=0, out_row0=ts, out_rows=t, alias=x)
        else:
            x = ffn(x, li, 0, 0)
        j = li // 2
        if li % 2 == 0:
            n_heads = cache_na_k.shape[3]
            d_na = n_heads * HEAD_DIM
            d_pool = ab_w_in.shape[2] - 3 * d_na
            tn = 512
            qk_t, pool_t = 2 * d_na // tn, d_pool // tn

            def col_perm(c):
                return jnp.where(c < qk_t, c + pool_t, jnp.where(c < qk_t + pool_t, c - qk_t, c))

            head_gain = jnp.concatenate(
                [jnp.tile(na_q_g[j], n_heads), jnp.tile(na_k_g[j], n_heads), jnp.ones((d_pool + d_na,), F32)])[None]
            q_col, k_col, u_col, v_col = 0, d_na, 2 * d_na, 2 * d_na + d_pool
            proj = _proj_in(x, ada(li, 1), ab_w_in[j], head_gain, tn=tn, norm_cols=2 * d_na, col_perm=col_perm)
            y_pool = _pool_mixer(proj, pool_w[j].astype(BF16), pool_scale[j][None], u_col=u_col,
                                 n_sample_blocks=n_dec, prompt_seq=seq)
            o = _na_attention(proj, cache_na_k[:, j].reshape(n_dec, -1, d_na),
                              cache_na_v[:, j].reshape(n_dec, -1, d_na), na_rel_bias[j], n_batch=n_dec, seq=dec_seq,
                              n_heads=n_heads, q_col=q_col, k_col=k_col, v_col=v_col)
            o, new_k, new_v = _dense_attention(proj, None, o, row0=ts, n_seq=n_prompt, seq=seq, q_col=q_col,
                                               k_col=k_col, v_col=v_col, n_kv=n_heads, groups=1)
            parts, w_out = [y_pool, o], ab_w_out[j]
            na_k.append(new_k.reshape(n_prompt, seq, n_heads, HEAD_DIM))
            na_v.append(new_v.reshape(n_prompt, seq, n_heads, HEAD_DIM))
        else:
            n_kv = cache_win_k.shape[3]
            dkv = n_kv * HEAD_DIM
            groups = d // dkv
            head_gain = jnp.concatenate(
                [jnp.tile(win_q_g[j], n_kv * groups), jnp.tile(win_k_g[j], n_kv), jnp.ones((dkv,), F32)])[None]
            proj = _proj_in(x, ada(li, 1), win_w_in[j], head_gain,
                            tn=512 if (d + dkv) % 512 == 0 else 256, norm_cols=d + dkv)
            o = _win_attention(proj, cache_win_k[:, j].reshape(n_dec, -1, dkv),
                               cache_win_v[:, j].reshape(n_dec, -1, dkv), win_sink[j], n_batch=n_dec,
                               seq=dec_seq, n_kv=n_kv, groups=groups, k_col=d, v_col=d + dkv)
            o, new_k, new_v = _dense_attention(proj, win_sink[j], o, row0=ts, n_seq=n_prompt, seq=seq, q_col=0,
                                               k_col=d, v_col=d + dkv, n_kv=n_kv, groups=groups)
            parts, w_out = [o], win_w_out[j]
            win_k.append(new_k.reshape(n_prompt, seq, n_kv, HEAD_DIM))
            win_v.append(new_v.reshape(n_prompt, seq, n_kv, HEAD_DIM))
        x = _matmul_resid(parts, w_out, x, mod[li], gate_row=5, mult=1.0, tm=ROW_TILE, tn=512, single_buffer_a=False,
                          vmem_mib=56, n_rows=t, a_row0=0, x_row0=0, out_row0=0, out_rows=t)
        if li < depth - 1:
            x = ffn(x, li, 1, 2)

    li = depth - 1
    hidden = ffn_in(x, li, 1, 2, n_rows=t, out_row0=0, out_rows=t)
    ys = ffn_out(hidden, x, li, 1, 2, n_rows=ts, a_row0=0, x_row0=0, out_row0=0, out_rows=ts)
    yp = ffn_out(hidden, x, li, 1, 2, n_rows=tp, a_row0=ts, x_row0=ts, out_row0=0, out_rows=tp)
    return (yp.reshape(n_prompt, seq, d), ys.reshape(n_dec, dec_seq, d), jnp.stack(na_k, axis=1),
            jnp.stack(na_v, axis=1), jnp.stack(win_k, axis=1), jnp.stack(win_v, axis=1))
```

```python
import functools

import jax
import jax.numpy as jnp
from jax import lax
from jax.experimental import pallas as pl
from jax.experimental.pallas import tpu as pltpu

EPS = 1e-6
NEG = -1e30
LOG2E = 1.4426950408889634
HEAD_DIM = 128
LANES = 128
GRID_W = 64
POOL_WINDOWS = (2, 4, 8, 16)
NA_ROWS = 8
NA_COLS = 16
C_WINDOW = 128
C_BLOCK = 128
ROPE_BASE = 10000.0
N_SUB = 3
N_MOD = 3 * N_SUB

SEQ_BLOCK = 2048
ROW_TILE = 1024
IN_TILE = 2048
PRO_ROWS = 256
STAT_ROWS = 128
NORM_ROWS = 16
NORM_COLS = 1024
NA_QROWS = 4
NA_KROWS = 12
MOD_ROWS = 16
MIB = 1024 * 1024
BF16 = jnp.bfloat16
F32 = jnp.float32


def _cparams(sem, vmem_mib):
    return pltpu.CompilerParams(dimension_semantics=sem, vmem_limit_bytes=vmem_mib * MIB)


def _alias_args(alias, n_inputs):
    if alias is None:
        return [], [], {}
    return [alias], [pl.BlockSpec(memory_space=pl.ANY)], {n_inputs: 0}


def _mod_kernel(c_ref, w_ref, b_ref, o_ref):
    c = c_ref[...]
    a = (c * jax.nn.sigmoid(c)).astype(BF16)
    w = w_ref[...].astype(BF16)
    o_ref[...] = jnp.dot(a, w, preferred_element_type=F32) + b_ref[...]


def _modulation(c_all, mod_w, mod_b, tn=512):
    depth, d, n = mod_w.shape
    return pl.pallas_call(
        _mod_kernel,
        grid=(depth, n // tn),
        in_specs=[
            pl.BlockSpec((MOD_ROWS, d), lambda l, j: (0, 0)),
            pl.BlockSpec((None, d, tn), lambda l, j: (l, 0, j)),
            pl.BlockSpec((None, 1, tn), lambda l, j: (l, 0, j)),
        ],
        out_specs=pl.BlockSpec((None, MOD_ROWS, tn), lambda l, j: (l, 0, j)),
        out_shape=jax.ShapeDtypeStruct((depth, MOD_ROWS, n), F32),
        compiler_params=_cparams(("arbitrary", "arbitrary"), 40),
        name="modulation",
    )(c_all, mod_w, mod_b.reshape(depth, 1, n))


def _adaln_rows(x_ref, ada_ref, h_ref, r_ref, row0):
    rows, d = x_ref.shape
    inv_d = 1.0 / d

    def stats(c, carry):
        r = pl.multiple_of(c * STAT_ROWS, STAT_ROWS)
        x = x_ref[pl.ds(r, STAT_ROWS), :]
        ms = jnp.sum(x * x, axis=-1, keepdims=True) * inv_d
        r_ref[pl.ds(r, STAT_ROWS), :] = jnp.broadcast_to(lax.rsqrt(ms + EPS), (STAT_ROWS, LANES))
        return carry

    def scale(c, carry):
        r = pl.multiple_of(c * NORM_ROWS, NORM_ROWS)
        rinv = jnp.concatenate([r_ref[pl.ds(r, NORM_ROWS), :]] * (NORM_COLS // LANES), axis=1)
        for c0 in range(0, d, NORM_COLS):
            cs = slice(c0, c0 + NORM_COLS)
            gain = jnp.concatenate([ada_ref[0, 0, :, cs]] * (NORM_ROWS // 8), axis=0)
            shift = jnp.concatenate([ada_ref[0, 1, :, cs]] * (NORM_ROWS // 8), axis=0)
            x = x_ref[pl.ds(r, NORM_ROWS), cs]
            h_ref[pl.ds(pl.multiple_of(row0 + r, NORM_ROWS), NORM_ROWS), cs] = (
                (x * rinv) * gain + shift).astype(h_ref.dtype)
        return carry

    lax.fori_loop(0, rows // STAT_ROWS, stats, 0)
    lax.fori_loop(0, rows // NORM_ROWS, scale, 0, unroll=2)


def _as_bf16(w):
    return w if w.dtype == BF16 else w.astype(BF16)


def _adaln_specs(d, tile0):
    n_pro = IN_TILE // PRO_ROWS
    per = SEQ_BLOCK // IN_TILE
    return n_pro, [
        pl.BlockSpec((PRO_ROWS, d), lambda i, j: (i * n_pro + jnp.minimum(j, n_pro - 1), 0)),
        pl.BlockSpec((1, 2, 8, d), lambda i, j: ((i + tile0) // per, 0, 0, 0)),
    ]


def _swiglu_kernel(x_ref, ada_ref, wg_ref, wu_ref, *rest, n_pro):
    o_ref, h_ref, r_ref = rest[-3:]
    j = pl.program_id(1)

    @pl.when(j < n_pro)
    def _():
        _adaln_rows(x_ref, ada_ref, h_ref, r_ref, j * PRO_ROWS)

    @pl.when(j >= n_pro)
    def _():
        h = h_ref[...]
        gate = jnp.dot(h, _as_bf16(wg_ref[...]), preferred_element_type=F32)
        up = jnp.dot(h, _as_bf16(wu_ref[...]), preferred_element_type=F32)
        o_ref[...] = ((gate * jax.nn.sigmoid(gate)) * up).astype(o_ref.dtype)


def _ffn_in(x, ada, w_in, w_sel, *, n_rows, out_row0, out_rows, alias=None, tn=256):
    tm = IN_TILE
    d = x.shape[1]
    dff = w_in.shape[1] // 2
    nj = dff // tn
    tile0 = out_row0 // tm
    n_pro, specs = _adaln_specs(d, tile0)
    extra, extra_specs, aliases = _alias_args(alias, 4)

    def wcol(j):
        return jnp.maximum(j - n_pro, 0)

    return pl.pallas_call(
        functools.partial(_swiglu_kernel, n_pro=n_pro),
        grid=(n_rows // tm, n_pro + nj),
        in_specs=specs + [
            pl.BlockSpec((d, tn), lambda i, j: (w_sel, wcol(j))),
            pl.BlockSpec((d, tn), lambda i, j: (w_sel, wcol(j) + nj)),
        ] + extra_specs,
        out_specs=pl.BlockSpec((tm, tn), lambda i, j: (i + tile0, wcol(j))),
        out_shape=jax.ShapeDtypeStruct((out_rows, dff), BF16),
        scratch_shapes=[pltpu.VMEM((tm, d), BF16), pltpu.VMEM((PRO_ROWS, LANES), F32)],
        input_output_aliases=aliases,
        compiler_params=_cparams(("arbitrary", "arbitrary"), 56),
        name="ffn_in",
    )(x, ada, w_in, w_in, *extra)


def _headnorm_kernel(x_ref, ada_ref, w_ref, hg_ref, o_ref, h_ref, r_ref, *, n_pro, norm_tiles):
    j = pl.program_id(1)

    @pl.when(j < n_pro)
    def _():
        _adaln_rows(x_ref, ada_ref, h_ref, r_ref, j * PRO_ROWS)

    @pl.when((j >= n_pro) & (j < n_pro + norm_tiles))
    def _():
        y = jnp.dot(h_ref[...], _as_bf16(w_ref[...]), preferred_element_type=F32)
        for c in range(0, y.shape[1], HEAD_DIM):
            sl = slice(c, c + HEAD_DIM)
            yc = y[:, sl]
            ms = jnp.mean(yc * yc, axis=-1, keepdims=True)
            o_ref[:, sl] = (yc * lax.rsqrt(ms + EPS)) * hg_ref[:, sl]

    @pl.when(j >= n_pro + norm_tiles)
    def _():
        o_ref[...] = jnp.dot(h_ref[...], _as_bf16(w_ref[...]), preferred_element_type=F32)


def _proj_in(x, ada, w, head_gain, *, tn, norm_cols, col_perm=None):
    tm = IN_TILE
    t, d = x.shape
    n = w.shape[1]
    n_pro, specs = _adaln_specs(d, 0)
    if col_perm is None:
        col_perm = lambda c: c

    def wcol(j):
        return jnp.maximum(j - n_pro, 0)

    return pl.pallas_call(
        functools.partial(_headnorm_kernel, n_pro=n_pro, norm_tiles=norm_cols // tn),
        grid=(t // tm, n_pro + n // tn),
        in_specs=specs + [
            pl.BlockSpec((d, tn), lambda i, j: (0, col_perm(wcol(j)))),
            pl.BlockSpec((1, tn), lambda i, j: (0, wcol(j))),
        ],
        out_specs=pl.BlockSpec((tm, tn), lambda i, j: (i, wcol(j))),
        out_shape=jax.ShapeDtypeStruct((t, n), F32),
        scratch_shapes=[pltpu.VMEM((tm, d), BF16), pltpu.VMEM((PRO_ROWS, LANES), F32)],
        compiler_params=_cparams(("arbitrary", "arbitrary"), 56),
        name="proj_in",
    )(x, ada, w, head_gain)


def _resid_kernel(*refs, n_parts, gate_row, mult):
    a_refs = refs[:n_parts]
    w_refs = refs[n_parts:2 * n_parts]
    x_ref, mod_ref = refs[2 * n_parts:2 * n_parts + 2]
    o_ref = refs[-1]
    acc = jnp.dot(a_refs[0][...], _as_bf16(w_refs[0][...]), preferred_element_type=F32)
    for a_ref, w_ref in zip(a_refs[1:], w_refs[1:]):
        acc = acc + jnp.dot(a_ref[...], _as_bf16(w_ref[...]), preferred_element_type=F32)
    gate = mod_ref[0, gate_row:gate_row + 1, :]
    if mult != 1.0:
        gate = mult * gate
    o_ref[...] = x_ref[...] + gate * acc


def _matmul_resid(parts, w, x, mod, *, gate_row, mult, tm, tn, single_buffer_a, n_rows, a_row0, x_row0,
                  out_row0, out_rows, vmem_mib, w_sel=0, alias=None):
    n_tiles, a_tile0, x_tile0, out_tile0 = n_rows // tm, a_row0 // tm, x_row0 // tm, out_row0 // tm
    d = w.shape[1]
    per = SEQ_BLOCK // tm
    a_mode = dict(pipeline_mode=pl.Buffered(1)) if single_buffer_a else {}
    k_total = sum(a.shape[1] for a in parts)
    in_specs, w_specs, off = [], [], w_sel * k_total
    for a in parts:
        k = a.shape[1]
        assert off % k == 0
        in_specs.append(pl.BlockSpec((tm, k), lambda i, j: (i + a_tile0, 0), **a_mode))
        w_specs.append(pl.BlockSpec((k, tn), functools.partial(lambda i, j, kb: (kb, j), kb=off // k)))
        off += k
    in_specs += w_specs + [
        pl.BlockSpec((tm, tn), lambda i, j: (i + x_tile0, j)),
        pl.BlockSpec((1, N_MOD, tn), lambda i, j: ((i + a_tile0) // per, 0, j)),
    ]
    extra, extra_specs, aliases = _alias_args(alias, len(in_specs))
    return pl.pallas_call(
        functools.partial(_resid_kernel, n_parts=len(parts), gate_row=gate_row, mult=mult),
        grid=(n_tiles, d // tn),
        in_specs=in_specs + extra_specs,
        out_specs=pl.BlockSpec((tm, tn), lambda i, j: (i + out_tile0, j)),
        out_shape=jax.ShapeDtypeStruct((out_rows, d), F32),
        input_output_aliases=aliases,
        compiler_params=_cparams(("arbitrary", "arbitrary"), vmem_mib),
        name="matmul_resid",
    )(*parts, *([w] * len(parts)), x, mod, *extra)


def _pool_kernel(u_ref, w_ref, s_ref, o_ref, *, n_sample_blocks, prompt_seq):
    i = pl.program_id(0)
    g = pl.program_id(1)
    rows = u_ref.shape[0]
    seq_len = jnp.where(i < n_sample_blocks, rows, prompt_seq)
    pos = lax.broadcasted_iota(jnp.int32, (rows, 1), 0) & (seq_len - 1)

    def shifted(x, k):
        src = pos - k
        return jnp.where((src >= 0) & (src < seq_len), pltpu.roll(x, k % rows, 0), 0.0)

    for gi, win in enumerate(POOL_WINDOWS):
        @pl.when(g == gi)
        def _(win=win):
            half = win // 2
            u = u_ref[...]
            ahead, behind, span = u, u, 1
            while span < half:
                ahead = ahead + shifted(ahead, -span)
                behind = behind + shifted(behind, span)
                span *= 2
            acc = ahead + shifted(behind, 1)
            cnt = (jnp.minimum(pos + half, seq_len) - jnp.maximum(pos - half, 0)).astype(F32)
            diff = (acc / cnt - u).astype(BF16)
            y = jnp.dot(diff, w_ref[...], preferred_element_type=F32)
            o_ref[...] = (y * s_ref[...]).astype(o_ref.dtype)


def _pool_mixer(proj, pool_w, pool_scale, *, u_col, n_sample_blocks, prompt_seq):
    t = proj.shape[0]
    n_groups, cg, _ = pool_w.shape
    g0 = u_col // cg
    return pl.pallas_call(
        functools.partial(_pool_kernel, n_sample_blocks=n_sample_blocks, prompt_seq=prompt_seq),
        grid=(t // SEQ_BLOCK, n_groups),
        in_specs=[
            pl.BlockSpec((SEQ_BLOCK, cg), lambda i, g: (i, g0 + g)),
            pl.BlockSpec((None, cg, cg), lambda i, g: (g, 0, 0)),
            pl.BlockSpec((1, cg), lambda i, g: (0, g)),
        ],
        out_specs=pl.BlockSpec((SEQ_BLOCK, cg), lambda i, g: (i, g)),
        out_shape=jax.ShapeDtypeStruct((t, n_groups * cg), BF16),
        compiler_params=_cparams(("arbitrary", "arbitrary"), 48),
        name="pool_mixer",
    )(proj, pool_w, pool_scale)


def _qkt(q, k):
    return lax.dot_general(q, k, (((1,), (1,)), ((), ())), preferred_element_type=F32)


def _with_ones(v):
    return jnp.concatenate([v.astype(BF16), jnp.ones(v.shape, BF16)], axis=1)


def _softmax_pv(s_list, v_list, sink=None):
    tiles = [s[:, c:c + LANES] for s in s_list for c in range(0, s.shape[1], LANES)]
    mt = tiles[0]
    for tile in tiles[1:]:
        mt = jnp.maximum(mt, tile)
    m = mt.max(axis=-1, keepdims=True)
    if sink is not None:
        m = jnp.maximum(m, sink)
    acc = None
    for s, v in zip(s_list, v_list):
        part = jnp.dot(jnp.exp2(s - m).astype(BF16), v, preferred_element_type=F32)
        acc = part if acc is None else acc + part
    o, l = acc[:, :HEAD_DIM], acc[:, HEAD_DIM:]
    if sink is not None:
        l = l + jnp.exp2(sink - m)
    return o / l


def _dense_attn_kernel(*refs, n_kv, groups, has_sink):
    if has_sink:
        sink_ref, q_ref, k_ref, v_ref = refs[:4]
    else:
        q_ref, k_ref, v_ref = refs[:3]
    o_ref, nk_ref, nv_ref = refs[-3:]
    seq = q_ref.shape[0]
    qscale = HEAD_DIM ** -0.5 * LOG2E
    nk_ref[...] = k_ref[...]
    nv_ref[...] = v_ref[...]
    for h in range(n_kv):
        ks = slice(h * HEAD_DIM, (h + 1) * HEAD_DIM)
        k = k_ref[:, ks].astype(BF16)
        v = _with_ones(v_ref[:, ks])
        qs = [q_ref[:, (h * groups + g) * HEAD_DIM:(h * groups + g + 1) * HEAD_DIM] for g in range(groups)]
        q = ((qs[0] if groups == 1 else jnp.concatenate(qs, axis=0)) * qscale).astype(BF16)
        s = _qkt(q, k)
        sink = None
        if has_sink:
            sink = jnp.concatenate(
                [jnp.full((seq, 1), sink_ref[h * groups + g] * LOG2E, F32) for g in range(groups)], axis=0)
        o = _softmax_pv([s], [v], sink)
        for g in range(groups):
            c0 = (h * groups + g) * HEAD_DIM
            o_ref[:, c0:c0 + HEAD_DIM] = o[g * seq:(g + 1) * seq].astype(o_ref.dtype)


def _dense_attention(proj, sink, o_buf, *, row0, n_seq, seq, q_col, k_col, v_col, n_kv, groups):
    dq = n_kv * groups * HEAD_DIM
    dkv = n_kv * HEAD_DIM
    rb = row0 // seq
    in_specs = [
        pl.BlockSpec((seq, dq), lambda b: (rb + b, q_col // dq)),
        pl.BlockSpec((seq, dkv), lambda b: (rb + b, k_col // dkv)),
        pl.BlockSpec((seq, dkv), lambda b: (rb + b, v_col // dkv)),
    ]
    args = [proj, proj, proj]
    if sink is not None:
        in_specs = [pl.BlockSpec(memory_space=pltpu.SMEM)] + in_specs
        args = [sink] + args
    extra, extra_specs, aliases = _alias_args(o_buf, len(args))
    kv_shape = jax.ShapeDtypeStruct((n_seq * seq, dkv), F32)
    return pl.pallas_call(
        functools.partial(_dense_attn_kernel, n_kv=n_kv, groups=groups, has_sink=sink is not None),
        grid=(n_seq,),
        in_specs=in_specs + extra_specs,
        out_specs=[
            pl.BlockSpec((seq, dq), lambda b: (rb + b, 0)),
            pl.BlockSpec((seq, dkv), lambda b: (b, 0)),
            pl.BlockSpec((seq, dkv), lambda b: (b, 0)),
        ],
        out_shape=[jax.ShapeDtypeStruct(o_buf.shape, o_buf.dtype), kv_shape, kv_shape],
        input_output_aliases=aliases,
        compiler_params=_cparams(("arbitrary",), 32),
        name="dense_attention",
    )(*args, *extra)


def _na_block_start(qb, rows):
    return min(max(qb * NA_QROWS - NA_ROWS // 2, 0), rows - NA_KROWS)


def _na_build_bias(rb_ref, bias_ref, rows):
    n_qb = rows // NA_QROWS
    qc = lax.broadcasted_iota(jnp.int32, (GRID_W, LANES), 0)
    lane = lax.broadcasted_iota(jnp.int32, (GRID_W, LANES), 1)
    kc = lane & (GRID_W - 1)
    ws = jnp.clip(qc - NA_COLS // 2, 0, GRID_W - NA_COLS)
    col_valid = (kc >= ws) & (kc < ws + NA_COLS)
    first_half = lane < GRID_W
    for kind, qb in enumerate((0, 1, n_qb - 1)):
        k_start = _na_block_start(qb, rows)
        for a in range(NA_QROWS):
            r = qb * NA_QROWS + a
            rs = min(max(r - NA_ROWS // 2, 0), rows - NA_ROWS)
            for pair in range(NA_KROWS // 2):
                vec = None
                valid = None
                for half in range(2):
                    kr = k_start + 2 * pair + half
                    if not rs <= kr < rs + NA_ROWS:
                        continue
                    drow = kr - r + NA_ROWS - 1
                    piece = pltpu.roll(rb_ref[0, drow:drow + 1, :], (half * GRID_W - (NA_COLS - 1)) % LANES, 1)
                    vec = piece if vec is None else vec + piece
                    hv = first_half if half == 0 else jnp.logical_not(first_half)
                    valid = hv if valid is None else jnp.logical_or(valid, hv)
                dst = (kind, slice(a * GRID_W, (a + 1) * GRID_W), slice(pair * LANES, (pair + 1) * LANES))
                if vec is None:
                    bias_ref[dst] = jnp.full((GRID_W, LANES), NEG, F32)
                else:
                    toeplitz = pltpu.roll(jnp.broadcast_to(vec, (GRID_W, LANES)), 0, 1, stride=1, stride_axis=0)
                    bias_ref[dst] = jnp.where(valid & col_valid, toeplitz * LOG2E, NEG)


def _na_kernel(q_ref, k_ref, v_ref, ck_ref, cv_ref, rb_ref, o_ref, kb_ref, vb_ref, bias_ref):
    rows = q_ref.shape[0] // GRID_W
    n_qb = rows // NA_QROWS
    qn = NA_QROWS * GRID_W
    kn = NA_KROWS * GRID_W
    qscale = HEAD_DIM ** -0.5 * LOG2E

    @pl.when(pl.program_id(1) == 0)
    def _():
        _na_build_bias(rb_ref, bias_ref, rows)

    kb_ref[...] = k_ref[...].astype(BF16)
    vb_ref[...] = _with_ones(v_ref[...])
    ck = ck_ref[0].astype(BF16)
    cv = _with_ones(cv_ref[0])
    for qb in range(n_qb):
        k0 = _na_block_start(qb, rows) * GRID_W
        kind = 0 if qb == 0 else (2 if qb == n_qb - 1 else 1)
        q = (q_ref[qb * qn:(qb + 1) * qn, :] * qscale).astype(BF16)
        s_loc = _qkt(q, kb_ref[k0:k0 + kn, :]) + bias_ref[kind]
        s_ctx = _qkt(q, ck)
        o = _softmax_pv([s_loc, s_ctx], [vb_ref[k0:k0 + kn, :], cv])
        o_ref[qb * qn:(qb + 1) * qn, :] = o.astype(o_ref.dtype)


def _na_attention(proj, ck, cv, rel_bias, *, n_batch, seq, n_heads, q_col, k_col, v_col):
    qn, kn = NA_QROWS * GRID_W, NA_KROWS * GRID_W
    qo, ko, vo = q_col // HEAD_DIM, k_col // HEAD_DIM, v_col // HEAD_DIM
    rb = jnp.zeros((n_heads, 2 * NA_ROWS, LANES), F32).at[:, :2 * NA_ROWS - 1, :2 * NA_COLS - 1].set(rel_bias)
    return pl.pallas_call(
        _na_kernel,
        grid=(n_heads, n_batch),
        in_specs=[
            pl.BlockSpec((seq, HEAD_DIM), lambda h, b: (b, qo + h)),
            pl.BlockSpec((seq, HEAD_DIM), lambda h, b: (b, ko + h)),
            pl.BlockSpec((seq, HEAD_DIM), lambda h, b: (b, vo + h)),
            pl.BlockSpec((1, ck.shape[1], HEAD_DIM), lambda h, b: (b, 0, h)),
            pl.BlockSpec((1, cv.shape[1], HEAD_DIM), lambda h, b: (b, 0, h)),
            pl.BlockSpec((1, 2 * NA_ROWS, LANES), lambda h, b: (h, 0, 0)),
        ],
        out_specs=pl.BlockSpec((seq, HEAD_DIM), lambda h, b: (b, h)),
        out_shape=jax.ShapeDtypeStruct((proj.shape[0], n_heads * HEAD_DIM), BF16),
        scratch_shapes=[pltpu.VMEM((seq, HEAD_DIM), BF16), pltpu.VMEM((seq, 2 * HEAD_DIM), BF16),
                        pltpu.VMEM((3, qn, kn), F32)],
        compiler_params=_cparams(("arbitrary", "arbitrary"), 32),
        name="na_attention",
    )(proj, proj, proj, ck, cv, rb)


def _rope_tables(seq):
    half = HEAD_DIM // 2
    quarter = half // 2
    t = jnp.arange(seq)
    freqs = ROPE_BASE ** (-jnp.arange(quarter, dtype=F32) * 2.0 / half)
    ang_r = (t // GRID_W).astype(F32)[:, None] * freqs[None]
    ang_c = (t % GRID_W).astype(F32)[:, None] * freqs[None]
    cos = jnp.concatenate([jnp.cos(ang_r)] * 2 + [jnp.cos(ang_c)] * 2, axis=-1)
    sin = jnp.concatenate([-jnp.sin(ang_r), jnp.sin(ang_r), -jnp.sin(ang_c), jnp.sin(ang_c)], axis=-1)
    return cos, sin


def _rope(x, cos, sin):
    quarter = HEAD_DIM // 4
    lane = lax.broadcasted_iota(jnp.int32, x.shape, 1)
    first = (lane & (2 * quarter - 1)) < quarter
    partner = jnp.where(first, pltpu.roll(x, HEAD_DIM - quarter, 1), pltpu.roll(x, quarter, 1))
    return x * cos + partner * sin


def _win_kernel(sink_ref, q_ref, k_ref, v_ref, ck_ref, cv_ref, rope_ref, o_ref, kb_ref, vb_ref, mask_ref,
                *, groups):
    hkv = pl.program_id(1)
    seq = q_ref.shape[0]
    span = C_BLOCK + 2 * C_WINDOW
    n_blocks = seq // C_BLOCK

    def key_start(i):
        return min(max(i * C_BLOCK - C_WINDOW, 0), seq - span)

    @pl.when((pl.program_id(0) == 0) & (hkv == 0))
    def _():
        qrow = lax.broadcasted_iota(jnp.int32, (groups * C_BLOCK, span), 0) & (C_BLOCK - 1)
        kcol = lax.broadcasted_iota(jnp.int32, (groups * C_BLOCK, span), 1)
        for kind, i in enumerate((0, 1, n_blocks - 1)):
            valid = jnp.abs(kcol - qrow + (key_start(i) - i * C_BLOCK)) <= C_WINDOW
            mask_ref[kind] = jnp.where(valid, 0.0, NEG)

    kb_ref[...] = _rope(k_ref[...], rope_ref[0], rope_ref[1]).astype(BF16)
    vb_ref[...] = _with_ones(v_ref[...])
    ck = ck_ref[0].astype(BF16)
    cv = _with_ones(cv_ref[0])
    sink = jnp.concatenate(
        [jnp.full((C_BLOCK, 1), sink_ref[hkv * groups + g] * LOG2E, F32) for g in range(groups)], axis=0)
    for i in range(n_blocks):
        q0 = i * C_BLOCK
        k0 = key_start(i)
        kind = 0 if i == 0 else (2 if i == n_blocks - 1 else 1)
        cos = rope_ref[2, q0:q0 + C_BLOCK, :]
        sin = rope_ref[3, q0:q0 + C_BLOCK, :]
        q = jnp.concatenate(
            [_rope(q_ref[q0:q0 + C_BLOCK, g * HEAD_DIM:(g + 1) * HEAD_DIM], cos, sin) for g in range(groups)],
            axis=0).astype(BF16)
        s_loc = _qkt(q, kb_ref[k0:k0 + span, :]) + mask_ref[kind]
        s_ctx = _qkt(q, ck)
        o = _softmax_pv([s_loc, s_ctx], [vb_ref[k0:k0 + span, :], cv], sink)
        for g in range(groups):
            o_ref[q0:q0 + C_BLOCK, g * HEAD_DIM:(g + 1) * HEAD_DIM] = (
                o[g * C_BLOCK:(g + 1) * C_BLOCK].astype(o_ref.dtype))


def _win_attention(proj, ck, cv, sink, *, n_batch, seq, n_kv, groups, k_col, v_col):
    cos, sin = _rope_tables(seq)
    qscale = HEAD_DIM ** -0.5 * LOG2E
    rope = jnp.stack([cos, sin, cos * qscale, sin * qscale])
    ko, vo = k_col // HEAD_DIM, v_col // HEAD_DIM
    gw = groups * HEAD_DIM
    span = C_BLOCK + 2 * C_WINDOW
    return pl.pallas_call(
        functools.partial(_win_kernel, groups=groups),
        grid=(n_batch, n_kv),
        in_specs=[
            pl.BlockSpec(memory_space=pltpu.SMEM),
            pl.BlockSpec((seq, gw), lambda b, h: (b, h)),
            pl.BlockSpec((seq, HEAD_DIM), lambda b, h: (b, ko + h)),
            pl.BlockSpec((seq, HEAD_DIM), lambda b, h: (b, vo + h)),
            pl.BlockSpec((1, ck.shape[1], HEAD_DIM), lambda b, h: (b, 0, h)),
            pl.BlockSpec((1, cv.shape[1], HEAD_DIM), lambda b, h: (b, 0, h)),
            pl.BlockSpec((4, seq, HEAD_DIM), lambda b, h: (0, 0, 0)),
        ],
        out_specs=pl.BlockSpec((seq, gw), lambda b, h: (b, h)),
        out_shape=jax.ShapeDtypeStruct((proj.shape[0], n_kv * gw), BF16),
        scratch_shapes=[pltpu.VMEM((seq, HEAD_DIM), BF16), pltpu.VMEM((seq, 2 * HEAD_DIM), BF16),
                        pltpu.VMEM((3, groups * C_BLOCK, span), F32)],
        compiler_params=_cparams(("arbitrary", "arbitrary"), 32),
        name="win_attention",
    )(sink, proj, proj, proj, ck, cv, rope)


def kernel(x_prompt, x_sample, cache_na_k, cache_na_v, cache_win_k, cache_win_v, c, c_ctx,
           norm_g, mod_w, mod_b, ffn_w_in, ffn_w_out, ab_w_in, pool_w, pool_scale,
           na_q_g, na_k_g, na_rel_bias, ab_w_out, win_w_in, win_q_g, win_k_g, win_sink, win_w_out):
    n_prompt, seq, d = x_prompt.shape
    n_dec, dec_seq, _ = x_sample.shape
    depth = norm_g.shape[0]
    assert dec_seq == SEQ_BLOCK and SEQ_BLOCK % seq == 0 and (n_prompt * seq) % SEQ_BLOCK == 0
    assert n_dec + 1 <= MOD_ROWS
    ts, tp = n_dec * dec_seq, n_prompt * seq
    t = ts + tp
    n_prompt_blocks = tp // SEQ_BLOCK

    c_all = jnp.concatenate([c, c_ctx[None], jnp.zeros((MOD_ROWS - n_dec - 1, d), F32)], axis=0)
    m_all = _modulation(c_all, mod_w, mod_b).reshape(depth, MOD_ROWS, N_MOD, d)
    mod = jnp.concatenate(
        [m_all[:, :n_dec], jnp.broadcast_to(m_all[:, n_dec:n_dec + 1], (depth, n_prompt_blocks, N_MOD, d))], axis=1)

    def ada(li, sub):
        m = mod[li]
        rows = jnp.stack([norm_g[li, sub][None] * (1.0 + m[:, 3 * sub + 1]), m[:, 3 * sub]], axis=1)
        return jnp.broadcast_to(rows[:, :, None, :], rows.shape[:2] + (8, d))

    n_ffn = ffn_w_in.shape[1]
    ffn_in_w = ffn_w_in.reshape(depth * n_ffn * d, -1)
    ffn_out_w = ffn_w_out.astype(BF16).reshape(-1, d)
    xs2, xp2 = x_sample.reshape(ts, d), x_prompt.reshape(tp, d)

    def ffn_in(x_src, li, which, sub, **kw):
        return _ffn_in(x_src, ada(li, sub), ffn_in_w, li * n_ffn + which, **kw)

    def ffn_out(hidden, x_src, li, which, sub, **kw):
        return _matmul_resid([hidden], ffn_out_w, x_src, mod[li], gate_row=3 * sub + 2, mult=0.5,
                             tm=512, tn=512, single_buffer_a=False, vmem_mib=56, w_sel=li * n_ffn + which, **kw)

    def ffn(x, li, which, sub):
        hidden = ffn_in(x, li, which, sub, n_rows=t, out_row0=0, out_rows=t)
        return ffn_out(hidden, x, li, which, sub, n_rows=t, a_row0=0, x_row0=0, out_row0=0, out_rows=t)

    na_k, na_v, win_k, win_v = [], [], [], []
    x = None
    for li in range(depth):
        if li == 0:
            hidden = ffn_in(xs2, 0, 0, 0, n_rows=ts, out_row0=0, out_rows=t)
            hidden = ffn_in(xp2, 0, 0, 0, n_rows=tp, out_row0=ts, out_rows=t, alias=hidden)
            x = ffn_out(hidden, xs2, 0, 0, 0, n_rows=ts, a_row0=0, x_row0=0, out_row0=0, out_rows=t)
            x = ffn_out(hidden, xp2, 0, 0, 0, n_rows=tp, a_row0=ts, x_row0=0, out_row0=ts, out_rows=t, alias=x)
        else:
            x = ffn(x, li, 0, 0)
        j = li // 2
        if li % 2 == 0:
            n_heads = cache_na_k.shape[3]
            d_na = n_heads * HEAD_DIM
            d_pool = ab_w_in.shape[2] - 3 * d_na
            tn = 512
            qk_t, pool_t = 2 * d_na // tn, d_pool // tn

            def col_perm(c):
                return jnp.where(c < qk_t, c + pool_t, jnp.where(c < qk_t + pool_t, c - qk_t, c))

            head_gain = jnp.concatenate(
                [jnp.tile(na_q_g[j], n_heads), jnp.tile(na_k_g[j], n_heads), jnp.ones((d_pool + d_na,), F32)])[None]
            q_col, k_col, u_col, v_col = 0, d_na, 2 * d_na, 2 * d_na + d_pool
            proj = _proj_in(x, ada(li, 1), ab_w_in[j], head_gain, tn=tn, norm_cols=2 * d_na, col_perm=col_perm)
            y_pool = _pool_mixer(proj, pool_w[j].astype(BF16), pool_scale[j][None], u_col=u_col,
                                 n_sample_blocks=n_dec, prompt_seq=seq)
            o = _na_attention(proj, cache_na_k[:, j].reshape(n_dec, -1, d_na),
                              cache_na_v[:, j].reshape(n_dec, -1, d_na), na_rel_bias[j], n_batch=n_dec, seq=dec_seq,
                              n_heads=n_heads, q_col=q_col, k_col=k_col, v_col=v_col)
            o, new_k, new_v = _dense_attention(proj, None, o, row0=ts, n_seq=n_prompt, seq=seq, q_col=q_col,
                                               k_col=k_col, v_col=v_col, n_kv=n_heads, groups=1)
            parts, w_out = [y_pool, o], ab_w_out[j].astype(BF16)
            na_k.append(new_k.reshape(n_prompt, seq, n_heads, HEAD_DIM))
            na_v.append(new_v.reshape(n_prompt, seq, n_heads, HEAD_DIM))
        else:
            n_kv = cache_win_k.shape[3]
            dkv = n_kv * HEAD_DIM
            groups = d // dkv
            head_gain = jnp.concatenate(
                [jnp.tile(win_q_g[j], n_kv * groups), jnp.tile(win_k_g[j], n_kv), jnp.ones((dkv,), F32)])[None]
            proj = _proj_in(x, ada(li, 1), win_w_in[j], head_gain,
                            tn=512 if (d + dkv) % 512 == 0 else 256, norm_cols=d + dkv)
            o = _win_attention(proj, cache_win_k[:, j].reshape(n_dec, -1, dkv),
                               cache_win_v[:, j].reshape(n_dec, -1, dkv), win_sink[j], n_batch=n_dec,
                               seq=dec_seq, n_kv=n_kv, groups=groups, k_col=d, v_col=d + dkv)
            o, new_k, new_v = _dense_attention(proj, win_sink[j], o, row0=ts, n_seq=n_prompt, seq=seq, q_col=0,
                                               k_col=d, v_col=d + dkv, n_kv=n_kv, groups=groups)
            parts, w_out = [o], win_w_out[j].astype(BF16)
            win_k.append(new_k.reshape(n_prompt, seq, n_kv, HEAD_DIM))
            win_v.append(new_v.reshape(n_prompt, seq, n_kv, HEAD_DIM))
        x = _matmul_resid(parts, w_out, x, mod[li], gate_row=5, mult=1.0, tm=ROW_TILE, tn=512, single_buffer_a=False,
                          vmem_mib=56, n_rows=t, a_row0=0, x_row0=0, out_row0=0, out_rows=t)
        if li < depth - 1:
            x = ffn(x, li, 1, 2)

    li = depth - 1
    hidden = ffn_in(x, li, 1, 2, n_rows=t, out_row0=0, out_rows=t)
    ys = ffn_out(hidden, x, li, 1, 2, n_rows=ts, a_row0=0, x_row0=0, out_row0=0, out_rows=ts)
    yp = ffn_out(hidden, x, li, 1, 2, n_rows=tp, a_row0=ts, x_row0=ts, out_row0=0, out_rows=tp)
    return (yp.reshape(n_prompt, seq, d), ys.reshape(n_dec, dec_seq, d), jnp.stack(na_k, axis=1),
            jnp.stack(na_v, axis=1), jnp.stack(win_k, axis=1), jnp.stack(win_v, axis=1))
```

```python
import functools

import jax
import jax.numpy as jnp
from jax import lax
from jax.experimental import pallas as pl
from jax.experimental.pallas import tpu as pltpu

EPS = 1e-6
NEG = -1e30
LOG2E = 1.4426950408889634
HEAD_DIM = 128
LANES = 128
GRID_W = 64
POOL_WINDOWS = (2, 4, 8, 16)
NA_ROWS = 8
NA_COLS = 16
C_WINDOW = 128
C_BLOCK = 128
ROPE_BASE = 10000.0
N_SUB = 3
N_MOD = 3 * N_SUB

SEQ_BLOCK = 2048
ROW_TILE = 1024
IN_TILE = 2048
PRO_ROWS = 256
CAST_ROWS = 64
STAT_ROWS = 128
NORM_ROWS = 16
NORM_COLS = 1024
NA_QROWS = 4
NA_KROWS = 12
MOD_ROWS = 16
MIB = 1024 * 1024
BF16 = jnp.bfloat16
F32 = jnp.float32


def _cparams(sem, vmem_mib):
    return pltpu.CompilerParams(dimension_semantics=sem, vmem_limit_bytes=vmem_mib * MIB)


def _alias_args(alias, n_inputs):
    if alias is None:
        return [], [], {}
    return [alias], [pl.BlockSpec(memory_space=pl.ANY)], {n_inputs: 0}


def _mod_kernel(c_ref, w_ref, b_ref, o_ref):
    c = c_ref[...]
    a = (c * jax.nn.sigmoid(c)).astype(BF16)
    w = w_ref[...].astype(BF16)
    o_ref[...] = jnp.dot(a, w, preferred_element_type=F32) + b_ref[...]


def _modulation(c_all, mod_w, mod_b, tn=512):
    depth, d, n = mod_w.shape
    return pl.pallas_call(
        _mod_kernel,
        grid=(depth, n // tn),
        in_specs=[
            pl.BlockSpec((MOD_ROWS, d), lambda l, j: (0, 0)),
            pl.BlockSpec((None, d, tn), lambda l, j: (l, 0, j)),
            pl.BlockSpec((None, 1, tn), lambda l, j: (l, 0, j)),
        ],
        out_specs=pl.BlockSpec((None, MOD_ROWS, tn), lambda l, j: (l, 0, j)),
        out_shape=jax.ShapeDtypeStruct((depth, MOD_ROWS, n), F32),
        compiler_params=_cparams(("arbitrary", "arbitrary"), 40),
        name="modulation",
    )(c_all, mod_w, mod_b.reshape(depth, 1, n))


def _adaln_rows(x_ref, ada_ref, h_ref, r_ref, row0):
    rows, d = x_ref.shape
    inv_d = 1.0 / d

    def stats(c, carry):
        r = pl.multiple_of(c * STAT_ROWS, STAT_ROWS)
        x = x_ref[pl.ds(r, STAT_ROWS), :]
        ms = jnp.sum(x * x, axis=-1, keepdims=True) * inv_d
        r_ref[pl.ds(r, STAT_ROWS), :] = jnp.broadcast_to(lax.rsqrt(ms + EPS), (STAT_ROWS, LANES))
        return carry

    def scale(c, carry):
        r = pl.multiple_of(c * NORM_ROWS, NORM_ROWS)
        rinv = jnp.concatenate([r_ref[pl.ds(r, NORM_ROWS), :]] * (NORM_COLS // LANES), axis=1)
        for c0 in range(0, d, NORM_COLS):
            cs = slice(c0, c0 + NORM_COLS)
            gain = jnp.concatenate([ada_ref[0, 0, :, cs]] * (NORM_ROWS // 8), axis=0)
            shift = jnp.concatenate([ada_ref[0, 1, :, cs]] * (NORM_ROWS // 8), axis=0)
            x = x_ref[pl.ds(r, NORM_ROWS), cs]
            h_ref[pl.ds(pl.multiple_of(row0 + r, NORM_ROWS), NORM_ROWS), cs] = (
                (x * rinv) * gain + shift).astype(h_ref.dtype)
        return carry

    lax.fori_loop(0, rows // STAT_ROWS, stats, 0)
    lax.fori_loop(0, rows // NORM_ROWS, scale, 0, unroll=2)


def _as_bf16(w):
    return w if w.dtype == BF16 else w.astype(BF16)


def _adaln_specs(d, tile0):
    n_pro = IN_TILE // PRO_ROWS
    per = SEQ_BLOCK // IN_TILE
    return n_pro, [
        pl.BlockSpec((PRO_ROWS, d), lambda i, j: (i * n_pro + jnp.minimum(j, n_pro - 1), 0)),
        pl.BlockSpec((1, 2, 8, d), lambda i, j: ((i + tile0) // per, 0, 0, 0)),
    ]


def _swiglu_kernel(x_ref, ada_ref, wg_ref, wu_ref, *rest, n_pro, has_cast):
    h_ref, r_ref = rest[-2:]
    o_ref = rest[-4] if has_cast else rest[-3]
    j = pl.program_id(1)

    if has_cast:
        rest[-3][...] = rest[0][...].astype(BF16)

    @pl.when(j < n_pro)
    def _():
        _adaln_rows(x_ref, ada_ref, h_ref, r_ref, j * PRO_ROWS)

    @pl.when(j >= n_pro)
    def _():
        h = h_ref[...]
        gate = jnp.dot(h, _as_bf16(wg_ref[...]), preferred_element_type=F32)
        up = jnp.dot(h, _as_bf16(wu_ref[...]), preferred_element_type=F32)
        o_ref[...] = ((gate * jax.nn.sigmoid(gate)) * up).astype(o_ref.dtype)


def _ffn_in(x, ada, w_in, w_sel, *, n_rows, out_row0, out_rows, w_out=None, alias=None, tn=256):
    tm = IN_TILE
    d = x.shape[1]
    dff = w_in.shape[1] // 2
    nj = dff // tn
    tile0 = out_row0 // tm
    n_pro, specs = _adaln_specs(d, tile0)
    steps = n_pro + nj
    grid = (n_rows // tm, steps)

    def wcol(j):
        return jnp.maximum(j - n_pro, 0)

    in_specs = specs + [
        pl.BlockSpec((d, tn), lambda i, j: (w_sel, wcol(j))),
        pl.BlockSpec((d, tn), lambda i, j: (w_sel, wcol(j) + nj)),
    ]
    args = [x, ada, w_in, w_in]
    out_specs = [pl.BlockSpec((tm, tn), lambda i, j: (i + tile0, wcol(j)))]
    out_shape = [jax.ShapeDtypeStruct((out_rows, dff), BF16)]
    if w_out is not None:
        cast_rows = CAST_ROWS
        while dff // cast_rows > grid[0] * steps:
            cast_rows *= 2
        n_blk = dff // cast_rows

        def cast_blk(i, j):
            return jnp.minimum(i * steps + j, n_blk - 1)

        in_specs.append(pl.BlockSpec((cast_rows, d), lambda i, j: (w_sel * n_blk + cast_blk(i, j), 0)))
        args.append(w_out)
        out_specs.append(pl.BlockSpec((cast_rows, d), lambda i, j: (cast_blk(i, j), 0)))
        out_shape.append(jax.ShapeDtypeStruct((dff, d), BF16))
    extra, extra_specs, aliases = _alias_args(alias, len(args))
    outs = pl.pallas_call(
        functools.partial(_swiglu_kernel, n_pro=n_pro, has_cast=w_out is not None),
        grid=grid,
        in_specs=in_specs + extra_specs,
        out_specs=out_specs,
        out_shape=out_shape,
        scratch_shapes=[pltpu.VMEM((tm, d), BF16), pltpu.VMEM((PRO_ROWS, LANES), F32)],
        input_output_aliases=aliases,
        compiler_params=_cparams(("arbitrary", "arbitrary"), 56),
        name="ffn_in",
    )(*args, *extra)
    return outs if w_out is not None else outs[0]


def _headnorm_kernel(x_ref, ada_ref, w_ref, hg_ref, o_ref, h_ref, r_ref, *, n_pro, norm_tiles):
    j = pl.program_id(1)

    @pl.when(j < n_pro)
    def _():
        _adaln_rows(x_ref, ada_ref, h_ref, r_ref, j * PRO_ROWS)

    @pl.when((j >= n_pro) & (j < n_pro + norm_tiles))
    def _():
        y = jnp.dot(h_ref[...], _as_bf16(w_ref[...]), preferred_element_type=F32)
        for c in range(0, y.shape[1], HEAD_DIM):
            sl = slice(c, c + HEAD_DIM)
            yc = y[:, sl]
            ms = jnp.mean(yc * yc, axis=-1, keepdims=True)
            o_ref[:, sl] = (yc * lax.rsqrt(ms + EPS)) * hg_ref[:, sl]

    @pl.when(j >= n_pro + norm_tiles)
    def _():
        o_ref[...] = jnp.dot(h_ref[...], _as_bf16(w_ref[...]), preferred_element_type=F32)


def _proj_in(x, ada, w, head_gain, *, tn, norm_cols, col_perm=None):
    tm = IN_TILE
    t, d = x.shape
    n = w.shape[1]
    n_pro, specs = _adaln_specs(d, 0)
    if col_perm is None:
        col_perm = lambda c: c

    def wcol(j):
        return jnp.maximum(j - n_pro, 0)

    return pl.pallas_call(
        functools.partial(_headnorm_kernel, n_pro=n_pro, norm_tiles=norm_cols // tn),
        grid=(t // tm, n_pro + n // tn),
        in_specs=specs + [
            pl.BlockSpec((d, tn), lambda i, j: (0, col_perm(wcol(j)))),
            pl.BlockSpec((1, tn), lambda i, j: (0, wcol(j))),
        ],
        out_specs=pl.BlockSpec((tm, tn), lambda i, j: (i, wcol(j))),
        out_shape=jax.ShapeDtypeStruct((t, n), F32),
        scratch_shapes=[pltpu.VMEM((tm, d), BF16), pltpu.VMEM((PRO_ROWS, LANES), F32)],
        compiler_params=_cparams(("arbitrary", "arbitrary"), 56),
        name="proj_in",
    )(x, ada, w, head_gain)


def _resid_kernel(*refs, n_parts, gate_row, mult):
    a_refs = refs[:n_parts]
    w_refs = refs[n_parts:2 * n_parts]
    x_ref, mod_ref = refs[2 * n_parts:2 * n_parts + 2]
    o_ref = refs[-1]
    acc = jnp.dot(a_refs[0][...], _as_bf16(w_refs[0][...]), preferred_element_type=F32)
    for a_ref, w_ref in zip(a_refs[1:], w_refs[1:]):
        acc = acc + jnp.dot(a_ref[...], _as_bf16(w_ref[...]), preferred_element_type=F32)
    gate = mod_ref[0, gate_row:gate_row + 1, :]
    if mult != 1.0:
        gate = mult * gate
    o_ref[...] = x_ref[...] + gate * acc


def _matmul_resid(parts, w, x, mod, *, gate_row, mult, tm, tn, single_buffer_a, n_rows, a_row0, x_row0,
                  out_row0, out_rows, vmem_mib, w_sel=0, alias=None):
    n_tiles, a_tile0, x_tile0, out_tile0 = n_rows // tm, a_row0 // tm, x_row0 // tm, out_row0 // tm
    d = w.shape[1]
    per = SEQ_BLOCK // tm
    a_mode = dict(pipeline_mode=pl.Buffered(1)) if single_buffer_a else {}
    k_total = sum(a.shape[1] for a in parts)
    in_specs, w_specs, off = [], [], w_sel * k_total
    for a in parts:
        k = a.shape[1]
        assert off % k == 0
        in_specs.append(pl.BlockSpec((tm, k), lambda i, j: (i + a_tile0, 0), **a_mode))
        w_specs.append(pl.BlockSpec((k, tn), functools.partial(lambda i, j, kb: (kb, j), kb=off // k)))
        off += k
    in_specs += w_specs + [
        pl.BlockSpec((tm, tn), lambda i, j: (i + x_tile0, j)),
        pl.BlockSpec((1, N_MOD, tn), lambda i, j: ((i + a_tile0) // per, 0, j)),
    ]
    extra, extra_specs, aliases = _alias_args(alias, len(in_specs))
    return pl.pallas_call(
        functools.partial(_resid_kernel, n_parts=len(parts), gate_row=gate_row, mult=mult),
        grid=(n_tiles, d // tn),
        in_specs=in_specs + extra_specs,
        out_specs=pl.BlockSpec((tm, tn), lambda i, j: (i + out_tile0, j)),
        out_shape=jax.ShapeDtypeStruct((out_rows, d), F32),
        input_output_aliases=aliases,
        compiler_params=_cparams(("arbitrary", "arbitrary"), vmem_mib),
        name="matmul_resid",
    )(*parts, *([w] * len(parts)), x, mod, *extra)


def _pool_kernel(u_ref, w_ref, s_ref, o_ref, *, n_sample_blocks, prompt_seq):
    i = pl.program_id(0)
    g = pl.program_id(1)
    rows = u_ref.shape[0]
    seq_len = jnp.where(i < n_sample_blocks, rows, prompt_seq)
    pos = lax.broadcasted_iota(jnp.int32, (rows, 1), 0) & (seq_len - 1)

    def shifted(x, k):
        src = pos - k
        return jnp.where((src >= 0) & (src < seq_len), pltpu.roll(x, k % rows, 0), 0.0)

    for gi, win in enumerate(POOL_WINDOWS):
        @pl.when(g == gi)
        def _(win=win):
            half = win // 2
            u = u_ref[...]
            ahead, behind, span = u, u, 1
            while span < half:
                ahead = ahead + shifted(ahead, -span)
                behind = behind + shifted(behind, span)
                span *= 2
            acc = ahead + shifted(behind, 1)
            cnt = (jnp.minimum(pos + half, seq_len) - jnp.maximum(pos - half, 0)).astype(F32)
            diff = (acc / cnt - u).astype(BF16)
            y = jnp.dot(diff, w_ref[...], preferred_element_type=F32)
            o_ref[...] = (y * s_ref[...]).astype(o_ref.dtype)


def _pool_mixer(proj, pool_w, pool_scale, *, u_col, n_sample_blocks, prompt_seq):
    t = proj.shape[0]
    n_groups, cg, _ = pool_w.shape
    g0 = u_col // cg
    return pl.pallas_call(
        functools.partial(_pool_kernel, n_sample_blocks=n_sample_blocks, prompt_seq=prompt_seq),
        grid=(t // SEQ_BLOCK, n_groups),
        in_specs=[
            pl.BlockSpec((SEQ_BLOCK, cg), lambda i, g: (i, g0 + g)),
            pl.BlockSpec((None, cg, cg), lambda i, g: (g, 0, 0)),
            pl.BlockSpec((1, cg), lambda i, g: (0, g)),
        ],
        out_specs=pl.BlockSpec((SEQ_BLOCK, cg), lambda i, g: (i, g)),
        out_shape=jax.ShapeDtypeStruct((t, n_groups * cg), BF16),
        compiler_params=_cparams(("arbitrary", "arbitrary"), 48),
        name="pool_mixer",
    )(proj, pool_w, pool_scale)


def _qkt(q, k):
    return lax.dot_general(q, k, (((1,), (1,)), ((), ())), preferred_element_type=F32)


def _with_ones(v):
    return jnp.concatenate([v.astype(BF16), jnp.ones(v.shape, BF16)], axis=1)


def _softmax_pv(s_list, v_list, sink=None):
    tiles = [s[:, c:c + LANES] for s in s_list for c in range(0, s.shape[1], LANES)]
    mt = tiles[0]
    for tile in tiles[1:]:
        mt = jnp.maximum(mt, tile)
    m = mt.max(axis=-1, keepdims=True)
    if sink is not None:
        m = jnp.maximum(m, sink)
    acc = None
    for s, v in zip(s_list, v_list):
        part = jnp.dot(jnp.exp2(s - m).astype(BF16), v, preferred_element_type=F32)
        acc = part if acc is None else acc + part
    o, l = acc[:, :HEAD_DIM], acc[:, HEAD_DIM:]
    if sink is not None:
        l = l + jnp.exp2(sink - m)
    return o / l


def _dense_attn_kernel(*refs, n_kv, groups, has_sink):
    if has_sink:
        sink_ref, q_ref, k_ref, v_ref = refs[:4]
    else:
        q_ref, k_ref, v_ref = refs[:3]
    o_ref, nk_ref, nv_ref = refs[-3:]
    seq = q_ref.shape[0]
    qscale = HEAD_DIM ** -0.5 * LOG2E
    nk_ref[...] = k_ref[...]
    nv_ref[...] = v_ref[...]
    for h in range(n_kv):
        ks = slice(h * HEAD_DIM, (h + 1) * HEAD_DIM)
        k = k_ref[:, ks].astype(BF16)
        v = _with_ones(v_ref[:, ks])
        qs = [q_ref[:, (h * groups + g) * HEAD_DIM:(h * groups + g + 1) * HEAD_DIM] for g in range(groups)]
        q = ((qs[0] if groups == 1 else jnp.concatenate(qs, axis=0)) * qscale).astype(BF16)
        s = _qkt(q, k)
        sink = None
        if has_sink:
            sink = jnp.concatenate(
                [jnp.full((seq, 1), sink_ref[h * groups + g] * LOG2E, F32) for g in range(groups)], axis=0)
        o = _softmax_pv([s], [v], sink)
        for g in range(groups):
            c0 = (h * groups + g) * HEAD_DIM
            o_ref[:, c0:c0 + HEAD_DIM] = o[g * seq:(g + 1) * seq].astype(o_ref.dtype)


def _dense_attention(proj, sink, o_buf, *, row0, n_seq, seq, q_col, k_col, v_col, n_kv, groups):
    dq = n_kv * groups * HEAD_DIM
    dkv = n_kv * HEAD_DIM
    rb = row0 // seq
    in_specs = [
        pl.BlockSpec((seq, dq), lambda b: (rb + b, q_col // dq)),
        pl.BlockSpec((seq, dkv), lambda b: (rb + b, k_col // dkv)),
        pl.BlockSpec((seq, dkv), lambda b: (rb + b, v_col // dkv)),
    ]
    args = [proj, proj, proj]
    if sink is not None:
        in_specs = [pl.BlockSpec(memory_space=pltpu.SMEM)] + in_specs
        args = [sink] + args
    extra, extra_specs, aliases = _alias_args(o_buf, len(args))
    kv_shape = jax.ShapeDtypeStruct((n_seq * seq, dkv), F32)
    return pl.pallas_call(
        functools.partial(_dense_attn_kernel, n_kv=n_kv, groups=groups, has_sink=sink is not None),
        grid=(n_seq,),
        in_specs=in_specs + extra_specs,
        out_specs=[
            pl.BlockSpec((seq, dq), lambda b: (rb + b, 0)),
            pl.BlockSpec((seq, dkv), lambda b: (b, 0)),
            pl.BlockSpec((seq, dkv), lambda b: (b, 0)),
        ],
        out_shape=[jax.ShapeDtypeStruct(o_buf.shape, o_buf.dtype), kv_shape, kv_shape],
        input_output_aliases=aliases,
        compiler_params=_cparams(("arbitrary",), 32),
        name="dense_attention",
    )(*args, *extra)


def _na_block_start(qb, rows):
    return min(max(qb * NA_QROWS - NA_ROWS // 2, 0), rows - NA_KROWS)


def _na_build_bias(rb_ref, bias_ref, rows):
    n_qb = rows // NA_QROWS
    qc = lax.broadcasted_iota(jnp.int32, (GRID_W, LANES), 0)
    lane = lax.broadcasted_iota(jnp.int32, (GRID_W, LANES), 1)
    kc = lane & (GRID_W - 1)
    ws = jnp.clip(qc - NA_COLS // 2, 0, GRID_W - NA_COLS)
    col_valid = (kc >= ws) & (kc < ws + NA_COLS)
    first_half = lane < GRID_W
    for kind, qb in enumerate((0, 1, n_qb - 1)):
        k_start = _na_block_start(qb, rows)
        for a in range(NA_QROWS):
            r = qb * NA_QROWS + a
            rs = min(max(r - NA_ROWS // 2, 0), rows - NA_ROWS)
            for pair in range(NA_KROWS // 2):
                vec = None
                valid = None
                for half in range(2):
                    kr = k_start + 2 * pair + half
                    if not rs <= kr < rs + NA_ROWS:
                        continue
                    drow = kr - r + NA_ROWS - 1
                    piece = pltpu.roll(rb_ref[0, drow:drow + 1, :], (half * GRID_W - (NA_COLS - 1)) % LANES, 1)
                    vec = piece if vec is None else vec + piece
                    hv = first_half if half == 0 else jnp.logical_not(first_half)
                    valid = hv if valid is None else jnp.logical_or(valid, hv)
                dst = (kind, slice(a * GRID_W, (a + 1) * GRID_W), slice(pair * LANES, (pair + 1) * LANES))
                if vec is None:
                    bias_ref[dst] = jnp.full((GRID_W, LANES), NEG, F32)
                else:
                    toeplitz = pltpu.roll(jnp.broadcast_to(vec, (GRID_W, LANES)), 0, 1, stride=1, stride_axis=0)
                    bias_ref[dst] = jnp.where(valid & col_valid, toeplitz * LOG2E, NEG)


def _na_kernel(q_ref, k_ref, v_ref, ck_ref, cv_ref, rb_ref, o_ref, kb_ref, vb_ref, bias_ref):
    rows = q_ref.shape[0] // GRID_W
    n_qb = rows // NA_QROWS
    qn = NA_QROWS * GRID_W
    kn = NA_KROWS * GRID_W
    qscale = HEAD_DIM ** -0.5 * LOG2E

    @pl.when(pl.program_id(1) == 0)
    def _():
        _na_build_bias(rb_ref, bias_ref, rows)

    kb_ref[...] = k_ref[...].astype(BF16)
    vb_ref[...] = _with_ones(v_ref[...])
    ck = ck_ref[0].astype(BF16)
    cv = _with_ones(cv_ref[0])
    for qb in range(n_qb):
        k0 = _na_block_start(qb, rows) * GRID_W
        kind = 0 if qb == 0 else (2 if qb == n_qb - 1 else 1)
        q = (q_ref[qb * qn:(qb + 1) * qn, :] * qscale).astype(BF16)
        s_loc = _qkt(q, kb_ref[k0:k0 + kn, :]) + bias_ref[kind]
        s_ctx = _qkt(q, ck)
        o = _softmax_pv([s_loc, s_ctx], [vb_ref[k0:k0 + kn, :], cv])
        o_ref[qb * qn:(qb + 1) * qn, :] = o.astype(o_ref.dtype)


def _na_attention(proj, ck, cv, rel_bias, *, n_batch, seq, n_heads, q_col, k_col, v_col):
    qn, kn = NA_QROWS * GRID_W, NA_KROWS * GRID_W
    qo, ko, vo = q_col // HEAD_DIM, k_col // HEAD_DIM, v_col // HEAD_DIM
    rb = jnp.zeros((n_heads, 2 * NA_ROWS, LANES), F32).at[:, :2 * NA_ROWS - 1, :2 * NA_COLS - 1].set(rel_bias)
    return pl.pallas_call(
        _na_kernel,
        grid=(n_heads, n_batch),
        in_specs=[
            pl.BlockSpec((seq, HEAD_DIM), lambda h, b: (b, qo + h)),
            pl.BlockSpec((seq, HEAD_DIM), lambda h, b: (b, ko + h)),
            pl.BlockSpec((seq, HEAD_DIM), lambda h, b: (b, vo + h)),
            pl.BlockSpec((1, ck.shape[1], HEAD_DIM), lambda h, b: (b, 0, h)),
            pl.BlockSpec((1, cv.shape[1], HEAD_DIM), lambda h, b: (b, 0, h)),
            pl.BlockSpec((1, 2 * NA_ROWS, LANES), lambda h, b: (h, 0, 0)),
        ],
        out_specs=pl.BlockSpec((seq, HEAD_DIM), lambda h, b: (b, h)),
        out_shape=jax.ShapeDtypeStruct((proj.shape[0], n_heads * HEAD_DIM), BF16),
        scratch_shapes=[pltpu.VMEM((seq, HEAD_DIM), BF16), pltpu.VMEM((seq, 2 * HEAD_DIM), BF16),
                        pltpu.VMEM((3, qn, kn), F32)],
        compiler_params=_cparams(("arbitrary", "arbitrary"), 32),
        name="na_attention",
    )(proj, proj, proj, ck, cv, rb)


def _rope_tables(seq):
    half = HEAD_DIM // 2
    quarter = half // 2
    t = jnp.arange(seq)
    freqs = ROPE_BASE ** (-jnp.arange(quarter, dtype=F32) * 2.0 / half)
    ang_r = (t // GRID_W).astype(F32)[:, None] * freqs[None]
    ang_c = (t % GRID_W).astype(F32)[:, None] * freqs[None]
    cos = jnp.concatenate([jnp.cos(ang_r)] * 2 + [jnp.cos(ang_c)] * 2, axis=-1)
    sin = jnp.concatenate([-jnp.sin(ang_r), jnp.sin(ang_r), -jnp.sin(ang_c), jnp.sin(ang_c)], axis=-1)
    return cos, sin


def _rope(x, cos, sin):
    quarter = HEAD_DIM // 4
    lane = lax.broadcasted_iota(jnp.int32, x.shape, 1)
    first = (lane & (2 * quarter - 1)) < quarter
    partner = jnp.where(first, pltpu.roll(x, HEAD_DIM - quarter, 1), pltpu.roll(x, quarter, 1))
    return x * cos + partner * sin


def _win_kernel(sink_ref, q_ref, k_ref, v_ref, ck_ref, cv_ref, rope_ref, o_ref, kb_ref, vb_ref, mask_ref,
                *, groups):
    hkv = pl.program_id(1)
    seq = q_ref.shape[0]
    span = C_BLOCK + 2 * C_WINDOW
    n_blocks = seq // C_BLOCK

    def key_start(i):
        return min(max(i * C_BLOCK - C_WINDOW, 0), seq - span)

    @pl.when((pl.program_id(0) == 0) & (hkv == 0))
    def _():
        qrow = lax.broadcasted_iota(jnp.int32, (groups * C_BLOCK, span), 0) & (C_BLOCK - 1)
        kcol = lax.broadcasted_iota(jnp.int32, (groups * C_BLOCK, span), 1)
        for kind, i in enumerate((0, 1, n_blocks - 1)):
            valid = jnp.abs(kcol - qrow + (key_start(i) - i * C_BLOCK)) <= C_WINDOW
            mask_ref[kind] = jnp.where(valid, 0.0, NEG)

    kb_ref[...] = _rope(k_ref[...], rope_ref[0], rope_ref[1]).astype(BF16)
    vb_ref[...] = _with_ones(v_ref[...])
    ck = ck_ref[0].astype(BF16)
    cv = _with_ones(cv_ref[0])
    sink = jnp.concatenate(
        [jnp.full((C_BLOCK, 1), sink_ref[hkv * groups + g] * LOG2E, F32) for g in range(groups)], axis=0)
    for i in range(n_blocks):
        q0 = i * C_BLOCK
        k0 = key_start(i)
        kind = 0 if i == 0 else (2 if i == n_blocks - 1 else 1)
        cos = rope_ref[2, q0:q0 + C_BLOCK, :]
        sin = rope_ref[3, q0:q0 + C_BLOCK, :]
        q = jnp.concatenate(
            [_rope(q_ref[q0:q0 + C_BLOCK, g * HEAD_DIM:(g + 1) * HEAD_DIM], cos, sin) for g in range(groups)],
            axis=0).astype(BF16)
        s_loc = _qkt(q, kb_ref[k0:k0 + span, :]) + mask_ref[kind]
        s_ctx = _qkt(q, ck)
        o = _softmax_pv([s_loc, s_ctx], [vb_ref[k0:k0 + span, :], cv], sink)
        for g in range(groups):
            o_ref[q0:q0 + C_BLOCK, g * HEAD_DIM:(g + 1) * HEAD_DIM] = (
                o[g * C_BLOCK:(g + 1) * C_BLOCK].astype(o_ref.dtype))


def _win_attention(proj, ck, cv, sink, *, n_batch, seq, n_kv, groups, k_col, v_col):
    cos, sin = _rope_tables(seq)
    qscale = HEAD_DIM ** -0.5 * LOG2E
    rope = jnp.stack([cos, sin, cos * qscale, sin * qscale])
    ko, vo = k_col // HEAD_DIM, v_col // HEAD_DIM
    gw = groups * HEAD_DIM
    span = C_BLOCK + 2 * C_WINDOW
    return pl.pallas_call(
        functools.partial(_win_kernel, groups=groups),
        grid=(n_batch, n_kv),
        in_specs=[
            pl.BlockSpec(memory_space=pltpu.SMEM),
            pl.BlockSpec((seq, gw), lambda b, h: (b, h)),
            pl.BlockSpec((seq, HEAD_DIM), lambda b, h: (b, ko + h)),
            pl.BlockSpec((seq, HEAD_DIM), lambda b, h: (b, vo + h)),
            pl.BlockSpec((1, ck.shape[1], HEAD_DIM), lambda b, h: (b, 0, h)),
            pl.BlockSpec((1, cv.shape[1], HEAD_DIM), lambda b, h: (b, 0, h)),
            pl.BlockSpec((4, seq, HEAD_DIM), lambda b, h: (0, 0, 0)),
        ],
        out_specs=pl.BlockSpec((seq, gw), lambda b, h: (b, h)),
        out_shape=jax.ShapeDtypeStruct((proj.shape[0], n_kv * gw), BF16),
        scratch_shapes=[pltpu.VMEM((seq, HEAD_DIM), BF16), pltpu.VMEM((seq, 2 * HEAD_DIM), BF16),
                        pltpu.VMEM((3, groups * C_BLOCK, span), F32)],
        compiler_params=_cparams(("arbitrary", "arbitrary"), 32),
        name="win_attention",
    )(sink, proj, proj, proj, ck, cv, rope)


def kernel(x_prompt, x_sample, cache_na_k, cache_na_v, cache_win_k, cache_win_v, c, c_ctx,
           norm_g, mod_w, mod_b, ffn_w_in, ffn_w_out, ab_w_in, pool_w, pool_scale,
           na_q_g, na_k_g, na_rel_bias, ab_w_out, win_w_in, win_q_g, win_k_g, win_sink, win_w_out):
    n_prompt, seq, d = x_prompt.shape
    n_dec, dec_seq, _ = x_sample.shape
    depth = norm_g.shape[0]
    assert dec_seq == SEQ_BLOCK and SEQ_BLOCK % seq == 0 and (n_prompt * seq) % SEQ_BLOCK == 0
    assert n_dec + 1 <= MOD_ROWS
    ts, tp = n_dec * dec_seq, n_prompt * seq
    t = ts + tp
    n_prompt_blocks = tp // SEQ_BLOCK

    c_all = jnp.concatenate([c, c_ctx[None], jnp.zeros((MOD_ROWS - n_dec - 1, d), F32)], axis=0)
    m_all = _modulation(c_all, mod_w, mod_b).reshape(depth, MOD_ROWS, N_MOD, d)
    mod = jnp.concatenate(
        [m_all[:, :n_dec], jnp.broadcast_to(m_all[:, n_dec:n_dec + 1], (depth, n_prompt_blocks, N_MOD, d))], axis=1)

    def ada(li, sub):
        m = mod[li]
        rows = jnp.stack([norm_g[li, sub][None] * (1.0 + m[:, 3 * sub + 1]), m[:, 3 * sub]], axis=1)
        return jnp.broadcast_to(rows[:, :, None, :], rows.shape[:2] + (8, d))

    n_ffn = ffn_w_in.shape[1]
    ffn_in_w = ffn_w_in.reshape(depth * n_ffn * d, -1)
    ffn_out_w = ffn_w_out.reshape(-1, d)
    xs2, xp2 = x_sample.reshape(ts, d), x_prompt.reshape(tp, d)

    def ffn_in(x_src, li, which, sub, cast=True, **kw):
        return _ffn_in(x_src, ada(li, sub), ffn_in_w, li * n_ffn + which, w_out=ffn_out_w if cast else None, **kw)

    def ffn_out(hidden, w_out, x_src, li, sub, **kw):
        return _matmul_resid([hidden], w_out, x_src, mod[li], gate_row=3 * sub + 2, mult=0.5,
                             tm=1024, tn=256, single_buffer_a=False, vmem_mib=62, **kw)

    def ffn(x, li, which, sub):
        hidden, w_out = ffn_in(x, li, which, sub, n_rows=t, out_row0=0, out_rows=t)
        return ffn_out(hidden, w_out, x, li, sub, n_rows=t, a_row0=0, x_row0=0, out_row0=0, out_rows=t)

    na_k, na_v, win_k, win_v = [], [], [], []
    x = None
    for li in range(depth):
        if li == 0:
            hidden, w_out = ffn_in(xs2, 0, 0, 0, n_rows=ts, out_row0=0, out_rows=t)
            hidden = ffn_in(xp2, 0, 0, 0, cast=False, n_rows=tp, out_row0=ts, out_rows=t, alias=hidden)
            x = ffn_out(hidden, w_out, xs2, 0, 0, n_rows=ts, a_row0=0, x_row0=0, out_row0=0, out_rows=t)
            x = ffn_out(hidden, w_out, xp2, 0, 0, n_rows=tp, a_row0=ts, x_row0=0, out_row0=ts, out_rows=t, alias=x)
        else:
            x = ffn(x, li, 0, 0)
        j = li // 2
        if li % 2 == 0:
            n_heads = cache_na_k.shape[3]
            d_na = n_heads * HEAD_DIM
            d_pool = ab_w_in.shape[2] - 3 * d_na
            tn = 512
            qk_t, pool_t = 2 * d_na // tn, d_pool // tn

            def col_perm(c):
                return jnp.where(c < qk_t, c + pool_t, jnp.where(c < qk_t + pool_t, c - qk_t, c))

            head_gain = jnp.concatenate(
                [jnp.tile(na_q_g[j], n_heads), jnp.tile(na_k_g[j], n_heads), jnp.ones((d_pool + d_na,), F32)])[None]
            q_col, k_col, u_col, v_col = 0, d_na, 2 * d_na, 2 * d_na + d_pool
            proj = _proj_in(x, ada(li, 1), ab_w_in[j], head_gain, tn=tn, norm_cols=2 * d_na, col_perm=col_perm)
            y_pool = _pool_mixer(proj, pool_w[j].astype(BF16), pool_scale[j][None], u_col=u_col,
                                 n_sample_blocks=n_dec, prompt_seq=seq)
            o = _na_attention(proj, cache_na_k[:, j].reshape(n_dec, -1, d_na),
                              cache_na_v[:, j].reshape(n_dec, -1, d_na), na_rel_bias[j], n_batch=n_dec, seq=dec_seq,
                              n_heads=n_heads, q_col=q_col, k_col=k_col, v_col=v_col)
            o, new_k, new_v = _dense_attention(proj, None, o, row0=ts, n_seq=n_prompt, seq=seq, q_col=q_col,
                                               k_col=k_col, v_col=v_col, n_kv=n_heads, groups=1)
            parts, w_out = [y_pool, o], ab_w_out[j].astype(BF16)
            na_k.append(new_k.reshape(n_prompt, seq, n_heads, HEAD_DIM))
            na_v.append(new_v.reshape(n_prompt, seq, n_heads, HEAD_DIM))
        else:
            n_kv = cache_win_k.shape[3]
            dkv = n_kv * HEAD_DIM
            groups = d // dkv
            head_gain = jnp.concatenate(
                [jnp.tile(win_q_g[j], n_kv * groups), jnp.tile(win_k_g[j], n_kv), jnp.ones((dkv,), F32)])[None]
            proj = _proj_in(x, ada(li, 1), win_w_in[j], head_gain,
                            tn=512 if (d + dkv) % 512 == 0 else 256, norm_cols=d + dkv)
            o = _win_attention(proj, cache_win_k[:, j].reshape(n_dec, -1, dkv),
                               cache_win_v[:, j].reshape(n_dec, -1, dkv), win_sink[j], n_batch=n_dec,
                               seq=dec_seq, n_kv=n_kv, groups=groups, k_col=d, v_col=d + dkv)
            o, new_k, new_v = _dense_attention(proj, win_sink[j], o, row0=ts, n_seq=n_prompt, seq=seq, q_col=0,
                                               k_col=d, v_col=d + dkv, n_kv=n_kv, groups=groups)
            parts, w_out = [o], win_w_out[j].astype(BF16)
            win_k.append(new_k.reshape(n_prompt, seq, n_kv, HEAD_DIM))
            win_v.append(new_v.reshape(n_prompt, seq, n_kv, HEAD_DIM))
        x = _matmul_resid(parts, w_out, x, mod[li], gate_row=5, mult=1.0, tm=ROW_TILE, tn=512, single_buffer_a=False,
                          vmem_mib=56, n_rows=t, a_row0=0, x_row0=0, out_row0=0, out_rows=t)
        if li < depth - 1:
            x = ffn(x, li, 1, 2)

    li = depth - 1
    hidden, w_out = ffn_in(x, li, 1, 2, n_rows=t, out_row0=0, out_rows=t)
    ys = ffn_out(hidden, w_out, x, li, 2, n_rows=ts, a_row0=0, x_row0=0, out_row0=0, out_rows=ts)
    yp = ffn_out(hidden, w_out, x, li, 2, n_rows=tp, a_row0=ts, x_row0=ts, out_row0=0, out_rows=tp)
    return (yp.reshape(n_prompt, seq, d), ys.reshape(n_dec, dec_seq, d), jnp.stack(na_k, axis=1),
            jnp.stack(na_v, axis=1), jnp.stack(win_k, axis=1), jnp.stack(win_v, axis=1))
```

```python
import functools

import jax
import jax.numpy as jnp
from jax import lax
from jax.experimental import pallas as pl
from jax.experimental.pallas import tpu as pltpu

EPS = 1e-6
NEG = -1e30
LOG2E = 1.4426950408889634
HEAD_DIM = 128
LANES = 128
GRID_W = 64
POOL_WINDOWS = (2, 4, 8, 16)
NA_ROWS = 8
NA_COLS = 16
C_WINDOW = 128
C_BLOCK = 128
ROPE_BASE = 10000.0
N_SUB = 3
N_MOD = 3 * N_SUB

SEQ_BLOCK = 2048
ROW_TILE = 1024
IN_TILE = 2048
PRO_ROWS = 256
CAST_ROWS = 64
OUT_COLS = 256
STAT_ROWS = 128
NORM_ROWS = 16
NORM_COLS = 1024
NA_QROWS = 4
NA_KROWS = 12
MOD_ROWS = 16
MIB = 1024 * 1024
BF16 = jnp.bfloat16
F32 = jnp.float32


def _cparams(sem, vmem_mib):
    return pltpu.CompilerParams(dimension_semantics=sem, vmem_limit_bytes=vmem_mib * MIB)


def _alias_args(alias, n_inputs):
    if alias is None:
        return [], [], {}
    return [alias], [pl.BlockSpec(memory_space=pl.ANY)], {n_inputs: 0}


def _mod_kernel(c_ref, w_ref, b_ref, o_ref):
    c = c_ref[...]
    a = (c * jax.nn.sigmoid(c)).astype(BF16)
    w = w_ref[...].astype(BF16)
    o_ref[...] = jnp.dot(a, w, preferred_element_type=F32) + b_ref[...]


def _modulation(c_all, mod_w, mod_b, tn=512):
    depth, d, n = mod_w.shape
    return pl.pallas_call(
        _mod_kernel,
        grid=(depth, n // tn),
        in_specs=[
            pl.BlockSpec((MOD_ROWS, d), lambda l, j: (0, 0)),
            pl.BlockSpec((None, d, tn), lambda l, j: (l, 0, j)),
            pl.BlockSpec((None, 1, tn), lambda l, j: (l, 0, j)),
        ],
        out_specs=pl.BlockSpec((None, MOD_ROWS, tn), lambda l, j: (l, 0, j)),
        out_shape=jax.ShapeDtypeStruct((depth, MOD_ROWS, n), F32),
        compiler_params=_cparams(("arbitrary", "arbitrary"), 40),
        name="modulation",
    )(c_all, mod_w, mod_b.reshape(depth, 1, n))


def _adaln_rows(x_ref, ada_ref, h_ref, r_ref, row0):
    rows, d = x_ref.shape
    inv_d = 1.0 / d

    def stats(c, carry):
        r = pl.multiple_of(c * STAT_ROWS, STAT_ROWS)
        x = x_ref[pl.ds(r, STAT_ROWS), :]
        ms = jnp.sum(x * x, axis=-1, keepdims=True) * inv_d
        r_ref[pl.ds(r, STAT_ROWS), :] = jnp.broadcast_to(lax.rsqrt(ms + EPS), (STAT_ROWS, LANES))
        return carry

    def scale(c, carry):
        r = pl.multiple_of(c * NORM_ROWS, NORM_ROWS)
        rinv = jnp.concatenate([r_ref[pl.ds(r, NORM_ROWS), :]] * (NORM_COLS // LANES), axis=1)
        for c0 in range(0, d, NORM_COLS):
            cs = slice(c0, c0 + NORM_COLS)
            gain = jnp.concatenate([ada_ref[0, 0, :, cs]] * (NORM_ROWS // 8), axis=0)
            shift = jnp.concatenate([ada_ref[0, 1, :, cs]] * (NORM_ROWS // 8), axis=0)
            x = x_ref[pl.ds(r, NORM_ROWS), cs]
            h_ref[pl.ds(pl.multiple_of(row0 + r, NORM_ROWS), NORM_ROWS), cs] = (
                (x * rinv) * gain + shift).astype(h_ref.dtype)
        return carry

    lax.fori_loop(0, rows // STAT_ROWS, stats, 0)
    lax.fori_loop(0, rows // NORM_ROWS, scale, 0, unroll=2)


def _as_bf16(w):
    return w if w.dtype == BF16 else w.astype(BF16)


def _adaln_specs(d, tile0):
    n_pro = IN_TILE // PRO_ROWS
    per = SEQ_BLOCK // IN_TILE
    return n_pro, [
        pl.BlockSpec((PRO_ROWS, d), lambda i, j: (i * n_pro + jnp.minimum(j, n_pro - 1), 0)),
        pl.BlockSpec((1, 2, 8, d), lambda i, j: ((i + tile0) // per, 0, 0, 0)),
    ]


def _swiglu_kernel(x_ref, ada_ref, wg_ref, wu_ref, *rest, n_pro, has_cast):
    h_ref, r_ref = rest[-2:]
    o_ref = rest[-4] if has_cast else rest[-3]
    j = pl.program_id(1)

    if has_cast:
        src, dst = rest[0], rest[-3]
        for c in range(dst.shape[0]):
            dst[c] = src[:, c * dst.shape[2]:(c + 1) * dst.shape[2]].astype(BF16)

    @pl.when(j < n_pro)
    def _():
        _adaln_rows(x_ref, ada_ref, h_ref, r_ref, j * PRO_ROWS)

    @pl.when(j >= n_pro)
    def _():
        h = h_ref[...]
        gate = jnp.dot(h, _as_bf16(wg_ref[...]), preferred_element_type=F32)
        up = jnp.dot(h, _as_bf16(wu_ref[...]), preferred_element_type=F32)
        o_ref[...] = ((gate * jax.nn.sigmoid(gate)) * up).astype(o_ref.dtype)


def _ffn_in(x, ada, w_in, w_sel, *, n_rows, out_row0, out_rows, w_out=None, alias=None, tn=256):
    tm = IN_TILE
    d = x.shape[1]
    dff = w_in.shape[1] // 2
    nj = dff // tn
    tile0 = out_row0 // tm
    n_pro, specs = _adaln_specs(d, tile0)
    steps = n_pro + nj
    grid = (n_rows // tm, steps)

    def wcol(j):
        return jnp.maximum(j - n_pro, 0)

    in_specs = specs + [
        pl.BlockSpec((d, tn), lambda i, j: (w_sel, wcol(j))),
        pl.BlockSpec((d, tn), lambda i, j: (w_sel, wcol(j) + nj)),
    ]
    args = [x, ada, w_in, w_in]
    out_specs = [pl.BlockSpec((tm, tn), lambda i, j: (i + tile0, wcol(j)))]
    out_shape = [jax.ShapeDtypeStruct((out_rows, dff), BF16)]
    if w_out is not None:
        cast_rows = CAST_ROWS
        while dff // cast_rows > grid[0] * steps:
            cast_rows *= 2
        n_blk = dff // cast_rows

        def cast_blk(i, j):
            return jnp.minimum(i * steps + j, n_blk - 1)

        in_specs.append(pl.BlockSpec((cast_rows, d), lambda i, j: (w_sel * n_blk + cast_blk(i, j), 0)))
        args.append(w_out)
        out_specs.append(pl.BlockSpec((d // OUT_COLS, cast_rows, OUT_COLS), lambda i, j: (0, cast_blk(i, j), 0)))
        out_shape.append(jax.ShapeDtypeStruct((d // OUT_COLS, dff, OUT_COLS), BF16))
    extra, extra_specs, aliases = _alias_args(alias, len(args))
    outs = pl.pallas_call(
        functools.partial(_swiglu_kernel, n_pro=n_pro, has_cast=w_out is not None),
        grid=grid,
        in_specs=in_specs + extra_specs,
        out_specs=out_specs,
        out_shape=out_shape,
        scratch_shapes=[pltpu.VMEM((tm, d), BF16), pltpu.VMEM((PRO_ROWS, LANES), F32)],
        input_output_aliases=aliases,
        compiler_params=_cparams(("arbitrary", "arbitrary"), 56),
        name="ffn_in",
    )(*args, *extra)
    return outs if w_out is not None else outs[0]


def _headnorm_kernel(x_ref, ada_ref, w_ref, hg_ref, o_ref, h_ref, r_ref, *, n_pro, norm_tiles):
    j = pl.program_id(1)

    @pl.when(j < n_pro)
    def _():
        _adaln_rows(x_ref, ada_ref, h_ref, r_ref, j * PRO_ROWS)

    @pl.when((j >= n_pro) & (j < n_pro + norm_tiles))
    def _():
        y = jnp.dot(h_ref[...], _as_bf16(w_ref[...]), preferred_element_type=F32)
        for c in range(0, y.shape[1], HEAD_DIM):
            sl = slice(c, c + HEAD_DIM)
            yc = y[:, sl]
            ms = jnp.mean(yc * yc, axis=-1, keepdims=True)
            o_ref[:, sl] = (yc * lax.rsqrt(ms + EPS)) * hg_ref[:, sl]

    @pl.when(j >= n_pro + norm_tiles)
    def _():
        o_ref[...] = jnp.dot(h_ref[...], _as_bf16(w_ref[...]), preferred_element_type=F32)


def _proj_in(x, ada, w, head_gain, *, tn, norm_cols, col_perm=None):
    tm = IN_TILE
    t, d = x.shape
    n = w.shape[1]
    n_pro, specs = _adaln_specs(d, 0)
    if col_perm is None:
        col_perm = lambda c: c

    def wcol(j):
        return jnp.maximum(j - n_pro, 0)

    return pl.pallas_call(
        functools.partial(_headnorm_kernel, n_pro=n_pro, norm_tiles=norm_cols // tn),
        grid=(t // tm, n_pro + n // tn),
        in_specs=specs + [
            pl.BlockSpec((d, tn), lambda i, j: (0, col_perm(wcol(j)))),
            pl.BlockSpec((1, tn), lambda i, j: (0, wcol(j))),
        ],
        out_specs=pl.BlockSpec((tm, tn), lambda i, j: (i, wcol(j))),
        out_shape=jax.ShapeDtypeStruct((t, n), F32),
        scratch_shapes=[pltpu.VMEM((tm, d), BF16), pltpu.VMEM((PRO_ROWS, LANES), F32)],
        compiler_params=_cparams(("arbitrary", "arbitrary"), 56),
        name="proj_in",
    )(x, ada, w, head_gain)


def _resid_kernel(*refs, n_parts, gate_row, mult):
    a_refs = refs[:n_parts]
    w_refs = refs[n_parts:2 * n_parts]
    x_ref, mod_ref = refs[2 * n_parts:2 * n_parts + 2]
    o_ref = refs[-1]
    acc = jnp.dot(a_refs[0][...], _as_bf16(w_refs[0][...]), preferred_element_type=F32)
    for a_ref, w_ref in zip(a_refs[1:], w_refs[1:]):
        acc = acc + jnp.dot(a_ref[...], _as_bf16(w_ref[...]), preferred_element_type=F32)
    gate = mod_ref[0, gate_row:gate_row + 1, :]
    if mult != 1.0:
        gate = mult * gate
    o_ref[...] = x_ref[...] + gate * acc


def _matmul_resid(parts, w, x, mod, *, gate_row, mult, tm, tn, single_buffer_a, n_rows, a_row0, x_row0,
                  out_row0, out_rows, vmem_mib, w_sel=0, w_tiled=False, alias=None):
    n_tiles, a_tile0, x_tile0, out_tile0 = n_rows // tm, a_row0 // tm, x_row0 // tm, out_row0 // tm
    d = w.shape[0] * tn if w_tiled else w.shape[1]
    per = SEQ_BLOCK // tm
    a_mode = dict(pipeline_mode=pl.Buffered(1)) if single_buffer_a else {}
    k_total = sum(a.shape[1] for a in parts)
    in_specs, w_specs, off = [], [], w_sel * k_total
    for a in parts:
        k = a.shape[1]
        assert off % k == 0
        in_specs.append(pl.BlockSpec((tm, k), lambda i, j: (i + a_tile0, 0), **a_mode))
        if w_tiled:
            w_specs.append(pl.BlockSpec((None, k, tn), functools.partial(lambda i, j, kb: (j, kb, 0), kb=off // k)))
        else:
            w_specs.append(pl.BlockSpec((k, tn), functools.partial(lambda i, j, kb: (kb, j), kb=off // k)))
        off += k
    in_specs += w_specs + [
        pl.BlockSpec((tm, tn), lambda i, j: (i + x_tile0, j)),
        pl.BlockSpec((1, N_MOD, tn), lambda i, j: ((i + a_tile0) // per, 0, j)),
    ]
    extra, extra_specs, aliases = _alias_args(alias, len(in_specs))
    return pl.pallas_call(
        functools.partial(_resid_kernel, n_parts=len(parts), gate_row=gate_row, mult=mult),
        grid=(n_tiles, d // tn),
        in_specs=in_specs + extra_specs,
        out_specs=pl.BlockSpec((tm, tn), lambda i, j: (i + out_tile0, j)),
        out_shape=jax.ShapeDtypeStruct((out_rows, d), F32),
        input_output_aliases=aliases,
        compiler_params=_cparams(("arbitrary", "arbitrary"), vmem_mib),
        name="matmul_resid",
    )(*parts, *([w] * len(parts)), x, mod, *extra)


def _pool_kernel(u_ref, w_ref, s_ref, o_ref, *, n_sample_blocks, prompt_seq):
    i = pl.program_id(0)
    g = pl.program_id(1)
    rows = u_ref.shape[0]
    seq_len = jnp.where(i < n_sample_blocks, rows, prompt_seq)
    pos = lax.broadcasted_iota(jnp.int32, (rows, 1), 0) & (seq_len - 1)

    def shifted(x, k):
        src = pos - k
        return jnp.where((src >= 0) & (src < seq_len), pltpu.roll(x, k % rows, 0), 0.0)

    for gi, win in enumerate(POOL_WINDOWS):
        @pl.when(g == gi)
        def _(win=win):
            half = win // 2
            u = u_ref[...]
            ahead, behind, span = u, u, 1
            while span < half:
                ahead = ahead + shifted(ahead, -span)
                behind = behind + shifted(behind, span)
                span *= 2
            acc = ahead + shifted(behind, 1)
            cnt = (jnp.minimum(pos + half, seq_len) - jnp.maximum(pos - half, 0)).astype(F32)
            diff = (acc / cnt - u).astype(BF16)
            y = jnp.dot(diff, w_ref[...], preferred_element_type=F32)
            o_ref[...] = (y * s_ref[...]).astype(o_ref.dtype)


def _pool_mixer(proj, pool_w, pool_scale, *, u_col, n_sample_blocks, prompt_seq):
    t = proj.shape[0]
    n_groups, cg, _ = pool_w.shape
    g0 = u_col // cg
    return pl.pallas_call(
        functools.partial(_pool_kernel, n_sample_blocks=n_sample_blocks, prompt_seq=prompt_seq),
        grid=(t // SEQ_BLOCK, n_groups),
        in_specs=[
            pl.BlockSpec((SEQ_BLOCK, cg), lambda i, g: (i, g0 + g)),
            pl.BlockSpec((None, cg, cg), lambda i, g: (g, 0, 0)),
            pl.BlockSpec((1, cg), lambda i, g: (0, g)),
        ],
        out_specs=pl.BlockSpec((SEQ_BLOCK, cg), lambda i, g: (i, g)),
        out_shape=jax.ShapeDtypeStruct((t, n_groups * cg), BF16),
        compiler_params=_cparams(("arbitrary", "arbitrary"), 48),
        name="pool_mixer",
    )(proj, pool_w, pool_scale)


def _qkt(q, k):
    return lax.dot_general(q, k, (((1,), (1,)), ((), ())), preferred_element_type=F32)


def _with_ones(v):
    return jnp.concatenate([v.astype(BF16), jnp.ones(v.shape, BF16)], axis=1)


def _softmax_pv(s_list, v_list, sink=None):
    tiles = [s[:, c:c + LANES] for s in s_list for c in range(0, s.shape[1], LANES)]
    mt = tiles[0]
    for tile in tiles[1:]:
        mt = jnp.maximum(mt, tile)
    m = mt.max(axis=-1, keepdims=True)
    if sink is not None:
        m = jnp.maximum(m, sink)
    acc = None
    for s, v in zip(s_list, v_list):
        part = jnp.dot(jnp.exp2(s - m).astype(BF16), v, preferred_element_type=F32)
        acc = part if acc is None else acc + part
    o, l = acc[:, :HEAD_DIM], acc[:, HEAD_DIM:]
    if sink is not None:
        l = l + jnp.exp2(sink - m)
    return o / l


def _dense_attn_kernel(*refs, n_kv, groups, has_sink):
    if has_sink:
        sink_ref, q_ref, k_ref, v_ref = refs[:4]
    else:
        q_ref, k_ref, v_ref = refs[:3]
    o_ref, nk_ref, nv_ref = refs[-3:]
    seq = q_ref.shape[0]
    qscale = HEAD_DIM ** -0.5 * LOG2E
    nk_ref[...] = k_ref[...]
    nv_ref[...] = v_ref[...]
    for h in range(n_kv):
        ks = slice(h * HEAD_DIM, (h + 1) * HEAD_DIM)
        k = k_ref[:, ks].astype(BF16)
        v = _with_ones(v_ref[:, ks])
        qs = [q_ref[:, (h * groups + g) * HEAD_DIM:(h * groups + g + 1) * HEAD_DIM] for g in range(groups)]
        q = ((qs[0] if groups == 1 else jnp.concatenate(qs, axis=0)) * qscale).astype(BF16)
        s = _qkt(q, k)
        sink = None
        if has_sink:
            sink = jnp.concatenate(
                [jnp.full((seq, 1), sink_ref[h * groups + g] * LOG2E, F32) for g in range(groups)], axis=0)
        o = _softmax_pv([s], [v], sink)
        for g in range(groups):
            c0 = (h * groups + g) * HEAD_DIM
            o_ref[:, c0:c0 + HEAD_DIM] = o[g * seq:(g + 1) * seq].astype(o_ref.dtype)


def _dense_attention(proj, sink, o_buf, *, row0, n_seq, seq, q_col, k_col, v_col, n_kv, groups):
    dq = n_kv * groups * HEAD_DIM
    dkv = n_kv * HEAD_DIM
    rb = row0 // seq
    in_specs = [
        pl.BlockSpec((seq, dq), lambda b: (rb + b, q_col // dq)),
        pl.BlockSpec((seq, dkv), lambda b: (rb + b, k_col // dkv)),
        pl.BlockSpec((seq, dkv), lambda b: (rb + b, v_col // dkv)),
    ]
    args = [proj, proj, proj]
    if sink is not None:
        in_specs = [pl.BlockSpec(memory_space=pltpu.SMEM)] + in_specs
        args = [sink] + args
    extra, extra_specs, aliases = _alias_args(o_buf, len(args))
    kv_shape = jax.ShapeDtypeStruct((n_seq * seq, dkv), F32)
    return pl.pallas_call(
        functools.partial(_dense_attn_kernel, n_kv=n_kv, groups=groups, has_sink=sink is not None),
        grid=(n_seq,),
        in_specs=in_specs + extra_specs,
        out_specs=[
            pl.BlockSpec((seq, dq), lambda b: (rb + b, 0)),
            pl.BlockSpec((seq, dkv), lambda b: (b, 0)),
            pl.BlockSpec((seq, dkv), lambda b: (b, 0)),
        ],
        out_shape=[jax.ShapeDtypeStruct(o_buf.shape, o_buf.dtype), kv_shape, kv_shape],
        input_output_aliases=aliases,
        compiler_params=_cparams(("arbitrary",), 32),
        name="dense_attention",
    )(*args, *extra)


def _na_block_start(qb, rows):
    return min(max(qb * NA_QROWS - NA_ROWS // 2, 0), rows - NA_KROWS)


def _na_build_bias(rb_ref, bias_ref, rows):
    n_qb = rows // NA_QROWS
    qc = lax.broadcasted_iota(jnp.int32, (GRID_W, LANES), 0)
    lane = lax.broadcasted_iota(jnp.int32, (GRID_W, LANES), 1)
    kc = lane & (GRID_W - 1)
    ws = jnp.clip(qc - NA_COLS // 2, 0, GRID_W - NA_COLS)
    col_valid = (kc >= ws) & (kc < ws + NA_COLS)
    first_half = lane < GRID_W
    for kind, qb in enumerate((0, 1, n_qb - 1)):
        k_start = _na_block_start(qb, rows)
        for a in range(NA_QROWS):
            r = qb * NA_QROWS + a
            rs = min(max(r - NA_ROWS // 2, 0), rows - NA_ROWS)
            for pair in range(NA_KROWS // 2):
                vec = None
                valid = None
                for half in range(2):
                    kr = k_start + 2 * pair + half
                    if not rs <= kr < rs + NA_ROWS:
                        continue
                    drow = kr - r + NA_ROWS - 1
                    piece = pltpu.roll(rb_ref[0, drow:drow + 1, :], (half * GRID_W - (NA_COLS - 1)) % LANES, 1)
                    vec = piece if vec is None else vec + piece
                    hv = first_half if half == 0 else jnp.logical_not(first_half)
                    valid = hv if valid is None else jnp.logical_or(valid, hv)
                dst = (kind, slice(a * GRID_W, (a + 1) * GRID_W), slice(pair * LANES, (pair + 1) * LANES))
                if vec is None:
                    bias_ref[dst] = jnp.full((GRID_W, LANES), NEG, F32)
                else:
                    toeplitz = pltpu.roll(jnp.broadcast_to(vec, (GRID_W, LANES)), 0, 1, stride=1, stride_axis=0)
                    bias_ref[dst] = jnp.where(valid & col_valid, toeplitz * LOG2E, NEG)


def _na_kernel(q_ref, k_ref, v_ref, ck_ref, cv_ref, rb_ref, o_ref, kb_ref, vb_ref, bias_ref):
    rows = q_ref.shape[0] // GRID_W
    n_qb = rows // NA_QROWS
    qn = NA_QROWS * GRID_W
    kn = NA_KROWS * GRID_W
    qscale = HEAD_DIM ** -0.5 * LOG2E

    @pl.when(pl.program_id(1) == 0)
    def _():
        _na_build_bias(rb_ref, bias_ref, rows)

    kb_ref[...] = k_ref[...].astype(BF16)
    vb_ref[...] = _with_ones(v_ref[...])
    ck = ck_ref[0].astype(BF16)
    cv = _with_ones(cv_ref[0])
    for qb in range(n_qb):
        k0 = _na_block_start(qb, rows) * GRID_W
        kind = 0 if qb == 0 else (2 if qb == n_qb - 1 else 1)
        q = (q_ref[qb * qn:(qb + 1) * qn, :] * qscale).astype(BF16)
        s_loc = _qkt(q, kb_ref[k0:k0 + kn, :]) + bias_ref[kind]
        s_ctx = _qkt(q, ck)
        o = _softmax_pv([s_loc, s_ctx], [vb_ref[k0:k0 + kn, :], cv])
        o_ref[qb * qn:(qb + 1) * qn, :] = o.astype(o_ref.dtype)


def _na_attention(proj, ck, cv, rel_bias, *, n_batch, seq, n_heads, q_col, k_col, v_col):
    qn, kn = NA_QROWS * GRID_W, NA_KROWS * GRID_W
    qo, ko, vo = q_col // HEAD_DIM, k_col // HEAD_DIM, v_col // HEAD_DIM
    rb = jnp.zeros((n_heads, 2 * NA_ROWS, LANES), F32).at[:, :2 * NA_ROWS - 1, :2 * NA_COLS - 1].set(rel_bias)
    return pl.pallas_call(
        _na_kernel,
        grid=(n_heads, n_batch),
        in_specs=[
            pl.BlockSpec((seq, HEAD_DIM), lambda h, b: (b, qo + h)),
            pl.BlockSpec((seq, HEAD_DIM), lambda h, b: (b, ko + h)),
            pl.BlockSpec((seq, HEAD_DIM), lambda h, b: (b, vo + h)),
            pl.BlockSpec((1, ck.shape[1], HEAD_DIM), lambda h, b: (b, 0, h)),
            pl.BlockSpec((1, cv.shape[1], HEAD_DIM), lambda h, b: (b, 0, h)),
            pl.BlockSpec((1, 2 * NA_ROWS, LANES), lambda h, b: (h, 0, 0)),
        ],
        out_specs=pl.BlockSpec((seq, HEAD_DIM), lambda h, b: (b, h)),
        out_shape=jax.ShapeDtypeStruct((proj.shape[0], n_heads * HEAD_DIM), BF16),
        scratch_shapes=[pltpu.VMEM((seq, HEAD_DIM), BF16), pltpu.VMEM((seq, 2 * HEAD_DIM), BF16),
                        pltpu.VMEM((3, qn, kn), F32)],
        compiler_params=_cparams(("arbitrary", "arbitrary"), 32),
        name="na_attention",
    )(proj, proj, proj, ck, cv, rb)


def _rope_tables(seq):
    half = HEAD_DIM // 2
    quarter = half // 2
    t = jnp.arange(seq)
    freqs = ROPE_BASE ** (-jnp.arange(quarter, dtype=F32) * 2.0 / half)
    ang_r = (t // GRID_W).astype(F32)[:, None] * freqs[None]
    ang_c = (t % GRID_W).astype(F32)[:, None] * freqs[None]
    cos = jnp.concatenate([jnp.cos(ang_r)] * 2 + [jnp.cos(ang_c)] * 2, axis=-1)
    sin = jnp.concatenate([-jnp.sin(ang_r), jnp.sin(ang_r), -jnp.sin(ang_c), jnp.sin(ang_c)], axis=-1)
    return cos, sin


def _rope(x, cos, sin):
    quarter = HEAD_DIM // 4
    lane = lax.broadcasted_iota(jnp.int32, x.shape, 1)
    first = (lane & (2 * quarter - 1)) < quarter
    partner = jnp.where(first, pltpu.roll(x, HEAD_DIM - quarter, 1), pltpu.roll(x, quarter, 1))
    return x * cos + partner * sin


def _win_kernel(sink_ref, q_ref, k_ref, v_ref, ck_ref, cv_ref, rope_ref, o_ref, kb_ref, vb_ref, mask_ref,
                *, groups):
    hkv = pl.program_id(1)
    seq = q_ref.shape[0]
    span = C_BLOCK + 2 * C_WINDOW
    n_blocks = seq // C_BLOCK

    def key_start(i):
        return min(max(i * C_BLOCK - C_WINDOW, 0), seq - span)

    @pl.when((pl.program_id(0) == 0) & (hkv == 0))
    def _():
        qrow = lax.broadcasted_iota(jnp.int32, (groups * C_BLOCK, span), 0) & (C_BLOCK - 1)
        kcol = lax.broadcasted_iota(jnp.int32, (groups * C_BLOCK, span), 1)
        for kind, i in enumerate((0, 1, n_blocks - 1)):
            valid = jnp.abs(kcol - qrow + (key_start(i) - i * C_BLOCK)) <= C_WINDOW
            mask_ref[kind] = jnp.where(valid, 0.0, NEG)

    kb_ref[...] = _rope(k_ref[...], rope_ref[0], rope_ref[1]).astype(BF16)
    vb_ref[...] = _with_ones(v_ref[...])
    ck = ck_ref[0].astype(BF16)
    cv = _with_ones(cv_ref[0])
    sink = jnp.concatenate(
        [jnp.full((C_BLOCK, 1), sink_ref[hkv * groups + g] * LOG2E, F32) for g in range(groups)], axis=0)
    for i in range(n_blocks):
        q0 = i * C_BLOCK
        k0 = key_start(i)
        kind = 0 if i == 0 else (2 if i == n_blocks - 1 else 1)
        cos = rope_ref[2, q0:q0 + C_BLOCK, :]
        sin = rope_ref[3, q0:q0 + C_BLOCK, :]
        q = jnp.concatenate(
            [_rope(q_ref[q0:q0 + C_BLOCK, g * HEAD_DIM:(g + 1) * HEAD_DIM], cos, sin) for g in range(groups)],
            axis=0).astype(BF16)
        s_loc = _qkt(q, kb_ref[k0:k0 + span, :]) + mask_ref[kind]
        s_ctx = _qkt(q, ck)
        o = _softmax_pv([s_loc, s_ctx], [vb_ref[k0:k0 + span, :], cv], sink)
        for g in range(groups):
            o_ref[q0:q0 + C_BLOCK, g * HEAD_DIM:(g + 1) * HEAD_DIM] = (
                o[g * C_BLOCK:(g + 1) * C_BLOCK].astype(o_ref.dtype))


def _win_attention(proj, ck, cv, sink, *, n_batch, seq, n_kv, groups, k_col, v_col):
    cos, sin = _rope_tables(seq)
    qscale = HEAD_DIM ** -0.5 * LOG2E
    rope = jnp.stack([cos, sin, cos * qscale, sin * qscale])
    ko, vo = k_col // HEAD_DIM, v_col // HEAD_DIM
    gw = groups * HEAD_DIM
    span = C_BLOCK + 2 * C_WINDOW
    return pl.pallas_call(
        functools.partial(_win_kernel, groups=groups),
        grid=(n_batch, n_kv),
        in_specs=[
            pl.BlockSpec(memory_space=pltpu.SMEM),
            pl.BlockSpec((seq, gw), lambda b, h: (b, h)),
            pl.BlockSpec((seq, HEAD_DIM), lambda b, h: (b, ko + h)),
            pl.BlockSpec((seq, HEAD_DIM), lambda b, h: (b, vo + h)),
            pl.BlockSpec((1, ck.shape[1], HEAD_DIM), lambda b, h: (b, 0, h)),
            pl.BlockSpec((1, cv.shape[1], HEAD_DIM), lambda b, h: (b, 0, h)),
            pl.BlockSpec((4, seq, HEAD_DIM), lambda b, h: (0, 0, 0)),
        ],
        out_specs=pl.BlockSpec((seq, gw), lambda b, h: (b, h)),
        out_shape=jax.ShapeDtypeStruct((proj.shape[0], n_kv * gw), BF16),
        scratch_shapes=[pltpu.VMEM((seq, HEAD_DIM), BF16), pltpu.VMEM((seq, 2 * HEAD_DIM), BF16),
                        pltpu.VMEM((3, groups * C_BLOCK, span), F32)],
        compiler_params=_cparams(("arbitrary", "arbitrary"), 32),
        name="win_attention",
    )(sink, proj, proj, proj, ck, cv, rope)


def kernel(x_prompt, x_sample, cache_na_k, cache_na_v, cache_win_k, cache_win_v, c, c_ctx,
           norm_g, mod_w, mod_b, ffn_w_in, ffn_w_out, ab_w_in, pool_w, pool_scale,
           na_q_g, na_k_g, na_rel_bias, ab_w_out, win_w_in, win_q_g, win_k_g, win_sink, win_w_out):
    n_prompt, seq, d = x_prompt.shape
    n_dec, dec_seq, _ = x_sample.shape
    depth = norm_g.shape[0]
    assert dec_seq == SEQ_BLOCK and SEQ_BLOCK % seq == 0 and (n_prompt * seq) % SEQ_BLOCK == 0
    assert n_dec + 1 <= MOD_ROWS
    ts, tp = n_dec * dec_seq, n_prompt * seq
    t = ts + tp
    n_prompt_blocks = tp // SEQ_BLOCK

    c_all = jnp.concatenate([c, c_ctx[None], jnp.zeros((MOD_ROWS - n_dec - 1, d), F32)], axis=0)
    m_all = _modulation(c_all, mod_w, mod_b).reshape(depth, MOD_ROWS, N_MOD, d)
    mod = jnp.concatenate(
        [m_all[:, :n_dec], jnp.broadcast_to(m_all[:, n_dec:n_dec + 1], (depth, n_prompt_blocks, N_MOD, d))], axis=1)

    def ada(li, sub):
        m = mod[li]
        rows = jnp.stack([norm_g[li, sub][None] * (1.0 + m[:, 3 * sub + 1]), m[:, 3 * sub]], axis=1)
        return jnp.broadcast_to(rows[:, :, None, :], rows.shape[:2] + (8, d))

    n_ffn = ffn_w_in.shape[1]
    ffn_in_w = ffn_w_in.reshape(depth * n_ffn * d, -1)
    ffn_out_w = ffn_w_out.reshape(-1, d)
    xs2, xp2 = x_sample.reshape(ts, d), x_prompt.reshape(tp, d)

    def ffn_in(x_src, li, which, sub, cast=True, **kw):
        return _ffn_in(x_src, ada(li, sub), ffn_in_w, li * n_ffn + which, w_out=ffn_out_w if cast else None, **kw)

    def ffn_out(hidden, w_out, x_src, li, sub, **kw):
        return _matmul_resid([hidden], w_out, x_src, mod[li], gate_row=3 * sub + 2, mult=0.5,
                             tm=1024, tn=OUT_COLS, single_buffer_a=False, vmem_mib=62, w_tiled=True, **kw)

    def ffn(x, li, which, sub):
        hidden, w_out = ffn_in(x, li, which, sub, n_rows=t, out_row0=0, out_rows=t)
        return ffn_out(hidden, w_out, x, li, sub, n_rows=t, a_row0=0, x_row0=0, out_row0=0, out_rows=t)

    na_k, na_v, win_k, win_v = [], [], [], []
    x = None
    for li in range(depth):
        if li == 0:
            hidden, w_out = ffn_in(xs2, 0, 0, 0, n_rows=ts, out_row0=0, out_rows=t)
            hidden = ffn_in(xp2, 0, 0, 0, cast=False, n_rows=tp, out_row0=ts, out_rows=t, alias=hidden)
            x = ffn_out(hidden, w_out, xs2, 0, 0, n_rows=ts, a_row0=0, x_row0=0, out_row0=0, out_rows=t)
            x = ffn_out(hidden, w_out, xp2, 0, 0, n_rows=tp, a_row0=ts, x_row0=0, out_row0=ts, out_rows=t, alias=x)
        else:
            x = ffn(x, li, 0, 0)
        j = li // 2
        if li % 2 == 0:
            n_heads = cache_na_k.shape[3]
            d_na = n_heads * HEAD_DIM
            d_pool = ab_w_in.shape[2] - 3 * d_na
            tn = 512
            qk_t, pool_t = 2 * d_na // tn, d_pool // tn

            def col_perm(c):
                return jnp.where(c < qk_t, c + pool_t, jnp.where(c < qk_t + pool_t, c - qk_t, c))

            head_gain = jnp.concatenate(
                [jnp.tile(na_q_g[j], n_heads), jnp.tile(na_k_g[j], n_heads), jnp.ones((d_pool + d_na,), F32)])[None]
            q_col, k_col, u_col, v_col = 0, d_na, 2 * d_na, 2 * d_na + d_pool
            proj = _proj_in(x, ada(li, 1), ab_w_in[j], head_gain, tn=tn, norm_cols=2 * d_na, col_perm=col_perm)
            y_pool = _pool_mixer(proj, pool_w[j].astype(BF16), pool_scale[j][None], u_col=u_col,
                                 n_sample_blocks=n_dec, prompt_seq=seq)
            o = _na_attention(proj, cache_na_k[:, j].reshape(n_dec, -1, d_na),
                              cache_na_v[:, j].reshape(n_dec, -1, d_na), na_rel_bias[j], n_batch=n_dec, seq=dec_seq,
                              n_heads=n_heads, q_col=q_col, k_col=k_col, v_col=v_col)
            o, new_k, new_v = _dense_attention(proj, None, o, row0=ts, n_seq=n_prompt, seq=seq, q_col=q_col,
                                               k_col=k_col, v_col=v_col, n_kv=n_heads, groups=1)
            parts, w_out = [y_pool, o], ab_w_out[j].astype(BF16)
            na_k.append(new_k.reshape(n_prompt, seq, n_heads, HEAD_DIM))
            na_v.append(new_v.reshape(n_prompt, seq, n_heads, HEAD_DIM))
        else:
            n_kv = cache_win_k.shape[3]
            dkv = n_kv * HEAD_DIM
            groups = d // dkv
            head_gain = jnp.concatenate(
                [jnp.tile(win_q_g[j], n_kv * groups), jnp.tile(win_k_g[j], n_kv), jnp.ones((dkv,), F32)])[None]
            proj = _proj_in(x, ada(li, 1), win_w_in[j], head_gain,
                            tn=512 if (d + dkv) % 512 == 0 else 256, norm_cols=d + dkv)
            o = _win_attention(proj, cache_win_k[:, j].reshape(n_dec, -1, dkv),
                               cache_win_v[:, j].reshape(n_dec, -1, dkv), win_sink[j], n_batch=n_dec,
                               seq=dec_seq, n_kv=n_kv, groups=groups, k_col=d, v_col=d + dkv)
            o, new_k, new_v = _dense_attention(proj, win_sink[j], o, row0=ts, n_seq=n_prompt, seq=seq, q_col=0,
                                               k_col=d, v_col=d + dkv, n_kv=n_kv, groups=groups)
            parts, w_out = [o], win_w_out[j].astype(BF16)
            win_k.append(new_k.reshape(n_prompt, seq, n_kv, HEAD_DIM))
            win_v.append(new_v.reshape(n_prompt, seq, n_kv, HEAD_DIM))
        w_out = w_out.reshape(d, d // 512, 512).transpose(1, 0, 2)
        x = _matmul_resid(parts, w_out, x, mod[li], gate_row=5, mult=1.0, tm=ROW_TILE, tn=512, single_buffer_a=False,
                          vmem_mib=56, w_tiled=True, n_rows=t, a_row0=0, x_row0=0, out_row0=0, out_rows=t)
        if li < depth - 1:
            x = ffn(x, li, 1, 2)

    li = depth - 1
    hidden, w_out = ffn_in(x, li, 1, 2, n_rows=t, out_row0=0, out_rows=t)
    ys = ffn_out(hidden, w_out, x, li, 2, n_rows=ts, a_row0=0, x_row0=0, out_row0=0, out_rows=ts)
    yp = ffn_out(hidden, w_out, x, li, 2, n_rows=tp, a_row0=ts, x_row0=ts, out_row0=0, out_rows=tp)
    return (yp.reshape(n_prompt, seq, d), ys.reshape(n_dec, dec_seq, d), jnp.stack(na_k, axis=1),
            jnp.stack(na_v, axis=1), jnp.stack(win_k, axis=1), jnp.stack(win_v, axis=1))
```

```python
import functools

import jax
import jax.numpy as jnp
from jax import lax
from jax.experimental import pallas as pl
from jax.experimental.pallas import tpu as pltpu

EPS = 1e-6
NEG = -1e30
LOG2E = 1.4426950408889634
HEAD_DIM = 128
LANES = 128
GRID_W = 64
POOL_WINDOWS = (2, 4, 8, 16)
NA_ROWS = 8
NA_COLS = 16
C_WINDOW = 128
C_BLOCK = 128
ROPE_BASE = 10000.0
N_SUB = 3
N_MOD = 3 * N_SUB

SEQ_BLOCK = 2048
ROW_TILE = 1024
IN_TILE = 2048
PRO_ROWS = 256
CAST_ROWS = 64
STAT_ROWS = 128
NORM_ROWS = 16
NORM_COLS = 1024
NA_QROWS = 4
NA_KROWS = 12
MOD_ROWS = 16
MIB = 1024 * 1024
BF16 = jnp.bfloat16
F32 = jnp.float32


def _cparams(sem, vmem_mib):
    return pltpu.CompilerParams(dimension_semantics=sem, vmem_limit_bytes=vmem_mib * MIB)


def _alias_args(alias, n_inputs):
    if alias is None:
        return [], [], {}
    return [alias], [pl.BlockSpec(memory_space=pl.ANY)], {n_inputs: 0}


def _mod_kernel(c_ref, w_ref, b_ref, o_ref):
    c = c_ref[...]
    a = (c * jax.nn.sigmoid(c)).astype(BF16)
    w = w_ref[...].astype(BF16)
    o_ref[...] = jnp.dot(a, w, preferred_element_type=F32) + b_ref[...]


def _modulation(c_all, mod_w, mod_b, tn=512):
    depth, d, n = mod_w.shape
    return pl.pallas_call(
        _mod_kernel,
        grid=(depth, n // tn),
        in_specs=[
            pl.BlockSpec((MOD_ROWS, d), lambda l, j: (0, 0)),
            pl.BlockSpec((None, d, tn), lambda l, j: (l, 0, j)),
            pl.BlockSpec((None, 1, tn), lambda l, j: (l, 0, j)),
        ],
        out_specs=pl.BlockSpec((None, MOD_ROWS, tn), lambda l, j: (l, 0, j)),
        out_shape=jax.ShapeDtypeStruct((depth, MOD_ROWS, n), F32),
        compiler_params=_cparams(("arbitrary", "arbitrary"), 40),
        name="modulation",
    )(c_all, mod_w, mod_b.reshape(depth, 1, n))


def _adaln_rows(x_ref, ada_ref, h_ref, r_ref, row0):
    rows, d = x_ref.shape
    inv_d = 1.0 / d

    def stats(c, carry):
        r = pl.multiple_of(c * STAT_ROWS, STAT_ROWS)
        x = x_ref[pl.ds(r, STAT_ROWS), :]
        ms = jnp.sum(x * x, axis=-1, keepdims=True) * inv_d
        r_ref[pl.ds(r, STAT_ROWS), :] = jnp.broadcast_to(lax.rsqrt(ms + EPS), (STAT_ROWS, LANES))
        return carry

    def scale(c, carry):
        r = pl.multiple_of(c * NORM_ROWS, NORM_ROWS)
        rinv = jnp.concatenate([r_ref[pl.ds(r, NORM_ROWS), :]] * (NORM_COLS // LANES), axis=1)
        for c0 in range(0, d, NORM_COLS):
            cs = slice(c0, c0 + NORM_COLS)
            gain = jnp.concatenate([ada_ref[0, 0, :, cs]] * (NORM_ROWS // 8), axis=0)
            shift = jnp.concatenate([ada_ref[0, 1, :, cs]] * (NORM_ROWS // 8), axis=0)
            x = x_ref[pl.ds(r, NORM_ROWS), cs]
            h_ref[pl.ds(pl.multiple_of(row0 + r, NORM_ROWS), NORM_ROWS), cs] = (
                (x * rinv) * gain + shift).astype(h_ref.dtype)
        return carry

    lax.fori_loop(0, rows // STAT_ROWS, stats, 0)
    lax.fori_loop(0, rows // NORM_ROWS, scale, 0, unroll=2)


def _as_bf16(w):
    return w if w.dtype == BF16 else w.astype(BF16)


def _row_halves(ref):
    half = ref.shape[0] // 2
    return (slice(0, half), slice(half, 2 * half))


def _adaln_specs(d, tile0):
    n_pro = IN_TILE // PRO_ROWS
    per = SEQ_BLOCK // IN_TILE
    return n_pro, [
        pl.BlockSpec((PRO_ROWS, d), lambda i, j: (i * n_pro + jnp.minimum(j, n_pro - 1), 0)),
        pl.BlockSpec((1, 2, 8, d), lambda i, j: ((i + tile0) // per, 0, 0, 0)),
    ]


def _swiglu_kernel(x_ref, ada_ref, wg_ref, wu_ref, *rest, n_pro, has_cast):
    h_ref, r_ref = rest[-2:]
    o_ref = rest[-4] if has_cast else rest[-3]
    j = pl.program_id(1)

    if has_cast:
        rest[-3][...] = rest[0][...].astype(BF16)

    @pl.when(j < n_pro)
    def _():
        _adaln_rows(x_ref, ada_ref, h_ref, r_ref, j * PRO_ROWS)

    @pl.when(j >= n_pro)
    def _():
        wg = _as_bf16(wg_ref[...])
        wu = _as_bf16(wu_ref[...])
        for rows in _row_halves(h_ref):
            h = h_ref[rows, :]
            gate = jnp.dot(h, wg, preferred_element_type=F32)
            up = jnp.dot(h, wu, preferred_element_type=F32)
            o_ref[rows, :] = ((gate * jax.nn.sigmoid(gate)) * up).astype(o_ref.dtype)


def _ffn_in(x, ada, w_in, w_sel, *, n_rows, out_row0, out_rows, w_out=None, alias=None, tn=256):
    tm = IN_TILE
    d = x.shape[1]
    dff = w_in.shape[1] // 2
    nj = dff // tn
    tile0 = out_row0 // tm
    n_pro, specs = _adaln_specs(d, tile0)
    steps = n_pro + nj
    grid = (n_rows // tm, steps)

    def wcol(j):
        return jnp.maximum(j - n_pro, 0)

    in_specs = specs + [
        pl.BlockSpec((d, tn), lambda i, j: (w_sel, wcol(j))),
        pl.BlockSpec((d, tn), lambda i, j: (w_sel, wcol(j) + nj)),
    ]
    args = [x, ada, w_in, w_in]
    out_specs = [pl.BlockSpec((tm, tn), lambda i, j: (i + tile0, wcol(j)))]
    out_shape = [jax.ShapeDtypeStruct((out_rows, dff), BF16)]
    if w_out is not None:
        cast_rows = CAST_ROWS
        while dff // cast_rows > grid[0] * steps:
            cast_rows *= 2
        n_blk = dff // cast_rows

        def cast_blk(i, j):
            return jnp.minimum(i * steps + j, n_blk - 1)

        in_specs.append(pl.BlockSpec((cast_rows, d), lambda i, j: (w_sel * n_blk + cast_blk(i, j), 0)))
        args.append(w_out)
        out_specs.append(pl.BlockSpec((cast_rows, d), lambda i, j: (cast_blk(i, j), 0)))
        out_shape.append(jax.ShapeDtypeStruct((dff, d), BF16))
    extra, extra_specs, aliases = _alias_args(alias, len(args))
    outs = pl.pallas_call(
        functools.partial(_swiglu_kernel, n_pro=n_pro, has_cast=w_out is not None),
        grid=grid,
        in_specs=in_specs + extra_specs,
        out_specs=out_specs,
        out_shape=out_shape,
        scratch_shapes=[pltpu.VMEM((tm, d), BF16), pltpu.VMEM((PRO_ROWS, LANES), F32)],
        input_output_aliases=aliases,
        compiler_params=_cparams(("arbitrary", "arbitrary"), 56),
        name="ffn_in",
    )(*args, *extra)
    return outs if w_out is not None else outs[0]


def _headnorm_kernel(x_ref, ada_ref, w_ref, hg_ref, o_ref, h_ref, r_ref, *, n_pro, norm_tiles):
    j = pl.program_id(1)

    @pl.when(j < n_pro)
    def _():
        _adaln_rows(x_ref, ada_ref, h_ref, r_ref, j * PRO_ROWS)

    @pl.when((j >= n_pro) & (j < n_pro + norm_tiles))
    def _():
        w = _as_bf16(w_ref[...])
        for rows in _row_halves(h_ref):
            y = jnp.dot(h_ref[rows, :], w, preferred_element_type=F32)
            for c in range(0, y.shape[1], HEAD_DIM):
                sl = slice(c, c + HEAD_DIM)
                yc = y[:, sl]
                ms = jnp.mean(yc * yc, axis=-1, keepdims=True)
                o_ref[rows, sl] = (yc * lax.rsqrt(ms + EPS)) * hg_ref[:, sl]

    @pl.when(j >= n_pro + norm_tiles)
    def _():
        w = _as_bf16(w_ref[...])
        for rows in _row_halves(h_ref):
            o_ref[rows, :] = jnp.dot(h_ref[rows, :], w, preferred_element_type=F32)


def _proj_in(x, ada, w, head_gain, *, tn, norm_cols, col_perm=None):
    tm = IN_TILE
    t, d = x.shape
    n = w.shape[1]
    n_pro, specs = _adaln_specs(d, 0)
    if col_perm is None:
        col_perm = lambda c: c

    def wcol(j):
        return jnp.maximum(j - n_pro, 0)

    return pl.pallas_call(
        functools.partial(_headnorm_kernel, n_pro=n_pro, norm_tiles=norm_cols // tn),
        grid=(t // tm, n_pro + n // tn),
        in_specs=specs + [
            pl.BlockSpec((d, tn), lambda i, j: (0, col_perm(wcol(j)))),
            pl.BlockSpec((1, tn), lambda i, j: (0, wcol(j))),
        ],
        out_specs=pl.BlockSpec((tm, tn), lambda i, j: (i, wcol(j))),
        out_shape=jax.ShapeDtypeStruct((t, n), F32),
        scratch_shapes=[pltpu.VMEM((tm, d), BF16), pltpu.VMEM((PRO_ROWS, LANES), F32)],
        compiler_params=_cparams(("arbitrary", "arbitrary"), 56),
        name="proj_in",
    )(x, ada, w, head_gain)


def _resid_kernel(*refs, n_parts, gate_row, mult):
    a_refs = refs[:n_parts]
    w_refs = refs[n_parts:2 * n_parts]
    x_ref, mod_ref = refs[2 * n_parts:2 * n_parts + 2]
    o_ref = refs[-1]
    gate = mod_ref[0, gate_row:gate_row + 1, :]
    if mult != 1.0:
        gate = mult * gate
    ws = [_as_bf16(w_ref[...]) for w_ref in w_refs]
    for rows in _row_halves(o_ref):
        acc = jnp.dot(a_refs[0][rows, :], ws[0], preferred_element_type=F32)
        for a_ref, w in zip(a_refs[1:], ws[1:]):
            acc = acc + jnp.dot(a_ref[rows, :], w, preferred_element_type=F32)
        o_ref[rows, :] = x_ref[rows, :] + gate * acc


def _matmul_resid(parts, w, x, mod, *, gate_row, mult, tm, tn, single_buffer_a, n_rows, a_row0, x_row0,
                  out_row0, out_rows, vmem_mib, w_sel=0, alias=None):
    n_tiles, a_tile0, x_tile0, out_tile0 = n_rows // tm, a_row0 // tm, x_row0 // tm, out_row0 // tm
    d = w.shape[1]
    per = SEQ_BLOCK // tm
    a_mode = dict(pipeline_mode=pl.Buffered(1)) if single_buffer_a else {}
    k_total = sum(a.shape[1] for a in parts)
    in_specs, w_specs, off = [], [], w_sel * k_total
    for a in parts:
        k = a.shape[1]
        assert off % k == 0
        in_specs.append(pl.BlockSpec((tm, k), lambda i, j: (i + a_tile0, 0), **a_mode))
        w_specs.append(pl.BlockSpec((k, tn), functools.partial(lambda i, j, kb: (kb, j), kb=off // k)))
        off += k
    in_specs += w_specs + [
        pl.BlockSpec((tm, tn), lambda i, j: (i + x_tile0, j)),
        pl.BlockSpec((1, N_MOD, tn), lambda i, j: ((i + a_tile0) // per, 0, j)),
    ]
    extra, extra_specs, aliases = _alias_args(alias, len(in_specs))
    return pl.pallas_call(
        functools.partial(_resid_kernel, n_parts=len(parts), gate_row=gate_row, mult=mult),
        grid=(n_tiles, d // tn),
        in_specs=in_specs + extra_specs,
        out_specs=pl.BlockSpec((tm, tn), lambda i, j: (i + out_tile0, j)),
        out_shape=jax.ShapeDtypeStruct((out_rows, d), F32),
        input_output_aliases=aliases,
        compiler_params=_cparams(("arbitrary", "arbitrary"), vmem_mib),
        name="matmul_resid",
    )(*parts, *([w] * len(parts)), x, mod, *extra)


def _pool_kernel(u_ref, w_ref, s_ref, o_ref, *, n_sample_blocks, prompt_seq):
    i = pl.program_id(0)
    g = pl.program_id(1)
    rows = u_ref.shape[0]
    seq_len = jnp.where(i < n_sample_blocks, rows, prompt_seq)
    pos = lax.broadcasted_iota(jnp.int32, (rows, 1), 0) & (seq_len - 1)

    def shifted(x, k):
        src = pos - k
        return jnp.where((src >= 0) & (src < seq_len), pltpu.roll(x, k % rows, 0), 0.0)

    for gi, win in enumerate(POOL_WINDOWS):
        @pl.when(g == gi)
        def _(win=win):
            half = win // 2
            u = u_ref[...]
            ahead, behind, span = u, u, 1
            while span < half:
                ahead = ahead + shifted(ahead, -span)
                behind = behind + shifted(behind, span)
                span *= 2
            acc = ahead + shifted(behind, 1)
            cnt = (jnp.minimum(pos + half, seq_len) - jnp.maximum(pos - half, 0)).astype(F32)
            diff = (acc / cnt - u).astype(BF16)
            y = jnp.dot(diff, w_ref[...], preferred_element_type=F32)
            o_ref[...] = (y * s_ref[...]).astype(o_ref.dtype)


def _pool_mixer(proj, pool_w, pool_scale, *, u_col, n_sample_blocks, prompt_seq):
    t = proj.shape[0]
    n_groups, cg, _ = pool_w.shape
    g0 = u_col // cg
    return pl.pallas_call(
        functools.partial(_pool_kernel, n_sample_blocks=n_sample_blocks, prompt_seq=prompt_seq),
        grid=(t // SEQ_BLOCK, n_groups),
        in_specs=[
            pl.BlockSpec((SEQ_BLOCK, cg), lambda i, g: (i, g0 + g)),
            pl.BlockSpec((None, cg, cg), lambda i, g: (g, 0, 0)),
            pl.BlockSpec((1, cg), lambda i, g: (0, g)),
        ],
        out_specs=pl.BlockSpec((SEQ_BLOCK, cg), lambda i, g: (i, g)),
        out_shape=jax.ShapeDtypeStruct((t, n_groups * cg), BF16),
        compiler_params=_cparams(("arbitrary", "arbitrary"), 48),
        name="pool_mixer",
    )(proj, pool_w, pool_scale)


def _qkt(q, k):
    return lax.dot_general(q, k, (((1,), (1,)), ((), ())), preferred_element_type=F32)


def _with_ones(v):
    return jnp.concatenate([v.astype(BF16), jnp.ones(v.shape, BF16)], axis=1)


def _softmax_pv(s_list, v_list, sink=None):
    tiles = [s[:, c:c + LANES] for s in s_list for c in range(0, s.shape[1], LANES)]
    mt = tiles[0]
    for tile in tiles[1:]:
        mt = jnp.maximum(mt, tile)
    m = mt.max(axis=-1, keepdims=True)
    if sink is not None:
        m = jnp.maximum(m, sink)
    acc = None
    for s, v in zip(s_list, v_list):
        part = jnp.dot(jnp.exp2(s - m).astype(BF16), v, preferred_element_type=F32)
        acc = part if acc is None else acc + part
    o, l = acc[:, :HEAD_DIM], acc[:, HEAD_DIM:]
    if sink is not None:
        l = l + jnp.exp2(sink - m)
    return o / l


def _dense_attn_kernel(*refs, n_kv, groups, has_sink):
    if has_sink:
        sink_ref, q_ref, k_ref, v_ref = refs[:4]
    else:
        q_ref, k_ref, v_ref = refs[:3]
    o_ref, nk_ref, nv_ref = refs[-3:]
    seq = q_ref.shape[0]
    qscale = HEAD_DIM ** -0.5 * LOG2E
    nk_ref[...] = k_ref[...]
    nv_ref[...] = v_ref[...]
    for h in range(n_kv):
        ks = slice(h * HEAD_DIM, (h + 1) * HEAD_DIM)
        k = k_ref[:, ks].astype(BF16)
        v = _with_ones(v_ref[:, ks])
        qs = [q_ref[:, (h * groups + g) * HEAD_DIM:(h * groups + g + 1) * HEAD_DIM] for g in range(groups)]
        q = ((qs[0] if groups == 1 else jnp.concatenate(qs, axis=0)) * qscale).astype(BF16)
        s = _qkt(q, k)
        sink = None
        if has_sink:
            sink = jnp.concatenate(
                [jnp.full((seq, 1), sink_ref[h * groups + g] * LOG2E, F32) for g in range(groups)], axis=0)
        o = _softmax_pv([s], [v], sink)
        for g in range(groups):
            c0 = (h * groups + g) * HEAD_DIM
            o_ref[:, c0:c0 + HEAD_DIM] = o[g * seq:(g + 1) * seq].astype(o_ref.dtype)


def _dense_attention(proj, sink, o_buf, *, row0, n_seq, seq, q_col, k_col, v_col, n_kv, groups):
    dq = n_kv * groups * HEAD_DIM
    dkv = n_kv * HEAD_DIM
    rb = row0 // seq
    in_specs = [
        pl.BlockSpec((seq, dq), lambda b: (rb + b, q_col // dq)),
        pl.BlockSpec((seq, dkv), lambda b: (rb + b, k_col // dkv)),
        pl.BlockSpec((seq, dkv), lambda b: (rb + b, v_col // dkv)),
    ]
    args = [proj, proj, proj]
    if sink is not None:
        in_specs = [pl.BlockSpec(memory_space=pltpu.SMEM)] + in_specs
        args = [sink] + args
    extra, extra_specs, aliases = _alias_args(o_buf, len(args))
    kv_shape = jax.ShapeDtypeStruct((n_seq * seq, dkv), F32)
    return pl.pallas_call(
        functools.partial(_dense_attn_kernel, n_kv=n_kv, groups=groups, has_sink=sink is not None),
        grid=(n_seq,),
        in_specs=in_specs + extra_specs,
        out_specs=[
            pl.BlockSpec((seq, dq), lambda b: (rb + b, 0)),
            pl.BlockSpec((seq, dkv), lambda b: (b, 0)),
            pl.BlockSpec((seq, dkv), lambda b: (b, 0)),
        ],
        out_shape=[jax.ShapeDtypeStruct(o_buf.shape, o_buf.dtype), kv_shape, kv_shape],
        input_output_aliases=aliases,
        compiler_params=_cparams(("arbitrary",), 32),
        name="dense_attention",
    )(*args, *extra)


def _na_block_start(qb, rows):
    return min(max(qb * NA_QROWS - NA_ROWS // 2, 0), rows - NA_KROWS)


def _na_build_bias(rb_ref, bias_ref, rows):
    n_qb = rows // NA_QROWS
    qc = lax.broadcasted_iota(jnp.int32, (GRID_W, LANES), 0)
    lane = lax.broadcasted_iota(jnp.int32, (GRID_W, LANES), 1)
    kc = lane & (GRID_W - 1)
    ws = jnp.clip(qc - NA_COLS // 2, 0, GRID_W - NA_COLS)
    col_valid = (kc >= ws) & (kc < ws + NA_COLS)
    first_half = lane < GRID_W
    for kind, qb in enumerate((0, 1, n_qb - 1)):
        k_start = _na_block_start(qb, rows)
        for a in range(NA_QROWS):
            r = qb * NA_QROWS + a
            rs = min(max(r - NA_ROWS // 2, 0), rows - NA_ROWS)
            for pair in range(NA_KROWS // 2):
                vec = None
                valid = None
                for half in range(2):
                    kr = k_start + 2 * pair + half
                    if not rs <= kr < rs + NA_ROWS:
                        continue
                    drow = kr - r + NA_ROWS - 1
                    piece = pltpu.roll(rb_ref[0, drow:drow + 1, :], (half * GRID_W - (NA_COLS - 1)) % LANES, 1)
                    vec = piece if vec is None else vec + piece
                    hv = first_half if half == 0 else jnp.logical_not(first_half)
                    valid = hv if valid is None else jnp.logical_or(valid, hv)
                dst = (kind, slice(a * GRID_W, (a + 1) * GRID_W), slice(pair * LANES, (pair + 1) * LANES))
                if vec is None:
                    bias_ref[dst] = jnp.full((GRID_W, LANES), NEG, F32)
                else:
                    toeplitz = pltpu.roll(jnp.broadcast_to(vec, (GRID_W, LANES)), 0, 1, stride=1, stride_axis=0)
                    bias_ref[dst] = jnp.where(valid & col_valid, toeplitz * LOG2E, NEG)


def _na_kernel(q_ref, k_ref, v_ref, ck_ref, cv_ref, rb_ref, o_ref, kb_ref, vb_ref, bias_ref):
    rows = q_ref.shape[0] // GRID_W
    n_qb = rows // NA_QROWS
    qn = NA_QROWS * GRID_W
    kn = NA_KROWS * GRID_W
    qscale = HEAD_DIM ** -0.5 * LOG2E

    @pl.when(pl.program_id(1) == 0)
    def _():
        _na_build_bias(rb_ref, bias_ref, rows)

    kb_ref[...] = k_ref[...].astype(BF16)
    vb_ref[...] = _with_ones(v_ref[...])
    ck = ck_ref[0].astype(BF16)
    cv = _with_ones(cv_ref[0])
    for qb in range(n_qb):
        k0 = _na_block_start(qb, rows) * GRID_W
        kind = 0 if qb == 0 else (2 if qb == n_qb - 1 else 1)
        q = (q_ref[qb * qn:(qb + 1) * qn, :] * qscale).astype(BF16)
        s_loc = _qkt(q, kb_ref[k0:k0 + kn, :]) + bias_ref[kind]
        s_ctx = _qkt(q, ck)
        o = _softmax_pv([s_loc, s_ctx], [vb_ref[k0:k0 + kn, :], cv])
        o_ref[qb * qn:(qb + 1) * qn, :] = o.astype(o_ref.dtype)


def _na_attention(proj, ck, cv, rel_bias, *, n_batch, seq, n_heads, q_col, k_col, v_col):
    qn, kn = NA_QROWS * GRID_W, NA_KROWS * GRID_W
    qo, ko, vo = q_col // HEAD_DIM, k_col // HEAD_DIM, v_col // HEAD_DIM
    rb = jnp.zeros((n_heads, 2 * NA_ROWS, LANES), F32).at[:, :2 * NA_ROWS - 1, :2 * NA_COLS - 1].set(rel_bias)
    return pl.pallas_call(
        _na_kernel,
        grid=(n_heads, n_batch),
        in_specs=[
            pl.BlockSpec((seq, HEAD_DIM), lambda h, b: (b, qo + h)),
            pl.BlockSpec((seq, HEAD_DIM), lambda h, b: (b, ko + h)),
            pl.BlockSpec((seq, HEAD_DIM), lambda h, b: (b, vo + h)),
            pl.BlockSpec((1, ck.shape[1], HEAD_DIM), lambda h, b: (b, 0, h)),
            pl.BlockSpec((1, cv.shape[1], HEAD_DIM), lambda h, b: (b, 0, h)),
            pl.BlockSpec((1, 2 * NA_ROWS, LANES), lambda h, b: (h, 0, 0)),
        ],
        out_specs=pl.BlockSpec((seq, HEAD_DIM), lambda h, b: (b, h)),
        out_shape=jax.ShapeDtypeStruct((proj.shape[0], n_heads * HEAD_DIM), BF16),
        scratch_shapes=[pltpu.VMEM((seq, HEAD_DIM), BF16), pltpu.VMEM((seq, 2 * HEAD_DIM), BF16),
                        pltpu.VMEM((3, qn, kn), F32)],
        compiler_params=_cparams(("arbitrary", "arbitrary"), 32),
        name="na_attention",
    )(proj, proj, proj, ck, cv, rb)


def _rope_tables(seq):
    half = HEAD_DIM // 2
    quarter = half // 2
    t = jnp.arange(seq)
    freqs = ROPE_BASE ** (-jnp.arange(quarter, dtype=F32) * 2.0 / half)
    ang_r = (t // GRID_W).astype(F32)[:, None] * freqs[None]
    ang_c = (t % GRID_W).astype(F32)[:, None] * freqs[None]
    cos = jnp.concatenate([jnp.cos(ang_r)] * 2 + [jnp.cos(ang_c)] * 2, axis=-1)
    sin = jnp.concatenate([-jnp.sin(ang_r), jnp.sin(ang_r), -jnp.sin(ang_c), jnp.sin(ang_c)], axis=-1)
    return cos, sin


def _rope(x, cos, sin):
    quarter = HEAD_DIM // 4
    lane = lax.broadcasted_iota(jnp.int32, x.shape, 1)
    first = (lane & (2 * quarter - 1)) < quarter
    partner = jnp.where(first, pltpu.roll(x, HEAD_DIM - quarter, 1), pltpu.roll(x, quarter, 1))
    return x * cos + partner * sin


def _win_kernel(sink_ref, q_ref, k_ref, v_ref, ck_ref, cv_ref, rope_ref, o_ref, kb_ref, vb_ref, mask_ref,
                *, groups):
    hkv = pl.program_id(1)
    seq = q_ref.shape[0]
    span = C_BLOCK + 2 * C_WINDOW
    n_blocks = seq // C_BLOCK

    def key_start(i):
        return min(max(i * C_BLOCK - C_WINDOW, 0), seq - span)

    @pl.when((pl.program_id(0) == 0) & (hkv == 0))
    def _():
        qrow = lax.broadcasted_iota(jnp.int32, (groups * C_BLOCK, span), 0) & (C_BLOCK - 1)
        kcol = lax.broadcasted_iota(jnp.int32, (groups * C_BLOCK, span), 1)
        for kind, i in enumerate((0, 1, n_blocks - 1)):
            valid = jnp.abs(kcol - qrow + (key_start(i) - i * C_BLOCK)) <= C_WINDOW
            mask_ref[kind] = jnp.where(valid, 0.0, NEG)

    kb_ref[...] = _rope(k_ref[...], rope_ref[0], rope_ref[1]).astype(BF16)
    vb_ref[...] = _with_ones(v_ref[...])
    ck = ck_ref[0].astype(BF16)
    cv = _with_ones(cv_ref[0])
    sink = jnp.concatenate(
        [jnp.full((C_BLOCK, 1), sink_ref[hkv * groups + g] * LOG2E, F32) for g in range(groups)], axis=0)
    for i in range(n_blocks):
        q0 = i * C_BLOCK
        k0 = key_start(i)
        kind = 0 if i == 0 else (2 if i == n_blocks - 1 else 1)
        cos = rope_ref[2, q0:q0 + C_BLOCK, :]
        sin = rope_ref[3, q0:q0 + C_BLOCK, :]
        q = jnp.concatenate(
            [_rope(q_ref[q0:q0 + C_BLOCK, g * HEAD_DIM:(g + 1) * HEAD_DIM], cos, sin) for g in range(groups)],
            axis=0).astype(BF16)
        s_loc = _qkt(q, kb_ref[k0:k0 + span, :]) + mask_ref[kind]
        s_ctx = _qkt(q, ck)
        o = _softmax_pv([s_loc, s_ctx], [vb_ref[k0:k0 + span, :], cv], sink)
        for g in range(groups):
            o_ref[q0:q0 + C_BLOCK, g * HEAD_DIM:(g + 1) * HEAD_DIM] = (
                o[g * C_BLOCK:(g + 1) * C_BLOCK].astype(o_ref.dtype))


def _win_attention(proj, ck, cv, sink, *, n_batch, seq, n_kv, groups, k_col, v_col):
    cos, sin = _rope_tables(seq)
    qscale = HEAD_DIM ** -0.5 * LOG2E
    rope = jnp.stack([cos, sin, cos * qscale, sin * qscale])
    ko, vo = k_col // HEAD_DIM, v_col // HEAD_DIM
    gw = groups * HEAD_DIM
    span = C_BLOCK + 2 * C_WINDOW
    return pl.pallas_call(
        functools.partial(_win_kernel, groups=groups),
        grid=(n_batch, n_kv),
        in_specs=[
            pl.BlockSpec(memory_space=pltpu.SMEM),
            pl.BlockSpec((seq, gw), lambda b, h: (b, h)),
            pl.BlockSpec((seq, HEAD_DIM), lambda b, h: (b, ko + h)),
            pl.BlockSpec((seq, HEAD_DIM), lambda b, h: (b, vo + h)),
            pl.BlockSpec((1, ck.shape[1], HEAD_DIM), lambda b, h: (b, 0, h)),
            pl.BlockSpec((1, cv.shape[1], HEAD_DIM), lambda b, h: (b, 0, h)),
            pl.BlockSpec((4, seq, HEAD_DIM), lambda b, h: (0, 0, 0)),
        ],
        out_specs=pl.BlockSpec((seq, gw), lambda b, h: (b, h)),
        out_shape=jax.ShapeDtypeStruct((proj.shape[0], n_kv * gw), BF16),
        scratch_shapes=[pltpu.VMEM((seq, HEAD_DIM), BF16), pltpu.VMEM((seq, 2 * HEAD_DIM), BF16),
                        pltpu.VMEM((3, groups * C_BLOCK, span), F32)],
        compiler_params=_cparams(("arbitrary", "arbitrary"), 32),
        name="win_attention",
    )(sink, proj, proj, proj, ck, cv, rope)


def kernel(x_prompt, x_sample, cache_na_k, cache_na_v, cache_win_k, cache_win_v, c, c_ctx,
           norm_g, mod_w, mod_b, ffn_w_in, ffn_w_out, ab_w_in, pool_w, pool_scale,
           na_q_g, na_k_g, na_rel_bias, ab_w_out, win_w_in, win_q_g, win_k_g, win_sink, win_w_out):
    n_prompt, seq, d = x_prompt.shape
    n_dec, dec_seq, _ = x_sample.shape
    depth = norm_g.shape[0]
    assert dec_seq == SEQ_BLOCK and SEQ_BLOCK % seq == 0 and (n_prompt * seq) % SEQ_BLOCK == 0
    assert n_dec + 1 <= MOD_ROWS
    ts, tp = n_dec * dec_seq, n_prompt * seq
    t = ts + tp
    n_prompt_blocks = tp // SEQ_BLOCK

    c_all = jnp.concatenate([c, c_ctx[None], jnp.zeros((MOD_ROWS - n_dec - 1, d), F32)], axis=0)
    m_all = _modulation(c_all, mod_w, mod_b).reshape(depth, MOD_ROWS, N_MOD, d)
    mod = jnp.concatenate(
        [m_all[:, :n_dec], jnp.broadcast_to(m_all[:, n_dec:n_dec + 1], (depth, n_prompt_blocks, N_MOD, d))], axis=1)

    def ada(li, sub):
        m = mod[li]
        rows = jnp.stack([norm_g[li, sub][None] * (1.0 + m[:, 3 * sub + 1]), m[:, 3 * sub]], axis=1)
        return jnp.broadcast_to(rows[:, :, None, :], rows.shape[:2] + (8, d))

    n_ffn = ffn_w_in.shape[1]
    ffn_in_w = ffn_w_in.reshape(depth * n_ffn * d, -1)
    ffn_out_w = ffn_w_out.reshape(-1, d)
    xs2, xp2 = x_sample.reshape(ts, d), x_prompt.reshape(tp, d)

    def ffn_in(x_src, li, which, sub, cast=True, **kw):
        return _ffn_in(x_src, ada(li, sub), ffn_in_w, li * n_ffn + which, w_out=ffn_out_w if cast else None, **kw)

    def ffn_out(hidden, w_out, x_src, li, sub, **kw):
        return _matmul_resid([hidden], w_out, x_src, mod[li], gate_row=3 * sub + 2, mult=0.5,
                             tm=1024, tn=256, single_buffer_a=False, vmem_mib=62, **kw)

    def ffn(x, li, which, sub):
        hidden, w_out = ffn_in(x, li, which, sub, n_rows=t, out_row0=0, out_rows=t)
        return ffn_out(hidden, w_out, x, li, sub, n_rows=t, a_row0=0, x_row0=0, out_row0=0, out_rows=t)

    na_k, na_v, win_k, win_v = [], [], [], []
    x = None
    for li in range(depth):
        if li == 0:
            hidden, w_out = ffn_in(xs2, 0, 0, 0, n_rows=ts, out_row0=0, out_rows=t)
            hidden = ffn_in(xp2, 0, 0, 0, cast=False, n_rows=tp, out_row0=ts, out_rows=t, alias=hidden)
            x = ffn_out(hidden, w_out, xs2, 0, 0, n_rows=ts, a_row0=0, x_row0=0, out_row0=0, out_rows=t)
            x = ffn_out(hidden, w_out, xp2, 0, 0, n_rows=tp, a_row0=ts, x_row0=0, out_row0=ts, out_rows=t, alias=x)
        else:
            x = ffn(x, li, 0, 0)
        j = li // 2
        if li % 2 == 0:
            n_heads = cache_na_k.shape[3]
            d_na = n_heads * HEAD_DIM
            d_pool = ab_w_in.shape[2] - 3 * d_na
            tn = 512
            qk_t, pool_t = 2 * d_na // tn, d_pool // tn

            def col_perm(c):
                return jnp.where(c < qk_t, c + pool_t, jnp.where(c < qk_t + pool_t, c - qk_t, c))

            head_gain = jnp.concatenate(
                [jnp.tile(na_q_g[j], n_heads), jnp.tile(na_k_g[j], n_heads), jnp.ones((d_pool + d_na,), F32)])[None]
            q_col, k_col, u_col, v_col = 0, d_na, 2 * d_na, 2 * d_na + d_pool
            proj = _proj_in(x, ada(li, 1), ab_w_in[j], head_gain, tn=tn, norm_cols=2 * d_na, col_perm=col_perm)
            y_pool = _pool_mixer(proj, pool_w[j].astype(BF16), pool_scale[j][None], u_col=u_col,
                                 n_sample_blocks=n_dec, prompt_seq=seq)
            o = _na_attention(proj, cache_na_k[:, j].reshape(n_dec, -1, d_na),
                              cache_na_v[:, j].reshape(n_dec, -1, d_na), na_rel_bias[j], n_batch=n_dec, seq=dec_seq,
                              n_heads=n_heads, q_col=q_col, k_col=k_col, v_col=v_col)
            o, new_k, new_v = _dense_attention(proj, None, o, row0=ts, n_seq=n_prompt, seq=seq, q_col=q_col,
                                               k_col=k_col, v_col=v_col, n_kv=n_heads, groups=1)
            parts, w_out = [y_pool, o], ab_w_out[j].astype(BF16)
            na_k.append(new_k.reshape(n_prompt, seq, n_heads, HEAD_DIM))
            na_v.append(new_v.reshape(n_prompt, seq, n_heads, HEAD_DIM))
        else:
            n_kv = cache_win_k.shape[3]
            dkv = n_kv * HEAD_DIM
            groups = d // dkv
            head_gain = jnp.concatenate(
                [jnp.tile(win_q_g[j], n_kv * groups), jnp.tile(win_k_g[j], n_kv), jnp.ones((dkv,), F32)])[None]
            proj = _proj_in(x, ada(li, 1), win_w_in[j], head_gain,
                            tn=512 if (d + dkv) % 512 == 0 else 256, norm_cols=d + dkv)
            o = _win_attention(proj, cache_win_k[:, j].reshape(n_dec, -1, dkv),
                               cache_win_v[:, j].reshape(n_dec, -1, dkv), win_sink[j], n_batch=n_dec,
                               seq=dec_seq, n_kv=n_kv, groups=groups, k_col=d, v_col=d + dkv)
            o, new_k, new_v = _dense_attention(proj, win_sink[j], o, row0=ts, n_seq=n_prompt, seq=seq, q_col=0,
                                               k_col=d, v_col=d + dkv, n_kv=n_kv, groups=groups)
            parts, w_out = [o], win_w_out[j].astype(BF16)
            win_k.append(new_k.reshape(n_prompt, seq, n_kv, HEAD_DIM))
            win_v.append(new_v.reshape(n_prompt, seq, n_kv, HEAD_DIM))
        x = _matmul_resid(parts, w_out, x, mod[li], gate_row=5, mult=1.0, tm=ROW_TILE, tn=512, single_buffer_a=False,
                          vmem_mib=56, n_rows=t, a_row0=0, x_row0=0, out_row0=0, out_rows=t)
        if li < depth - 1:
            x = ffn(x, li, 1, 2)

    li = depth - 1
    hidden, w_out = ffn_in(x, li, 1, 2, n_rows=t, out_row0=0, out_rows=t)
    ys = ffn_out(hidden, w_out, x, li, 2, n_rows=ts, a_row0=0, x_row0=0, out_row0=0, out_rows=ts)
    yp = ffn_out(hidden, w_out, x, li, 2, n_rows=tp, a_row0=ts, x_row0=ts, out_row0=0, out_rows=tp)
    return (yp.reshape(n_prompt, seq, d), ys.reshape(n_dec, dec_seq, d), jnp.stack(na_k, axis=1),
            jnp.stack(na_v, axis=1), jnp.stack(win_k, axis=1), jnp.stack(win_v, axis=1))
```

```python
import functools

import jax
import jax.numpy as jnp
from jax import lax
from jax.experimental import pallas as pl
from jax.experimental.pallas import tpu as pltpu

EPS = 1e-6
NEG = -1e30
LOG2E = 1.4426950408889634
HEAD_DIM = 128
LANES = 128
GRID_W = 64
POOL_WINDOWS = (2, 4, 8, 16)
NA_ROWS = 8
NA_COLS = 16
C_WINDOW = 128
C_BLOCK = 128
ROPE_BASE = 10000.0
N_SUB = 3
N_MOD = 3 * N_SUB

SEQ_BLOCK = 2048
ROW_TILE = 1024
IN_TILE = 2048
PRO_ROWS = 256
CAST_ROWS = 64
STAT_ROWS = 128
NORM_ROWS = 16
NORM_COLS = 1024
NA_QROWS = 4
NA_KROWS = 12
MOD_ROWS = 16
MIB = 1024 * 1024
BF16 = jnp.bfloat16
F32 = jnp.float32


def _cparams(sem, vmem_mib):
    return pltpu.CompilerParams(dimension_semantics=sem, vmem_limit_bytes=vmem_mib * MIB)


def _alias_args(alias, n_inputs):
    if alias is None:
        return [], [], {}
    return [alias], [pl.BlockSpec(memory_space=pl.ANY)], {n_inputs: 0}


def _mod_kernel(c_ref, w_ref, b_ref, o_ref):
    c = c_ref[...]
    a = (c * jax.nn.sigmoid(c)).astype(BF16)
    w = w_ref[...].astype(BF16)
    o_ref[...] = jnp.dot(a, w, preferred_element_type=F32) + b_ref[...]


def _modulation(c_all, mod_w, mod_b, tn=512):
    depth, d, n = mod_w.shape
    return pl.pallas_call(
        _mod_kernel,
        grid=(depth, n // tn),
        in_specs=[
            pl.BlockSpec((MOD_ROWS, d), lambda l, j: (0, 0)),
            pl.BlockSpec((None, d, tn), lambda l, j: (l, 0, j)),
            pl.BlockSpec((None, 1, tn), lambda l, j: (l, 0, j)),
        ],
        out_specs=pl.BlockSpec((None, MOD_ROWS, tn), lambda l, j: (l, 0, j)),
        out_shape=jax.ShapeDtypeStruct((depth, MOD_ROWS, n), F32),
        compiler_params=_cparams(("arbitrary", "arbitrary"), 40),
        name="modulation",
    )(c_all, mod_w, mod_b.reshape(depth, 1, n))


def _adaln_rows(x_ref, ada_ref, h_ref, r_ref, row0):
    rows, d = x_ref.shape
    inv_d = 1.0 / d

    def stats(c, carry):
        r = pl.multiple_of(c * STAT_ROWS, STAT_ROWS)
        x = x_ref[pl.ds(r, STAT_ROWS), :]
        ms = jnp.sum(x * x, axis=-1, keepdims=True) * inv_d
        r_ref[pl.ds(r, STAT_ROWS), :] = jnp.broadcast_to(lax.rsqrt(ms + EPS), (STAT_ROWS, LANES))
        return carry

    def scale(c, carry):
        r = pl.multiple_of(c * NORM_ROWS, NORM_ROWS)
        rinv = jnp.concatenate([r_ref[pl.ds(r, NORM_ROWS), :]] * (NORM_COLS // LANES), axis=1)
        for c0 in range(0, d, NORM_COLS):
            cs = slice(c0, c0 + NORM_COLS)
            gain = jnp.concatenate([ada_ref[0, 0, :, cs]] * (NORM_ROWS // 8), axis=0)
            shift = jnp.concatenate([ada_ref[0, 1, :, cs]] * (NORM_ROWS // 8), axis=0)
            x = x_ref[pl.ds(r, NORM_ROWS), cs]
            h_ref[pl.ds(pl.multiple_of(row0 + r, NORM_ROWS), NORM_ROWS), cs] = (
                (x * rinv) * gain + shift).astype(h_ref.dtype)
        return carry

    lax.fori_loop(0, rows // STAT_ROWS, stats, 0)
    lax.fori_loop(0, rows // NORM_ROWS, scale, 0, unroll=2)


def _as_bf16(w):
    return w if w.dtype == BF16 else w.astype(BF16)


def _row_halves(ref):
    half = ref.shape[0] // 2
    return (slice(0, half), slice(half, 2 * half))


def _adaln_specs(d, tile0):
    n_pro = IN_TILE // PRO_ROWS
    per = SEQ_BLOCK // IN_TILE
    return n_pro, [
        pl.BlockSpec((PRO_ROWS, d), lambda i, j: (i * n_pro + jnp.minimum(j, n_pro - 1), 0)),
        pl.BlockSpec((1, 2, 8, d), lambda i, j: ((i + tile0) // per, 0, 0, 0)),
    ]


def _swiglu_kernel(x_ref, ada_ref, wg_ref, wu_ref, *rest, n_pro, has_cast):
    h_ref, r_ref = rest[-2:]
    o_ref = rest[-4] if has_cast else rest[-3]
    j = pl.program_id(1)

    if has_cast:
        rest[-3][...] = rest[0][...].astype(BF16)

    @pl.when(j < n_pro)
    def _():
        _adaln_rows(x_ref, ada_ref, h_ref, r_ref, j * PRO_ROWS)

    @pl.when(j >= n_pro)
    def _():
        wg = _as_bf16(wg_ref[...])
        wu = _as_bf16(wu_ref[...])
        for rows in _row_halves(h_ref):
            h = h_ref[rows, :]
            gate = jnp.dot(h, wg, preferred_element_type=F32)
            up = jnp.dot(h, wu, preferred_element_type=F32)
            o_ref[rows, :] = ((gate * jax.nn.sigmoid(gate)) * up).astype(o_ref.dtype)


def _ffn_in(x, ada, w_in, w_sel, *, n_rows, out_row0, out_rows, w_out=None, alias=None, tn=256):
    tm = IN_TILE
    d = x.shape[1]
    dff = w_in.shape[1] // 2
    nj = dff // tn
    tile0 = out_row0 // tm
    n_pro, specs = _adaln_specs(d, tile0)
    steps = n_pro + nj
    grid = (n_rows // tm, steps)

    def wcol(j):
        return jnp.maximum(j - n_pro, 0)

    in_specs = specs + [
        pl.BlockSpec((d, tn), lambda i, j: (w_sel, wcol(j))),
        pl.BlockSpec((d, tn), lambda i, j: (w_sel, wcol(j) + nj)),
    ]
    args = [x, ada, w_in, w_in]
    out_specs = [pl.BlockSpec((tm, tn), lambda i, j: (i + tile0, wcol(j)))]
    out_shape = [jax.ShapeDtypeStruct((out_rows, dff), BF16)]
    if w_out is not None:
        cast_rows = CAST_ROWS
        while dff // cast_rows > grid[0] * steps:
            cast_rows *= 2
        n_blk = dff // cast_rows

        def cast_blk(i, j):
            return jnp.minimum(i * steps + j, n_blk - 1)

        in_specs.append(pl.BlockSpec((cast_rows, d), lambda i, j: (w_sel * n_blk + cast_blk(i, j), 0)))
        args.append(w_out)
        out_specs.append(pl.BlockSpec((cast_rows, d), lambda i, j: (cast_blk(i, j), 0)))
        out_shape.append(jax.ShapeDtypeStruct((dff, d), BF16))
    extra, extra_specs, aliases = _alias_args(alias, len(args))
    outs = pl.pallas_call(
        functools.partial(_swiglu_kernel, n_pro=n_pro, has_cast=w_out is not None),
        grid=grid,
        in_specs=in_specs + extra_specs,
        out_specs=out_specs,
        out_shape=out_shape,
        scratch_shapes=[pltpu.VMEM((tm, d), BF16), pltpu.VMEM((PRO_ROWS, LANES), F32)],
        input_output_aliases=aliases,
        compiler_params=_cparams(("arbitrary", "arbitrary"), 56),
        name="ffn_in",
    )(*args, *extra)
    return outs if w_out is not None else outs[0]


def _headnorm_kernel(x_ref, ada_ref, w_ref, hg_ref, o_ref, h_ref, r_ref, *, n_pro, norm_tiles):
    j = pl.program_id(1)

    @pl.when(j < n_pro)
    def _():
        _adaln_rows(x_ref, ada_ref, h_ref, r_ref, j * PRO_ROWS)

    @pl.when((j >= n_pro) & (j < n_pro + norm_tiles))
    def _():
        w = _as_bf16(w_ref[...])
        tn = o_ref.shape[1]
        col = pl.multiple_of((j - n_pro) * tn, tn)
        for rows in _row_halves(h_ref):
            y = jnp.dot(h_ref[rows, :], w, preferred_element_type=F32)
            for c in range(0, tn, HEAD_DIM):
                sl = slice(c, c + HEAD_DIM)
                yc = y[:, sl]
                ms = jnp.mean(yc * yc, axis=-1, keepdims=True)
                gain = hg_ref[:, pl.ds(pl.multiple_of(col + c, HEAD_DIM), HEAD_DIM)]
                o_ref[rows, sl] = (yc * lax.rsqrt(ms + EPS)) * gain

    @pl.when(j >= n_pro + norm_tiles)
    def _():
        w = _as_bf16(w_ref[...])
        for rows in _row_halves(h_ref):
            o_ref[rows, :] = jnp.dot(h_ref[rows, :], w, preferred_element_type=F32)


def _proj_in(x, ada, w, head_gain, *, tn, norm_cols, col_perm=None):
    tm = IN_TILE
    t, d = x.shape
    n = w.shape[1]
    n_pro, specs = _adaln_specs(d, 0)
    if col_perm is None:
        col_perm = lambda c: c

    def wcol(j):
        return jnp.maximum(j - n_pro, 0)

    return pl.pallas_call(
        functools.partial(_headnorm_kernel, n_pro=n_pro, norm_tiles=norm_cols // tn),
        grid=(t // tm, n_pro + n // tn),
        in_specs=specs + [
            pl.BlockSpec((d, tn), lambda i, j: (0, col_perm(wcol(j)))),
            pl.BlockSpec((1, n), lambda i, j: (0, 0)),
        ],
        out_specs=pl.BlockSpec((tm, tn), lambda i, j: (i, wcol(j))),
        out_shape=jax.ShapeDtypeStruct((t, n), F32),
        scratch_shapes=[pltpu.VMEM((tm, d), BF16), pltpu.VMEM((PRO_ROWS, LANES), F32)],
        compiler_params=_cparams(("arbitrary", "arbitrary"), 56),
        name="proj_in",
    )(x, ada, w, head_gain)


def _resid_kernel(*refs, n_parts, gate_row, mult):
    a_refs = refs[:n_parts]
    w_refs = refs[n_parts:2 * n_parts]
    x_ref, mod_ref = refs[2 * n_parts:2 * n_parts + 2]
    o_ref = refs[-1]
    tn = o_ref.shape[1]
    col = pl.multiple_of(pl.program_id(1) * tn, tn)
    gate = mod_ref[0, gate_row:gate_row + 1, pl.ds(col, tn)]
    if mult != 1.0:
        gate = mult * gate
    ws = [_as_bf16(w_ref[...]) for w_ref in w_refs]
    for rows in _row_halves(o_ref):
        acc = jnp.dot(a_refs[0][rows, :], ws[0], preferred_element_type=F32)
        for a_ref, w in zip(a_refs[1:], ws[1:]):
            acc = acc + jnp.dot(a_ref[rows, :], w, preferred_element_type=F32)
        o_ref[rows, :] = x_ref[rows, :] + gate * acc


def _matmul_resid(parts, w, x, mod, *, gate_row, mult, tm, tn, single_buffer_a, n_rows, a_row0, x_row0,
                  out_row0, out_rows, vmem_mib, w_sel=0, alias=None):
    n_tiles, a_tile0, x_tile0, out_tile0 = n_rows // tm, a_row0 // tm, x_row0 // tm, out_row0 // tm
    d = w.shape[1]
    per = SEQ_BLOCK // tm
    a_mode = dict(pipeline_mode=pl.Buffered(1)) if single_buffer_a else {}
    k_total = sum(a.shape[1] for a in parts)
    in_specs, w_specs, off = [], [], w_sel * k_total
    for a in parts:
        k = a.shape[1]
        assert off % k == 0
        in_specs.append(pl.BlockSpec((tm, k), lambda i, j: (i + a_tile0, 0), **a_mode))
        w_specs.append(pl.BlockSpec((k, tn), functools.partial(lambda i, j, kb: (kb, j), kb=off // k)))
        off += k
    in_specs += w_specs + [
        pl.BlockSpec((tm, tn), lambda i, j: (i + x_tile0, j)),
        pl.BlockSpec((1, N_MOD, d), lambda i, j: ((i + a_tile0) // per, 0, 0)),
    ]
    extra, extra_specs, aliases = _alias_args(alias, len(in_specs))
    return pl.pallas_call(
        functools.partial(_resid_kernel, n_parts=len(parts), gate_row=gate_row, mult=mult),
        grid=(n_tiles, d // tn),
        in_specs=in_specs + extra_specs,
        out_specs=pl.BlockSpec((tm, tn), lambda i, j: (i + out_tile0, j)),
        out_shape=jax.ShapeDtypeStruct((out_rows, d), F32),
        input_output_aliases=aliases,
        compiler_params=_cparams(("arbitrary", "arbitrary"), vmem_mib),
        name="matmul_resid",
    )(*parts, *([w] * len(parts)), x, mod, *extra)


def _pool_kernel(u_ref, w_ref, s_ref, o_ref, *, n_sample_blocks, prompt_seq):
    i = pl.program_id(0)
    g = pl.program_id(1)
    rows = u_ref.shape[0]
    seq_len = jnp.where(i < n_sample_blocks, rows, prompt_seq)
    pos = lax.broadcasted_iota(jnp.int32, (rows, 1), 0) & (seq_len - 1)

    def shifted(x, k):
        src = pos - k
        return jnp.where((src >= 0) & (src < seq_len), pltpu.roll(x, k % rows, 0), 0.0)

    for gi, win in enumerate(POOL_WINDOWS):
        @pl.when(g == gi)
        def _(win=win):
            half = win // 2
            u = u_ref[...]
            ahead, behind, span = u, u, 1
            while span < half:
                ahead = ahead + shifted(ahead, -span)
                behind = behind + shifted(behind, span)
                span *= 2
            acc = ahead + shifted(behind, 1)
            cnt = (jnp.minimum(pos + half, seq_len) - jnp.maximum(pos - half, 0)).astype(F32)
            diff = (acc / cnt - u).astype(BF16)
            y = jnp.dot(diff, w_ref[...], preferred_element_type=F32)
            o_ref[...] = (y * s_ref[...]).astype(o_ref.dtype)


def _pool_mixer(proj, pool_w, pool_scale, *, u_col, n_sample_blocks, prompt_seq):
    t = proj.shape[0]
    n_groups, cg, _ = pool_w.shape
    g0 = u_col // cg
    return pl.pallas_call(
        functools.partial(_pool_kernel, n_sample_blocks=n_sample_blocks, prompt_seq=prompt_seq),
        grid=(t // SEQ_BLOCK, n_groups),
        in_specs=[
            pl.BlockSpec((SEQ_BLOCK, cg), lambda i, g: (i, g0 + g)),
            pl.BlockSpec((None, cg, cg), lambda i, g: (g, 0, 0)),
            pl.BlockSpec((1, cg), lambda i, g: (0, g)),
        ],
        out_specs=pl.BlockSpec((SEQ_BLOCK, cg), lambda i, g: (i, g)),
        out_shape=jax.ShapeDtypeStruct((t, n_groups * cg), BF16),
        compiler_params=_cparams(("arbitrary", "arbitrary"), 48),
        name="pool_mixer",
    )(proj, pool_w, pool_scale)


def _qkt(q, k):
    return lax.dot_general(q, k, (((1,), (1,)), ((), ())), preferred_element_type=F32)


def _with_ones(v):
    return jnp.concatenate([v.astype(BF16), jnp.ones(v.shape, BF16)], axis=1)


def _softmax_pv(s_list, v_list, sink=None):
    tiles = [s[:, c:c + LANES] for s in s_list for c in range(0, s.shape[1], LANES)]
    mt = tiles[0]
    for tile in tiles[1:]:
        mt = jnp.maximum(mt, tile)
    m = mt.max(axis=-1, keepdims=True)
    if sink is not None:
        m = jnp.maximum(m, sink)
    acc = None
    for s, v in zip(s_list, v_list):
        part = jnp.dot(jnp.exp2(s - m).astype(BF16), v, preferred_element_type=F32)
        acc = part if acc is None else acc + part
    o, l = acc[:, :HEAD_DIM], acc[:, HEAD_DIM:]
    if sink is not None:
        l = l + jnp.exp2(sink - m)
    return o / l


def _dense_attn_kernel(*refs, n_kv, groups, has_sink):
    if has_sink:
        sink_ref, q_ref, k_ref, v_ref = refs[:4]
    else:
        q_ref, k_ref, v_ref = refs[:3]
    o_ref, nk_ref, nv_ref = refs[-3:]
    seq = q_ref.shape[0]
    qscale = HEAD_DIM ** -0.5 * LOG2E
    nk_ref[...] = k_ref[...]
    nv_ref[...] = v_ref[...]
    for h in range(n_kv):
        ks = slice(h * HEAD_DIM, (h + 1) * HEAD_DIM)
        k = k_ref[:, ks].astype(BF16)
        v = _with_ones(v_ref[:, ks])
        qs = [q_ref[:, (h * groups + g) * HEAD_DIM:(h * groups + g + 1) * HEAD_DIM] for g in range(groups)]
        q = ((qs[0] if groups == 1 else jnp.concatenate(qs, axis=0)) * qscale).astype(BF16)
        s = _qkt(q, k)
        sink = None
        if has_sink:
            sink = jnp.concatenate(
                [jnp.full((seq, 1), sink_ref[h * groups + g] * LOG2E, F32) for g in range(groups)], axis=0)
        o = _softmax_pv([s], [v], sink)
        for g in range(groups):
            c0 = (h * groups + g) * HEAD_DIM
            o_ref[:, c0:c0 + HEAD_DIM] = o[g * seq:(g + 1) * seq].astype(o_ref.dtype)


def _dense_attention(proj, sink, o_buf, *, row0, n_seq, seq, q_col, k_col, v_col, n_kv, groups):
    dq = n_kv * groups * HEAD_DIM
    dkv = n_kv * HEAD_DIM
    rb = row0 // seq
    in_specs = [
        pl.BlockSpec((seq, dq), lambda b: (rb + b, q_col // dq)),
        pl.BlockSpec((seq, dkv), lambda b: (rb + b, k_col // dkv)),
        pl.BlockSpec((seq, dkv), lambda b: (rb + b, v_col // dkv)),
    ]
    args = [proj, proj, proj]
    if sink is not None:
        in_specs = [pl.BlockSpec(memory_space=pltpu.SMEM)] + in_specs
        args = [sink] + args
    extra, extra_specs, aliases = _alias_args(o_buf, len(args))
    kv_shape = jax.ShapeDtypeStruct((n_seq * seq, dkv), F32)
    return pl.pallas_call(
        functools.partial(_dense_attn_kernel, n_kv=n_kv, groups=groups, has_sink=sink is not None),
        grid=(n_seq,),
        in_specs=in_specs + extra_specs,
        out_specs=[
            pl.BlockSpec((seq, dq), lambda b: (rb + b, 0)),
            pl.BlockSpec((seq, dkv), lambda b: (b, 0)),
            pl.BlockSpec((seq, dkv), lambda b: (b, 0)),
        ],
        out_shape=[jax.ShapeDtypeStruct(o_buf.shape, o_buf.dtype), kv_shape, kv_shape],
        input_output_aliases=aliases,
        compiler_params=_cparams(("arbitrary",), 32),
        name="dense_attention",
    )(*args, *extra)


def _na_block_start(qb, rows):
    return min(max(qb * NA_QROWS - NA_ROWS // 2, 0), rows - NA_KROWS)


def _na_build_bias(rb_ref, bias_ref, rows):
    n_qb = rows // NA_QROWS
    qc = lax.broadcasted_iota(jnp.int32, (GRID_W, LANES), 0)
    lane = lax.broadcasted_iota(jnp.int32, (GRID_W, LANES), 1)
    kc = lane & (GRID_W - 1)
    ws = jnp.clip(qc - NA_COLS // 2, 0, GRID_W - NA_COLS)
    col_valid = (kc >= ws) & (kc < ws + NA_COLS)
    first_half = lane < GRID_W
    for kind, qb in enumerate((0, 1, n_qb - 1)):
        k_start = _na_block_start(qb, rows)
        for a in range(NA_QROWS):
            r = qb * NA_QROWS + a
            rs = min(max(r - NA_ROWS // 2, 0), rows - NA_ROWS)
            for pair in range(NA_KROWS // 2):
                vec = None
                valid = None
                for half in range(2):
                    kr = k_start + 2 * pair + half
                    if not rs <= kr < rs + NA_ROWS:
                        continue
                    drow = kr - r + NA_ROWS - 1
                    piece = pltpu.roll(rb_ref[0, drow:drow + 1, :], (half * GRID_W - (NA_COLS - 1)) % LANES, 1)
                    vec = piece if vec is None else vec + piece
                    hv = first_half if half == 0 else jnp.logical_not(first_half)
                    valid = hv if valid is None else jnp.logical_or(valid, hv)
                dst = (kind, slice(a * GRID_W, (a + 1) * GRID_W), slice(pair * LANES, (pair + 1) * LANES))
                if vec is None:
                    bias_ref[dst] = jnp.full((GRID_W, LANES), NEG, F32)
                else:
                    toeplitz = pltpu.roll(jnp.broadcast_to(vec, (GRID_W, LANES)), 0, 1, stride=1, stride_axis=0)
                    bias_ref[dst] = jnp.where(valid & col_valid, toeplitz * LOG2E, NEG)


def _na_kernel(q_ref, k_ref, v_ref, ck_ref, cv_ref, rb_ref, o_ref, kb_ref, vb_ref, bias_ref):
    rows = q_ref.shape[0] // GRID_W
    n_qb = rows // NA_QROWS
    qn = NA_QROWS * GRID_W
    kn = NA_KROWS * GRID_W
    qscale = HEAD_DIM ** -0.5 * LOG2E

    @pl.when(pl.program_id(1) == 0)
    def _():
        _na_build_bias(rb_ref, bias_ref, rows)

    kb_ref[...] = k_ref[...].astype(BF16)
    vb_ref[...] = _with_ones(v_ref[...])
    ck = ck_ref[0].astype(BF16)
    cv = _with_ones(cv_ref[0])
    for qb in range(n_qb):
        k0 = _na_block_start(qb, rows) * GRID_W
        kind = 0 if qb == 0 else (2 if qb == n_qb - 1 else 1)
        q = (q_ref[qb * qn:(qb + 1) * qn, :] * qscale).astype(BF16)
        s_loc = _qkt(q, kb_ref[k0:k0 + kn, :]) + bias_ref[kind]
        s_ctx = _qkt(q, ck)
        o = _softmax_pv([s_loc, s_ctx], [vb_ref[k0:k0 + kn, :], cv])
        o_ref[qb * qn:(qb + 1) * qn, :] = o.astype(o_ref.dtype)


def _na_attention(proj, ck, cv, rel_bias, *, n_batch, seq, n_heads, q_col, k_col, v_col):
    qn, kn = NA_QROWS * GRID_W, NA_KROWS * GRID_W
    qo, ko, vo = q_col // HEAD_DIM, k_col // HEAD_DIM, v_col // HEAD_DIM
    rb = jnp.zeros((n_heads, 2 * NA_ROWS, LANES), F32).at[:, :2 * NA_ROWS - 1, :2 * NA_COLS - 1].set(rel_bias)
    return pl.pallas_call(
        _na_kernel,
        grid=(n_heads, n_batch),
        in_specs=[
            pl.BlockSpec((seq, HEAD_DIM), lambda h, b: (b, qo + h)),
            pl.BlockSpec((seq, HEAD_DIM), lambda h, b: (b, ko + h)),
            pl.BlockSpec((seq, HEAD_DIM), lambda h, b: (b, vo + h)),
            pl.BlockSpec((1, ck.shape[1], HEAD_DIM), lambda h, b: (b, 0, h)),
            pl.BlockSpec((1, cv.shape[1], HEAD_DIM), lambda h, b: (b, 0, h)),
            pl.BlockSpec((1, 2 * NA_ROWS, LANES), lambda h, b: (h, 0, 0)),
        ],
        out_specs=pl.BlockSpec((seq, HEAD_DIM), lambda h, b: (b, h)),
        out_shape=jax.ShapeDtypeStruct((proj.shape[0], n_heads * HEAD_DIM), BF16),
        scratch_shapes=[pltpu.VMEM((seq, HEAD_DIM), BF16), pltpu.VMEM((seq, 2 * HEAD_DIM), BF16),
                        pltpu.VMEM((3, qn, kn), F32)],
        compiler_params=_cparams(("arbitrary", "arbitrary"), 32),
        name="na_attention",
    )(proj, proj, proj, ck, cv, rb)


def _rope_tables(seq):
    half = HEAD_DIM // 2
    quarter = half // 2
    t = jnp.arange(seq)
    freqs = ROPE_BASE ** (-jnp.arange(quarter, dtype=F32) * 2.0 / half)
    ang_r = (t // GRID_W).astype(F32)[:, None] * freqs[None]
    ang_c = (t % GRID_W).astype(F32)[:, None] * freqs[None]
    cos = jnp.concatenate([jnp.cos(ang_r)] * 2 + [jnp.cos(ang_c)] * 2, axis=-1)
    sin = jnp.concatenate([-jnp.sin(ang_r), jnp.sin(ang_r), -jnp.sin(ang_c), jnp.sin(ang_c)], axis=-1)
    return cos, sin


def _rope(x, cos, sin):
    quarter = HEAD_DIM // 4
    lane = lax.broadcasted_iota(jnp.int32, x.shape, 1)
    first = (lane & (2 * quarter - 1)) < quarter
    partner = jnp.where(first, pltpu.roll(x, HEAD_DIM - quarter, 1), pltpu.roll(x, quarter, 1))
    return x * cos + partner * sin


def _win_kernel(sink_ref, q_ref, k_ref, v_ref, ck_ref, cv_ref, rope_ref, o_ref, kb_ref, vb_ref, mask_ref,
                *, groups):
    hkv = pl.program_id(1)
    seq = q_ref.shape[0]
    span = C_BLOCK + 2 * C_WINDOW
    n_blocks = seq // C_BLOCK

    def key_start(i):
        return min(max(i * C_BLOCK - C_WINDOW, 0), seq - span)

    @pl.when((pl.program_id(0) == 0) & (hkv == 0))
    def _():
        qrow = lax.broadcasted_iota(jnp.int32, (groups * C_BLOCK, span), 0) & (C_BLOCK - 1)
        kcol = lax.broadcasted_iota(jnp.int32, (groups * C_BLOCK, span), 1)
        for kind, i in enumerate((0, 1, n_blocks - 1)):
            valid = jnp.abs(kcol - qrow + (key_start(i) - i * C_BLOCK)) <= C_WINDOW
            mask_ref[kind] = jnp.where(valid, 0.0, NEG)

    kb_ref[...] = _rope(k_ref[...], rope_ref[0], rope_ref[1]).astype(BF16)
    vb_ref[...] = _with_ones(v_ref[...])
    ck = ck_ref[0].astype(BF16)
    cv = _with_ones(cv_ref[0])
    sink = jnp.concatenate(
        [jnp.full((C_BLOCK, 1), sink_ref[hkv * groups + g] * LOG2E, F32) for g in range(groups)], axis=0)
    for i in range(n_blocks):
        q0 = i * C_BLOCK
        k0 = key_start(i)
        kind = 0 if i == 0 else (2 if i == n_blocks - 1 else 1)
        cos = rope_ref[2, q0:q0 + C_BLOCK, :]
        sin = rope_ref[3, q0:q0 + C_BLOCK, :]
        q = jnp.concatenate(
            [_rope(q_ref[q0:q0 + C_BLOCK, g * HEAD_DIM:(g + 1) * HEAD_DIM], cos, sin) for g in range(groups)],
            axis=0).astype(BF16)
        s_loc = _qkt(q, kb_ref[k0:k0 + span, :]) + mask_ref[kind]
        s_ctx = _qkt(q, ck)
        o = _softmax_pv([s_loc, s_ctx], [vb_ref[k0:k0 + span, :], cv], sink)
        for g in range(groups):
            o_ref[q0:q0 + C_BLOCK, g * HEAD_DIM:(g + 1) * HEAD_DIM] = (
                o[g * C_BLOCK:(g + 1) * C_BLOCK].astype(o_ref.dtype))


def _win_attention(proj, ck, cv, sink, *, n_batch, seq, n_kv, groups, k_col, v_col):
    cos, sin = _rope_tables(seq)
    qscale = HEAD_DIM ** -0.5 * LOG2E
    rope = jnp.stack([cos, sin, cos * qscale, sin * qscale])
    ko, vo = k_col // HEAD_DIM, v_col // HEAD_DIM
    gw = groups * HEAD_DIM
    span = C_BLOCK + 2 * C_WINDOW
    return pl.pallas_call(
        functools.partial(_win_kernel, groups=groups),
        grid=(n_batch, n_kv),
        in_specs=[
            pl.BlockSpec(memory_space=pltpu.SMEM),
            pl.BlockSpec((seq, gw), lambda b, h: (b, h)),
            pl.BlockSpec((seq, HEAD_DIM), lambda b, h: (b, ko + h)),
            pl.BlockSpec((seq, HEAD_DIM), lambda b, h: (b, vo + h)),
            pl.BlockSpec((1, ck.shape[1], HEAD_DIM), lambda b, h: (b, 0, h)),
            pl.BlockSpec((1, cv.shape[1], HEAD_DIM), lambda b, h: (b, 0, h)),
            pl.BlockSpec((4, seq, HEAD_DIM), lambda b, h: (0, 0, 0)),
        ],
        out_specs=pl.BlockSpec((seq, gw), lambda b, h: (b, h)),
        out_shape=jax.ShapeDtypeStruct((proj.shape[0], n_kv * gw), BF16),
        scratch_shapes=[pltpu.VMEM((seq, HEAD_DIM), BF16), pltpu.VMEM((seq, 2 * HEAD_DIM), BF16),
                        pltpu.VMEM((3, groups * C_BLOCK, span), F32)],
        compiler_params=_cparams(("arbitrary", "arbitrary"), 32),
        name="win_attention",
    )(sink, proj, proj, proj, ck, cv, rope)


def kernel(x_prompt, x_sample, cache_na_k, cache_na_v, cache_win_k, cache_win_v, c, c_ctx,
           norm_g, mod_w, mod_b, ffn_w_in, ffn_w_out, ab_w_in, pool_w, pool_scale,
           na_q_g, na_k_g, na_rel_bias, ab_w_out, win_w_in, win_q_g, win_k_g, win_sink, win_w_out):
    n_prompt, seq, d = x_prompt.shape
    n_dec, dec_seq, _ = x_sample.shape
    depth = norm_g.shape[0]
    assert dec_seq == SEQ_BLOCK and SEQ_BLOCK % seq == 0 and (n_prompt * seq) % SEQ_BLOCK == 0
    assert n_dec + 1 <= MOD_ROWS
    ts, tp = n_dec * dec_seq, n_prompt * seq
    t = ts + tp
    n_prompt_blocks = tp // SEQ_BLOCK

    c_all = jnp.concatenate([c, c_ctx[None], jnp.zeros((MOD_ROWS - n_dec - 1, d), F32)], axis=0)
    m_all = _modulation(c_all, mod_w, mod_b).reshape(depth, MOD_ROWS, N_MOD, d)
    mod = jnp.concatenate(
        [m_all[:, :n_dec], jnp.broadcast_to(m_all[:, n_dec:n_dec + 1], (depth, n_prompt_blocks, N_MOD, d))], axis=1)

    def ada(li, sub):
        m = mod[li]
        rows = jnp.stack([norm_g[li, sub][None] * (1.0 + m[:, 3 * sub + 1]), m[:, 3 * sub]], axis=1)
        return jnp.broadcast_to(rows[:, :, None, :], rows.shape[:2] + (8, d))

    n_ffn = ffn_w_in.shape[1]
    ffn_in_w = ffn_w_in.reshape(depth * n_ffn * d, -1)
    ffn_out_w = ffn_w_out.reshape(-1, d)
    xs2, xp2 = x_sample.reshape(ts, d), x_prompt.reshape(tp, d)

    def ffn_in(x_src, li, which, sub, cast=True, **kw):
        return _ffn_in(x_src, ada(li, sub), ffn_in_w, li * n_ffn + which, w_out=ffn_out_w if cast else None, **kw)

    def ffn_out(hidden, w_out, x_src, li, sub, **kw):
        return _matmul_resid([hidden], w_out, x_src, mod[li], gate_row=3 * sub + 2, mult=0.5,
                             tm=1024, tn=256, single_buffer_a=False, vmem_mib=62, **kw)

    def ffn(x, li, which, sub):
        hidden, w_out = ffn_in(x, li, which, sub, n_rows=t, out_row0=0, out_rows=t)
        return ffn_out(hidden, w_out, x, li, sub, n_rows=t, a_row0=0, x_row0=0, out_row0=0, out_rows=t)

    na_k, na_v, win_k, win_v = [], [], [], []
    x = None
    for li in range(depth):
        if li == 0:
            hidden, w_out = ffn_in(xs2, 0, 0, 0, n_rows=ts, out_row0=0, out_rows=t)
            hidden = ffn_in(xp2, 0, 0, 0, cast=False, n_rows=tp, out_row0=ts, out_rows=t, alias=hidden)
            x = ffn_out(hidden, w_out, xs2, 0, 0, n_rows=ts, a_row0=0, x_row0=0, out_row0=0, out_rows=t)
            x = ffn_out(hidden, w_out, xp2, 0, 0, n_rows=tp, a_row0=ts, x_row0=0, out_row0=ts, out_rows=t, alias=x)
        else:
            x = ffn(x, li, 0, 0)
        j = li // 2
        if li % 2 == 0:
            n_heads = cache_na_k.shape[3]
            d_na = n_heads * HEAD_DIM
            d_pool = ab_w_in.shape[2] - 3 * d_na
            tn = 512
            qk_t, pool_t = 2 * d_na // tn, d_pool // tn

            def col_perm(c):
                return jnp.where(c < qk_t, c + pool_t, jnp.where(c < qk_t + pool_t, c - qk_t, c))

            head_gain = jnp.concatenate(
                [jnp.tile(na_q_g[j], n_heads), jnp.tile(na_k_g[j], n_heads), jnp.ones((d_pool + d_na,), F32)])[None]
            q_col, k_col, u_col, v_col = 0, d_na, 2 * d_na, 2 * d_na + d_pool
            proj = _proj_in(x, ada(li, 1), ab_w_in[j], head_gain, tn=tn, norm_cols=2 * d_na, col_perm=col_perm)
            y_pool = _pool_mixer(proj, pool_w[j].astype(BF16), pool_scale[j][None], u_col=u_col,
                                 n_sample_blocks=n_dec, prompt_seq=seq)
            o = _na_attention(proj, cache_na_k[:, j].reshape(n_dec, -1, d_na),
                              cache_na_v[:, j].reshape(n_dec, -1, d_na), na_rel_bias[j], n_batch=n_dec, seq=dec_seq,
                              n_heads=n_heads, q_col=q_col, k_col=k_col, v_col=v_col)
            o, new_k, new_v = _dense_attention(proj, None, o, row0=ts, n_seq=n_prompt, seq=seq, q_col=q_col,
                                               k_col=k_col, v_col=v_col, n_kv=n_heads, groups=1)
            parts, w_out = [y_pool, o], ab_w_out[j].astype(BF16)
            na_k.append(new_k.reshape(n_prompt, seq, n_heads, HEAD_DIM))
            na_v.append(new_v.reshape(n_prompt, seq, n_heads, HEAD_DIM))
        else:
            n_kv = cache_win_k.shape[3]
            dkv = n_kv * HEAD_DIM
            groups = d // dkv
            head_gain = jnp.concatenate(
                [jnp.tile(win_q_g[j], n_kv * groups), jnp.tile(win_k_g[j], n_kv), jnp.ones((dkv,), F32)])[None]
            proj = _proj_in(x, ada(li, 1), win_w_in[j], head_gain,
                            tn=512 if (d + dkv) % 512 == 0 else 256, norm_cols=d + dkv)
            o = _win_attention(proj, cache_win_k[:, j].reshape(n_dec, -1, dkv),
                               cache_win_v[:, j].reshape(n_dec, -1, dkv), win_sink[j], n_batch=n_dec,
                               seq=dec_seq, n_kv=n_kv, groups=groups, k_col=d, v_col=d + dkv)
            o, new_k, new_v = _dense_attention(proj, win_sink[j], o, row0=ts, n_seq=n_prompt, seq=seq, q_col=0,
                                               k_col=d, v_col=d + dkv, n_kv=n_kv, groups=groups)
            parts, w_out = [o], win_w_out[j].astype(BF16)
            win_k.append(new_k.reshape(n_prompt, seq, n_kv, HEAD_DIM))
            win_v.append(new_v.reshape(n_prompt, seq, n_kv, HEAD_DIM))
        x = _matmul_resid(parts, w_out, x, mod[li], gate_row=5, mult=1.0, tm=ROW_TILE, tn=512, single_buffer_a=False,
                          vmem_mib=56, n_rows=t, a_row0=0, x_row0=0, out_row0=0, out_rows=t)
        if li < depth - 1:
            x = ffn(x, li, 1, 2)

    li = depth - 1
    hidden, w_out = ffn_in(x, li, 1, 2, n_rows=t, out_row0=0, out_rows=t)
    ys = ffn_out(hidden, w_out, x, li, 2, n_rows=ts, a_row0=0, x_row0=0, out_row0=0, out_rows=ts)
    yp = ffn_out(hidden, w_out, x, li, 2, n_rows=tp, a_row0=ts, x_row0=ts, out_row0=0, out_rows=tp)
    return (yp.reshape(n_prompt, seq, d), ys.reshape(n_dec, dec_seq, d), jnp.stack(na_k, axis=1),
            jnp.stack(na_v, axis=1), jnp.stack(win_k, axis=1), jnp.stack(win_v, axis=1))
```

```python
import functools

import jax
import jax.numpy as jnp
from jax import lax
from jax.experimental import pallas as pl
from jax.experimental.pallas import tpu as pltpu

EPS = 1e-6
NEG = -1e30
LOG2E = 1.4426950408889634
HEAD_DIM = 128
LANES = 128
GRID_W = 64
POOL_WINDOWS = (2, 4, 8, 16)
NA_ROWS = 8
NA_COLS = 16
C_WINDOW = 128
C_BLOCK = 128
ROPE_BASE = 10000.0
N_SUB = 3
N_MOD = 3 * N_SUB

SEQ_BLOCK = 2048
ROW_TILE = 1024
IN_TILE = 2048
PRO_ROWS = 256
CAST_ROWS = 64
STAT_ROWS = 128
NORM_ROWS = 16
NORM_COLS = 1024
NA_QROWS = 4
NA_KROWS = 12
MOD_ROWS = 16
MOD_COLS = 512
FFN_COLS = 256
PROJ_COLS = 512
VMEM_MODULATION = 40
VMEM_ADALN_MATMUL = 56
VMEM_FFN_OUT = 62
VMEM_MIX_OUT = 56
VMEM_POOL = 48
VMEM_ATTENTION = 32
MIB = 1024 * 1024
BF16 = jnp.bfloat16
F32 = jnp.float32


def _cparams(sem, vmem_mib):
    return pltpu.CompilerParams(dimension_semantics=sem, vmem_limit_bytes=vmem_mib * MIB)


def _alias_args(alias, n_inputs):
    if alias is None:
        return [], [], {}
    return [alias], [pl.BlockSpec(memory_space=pl.ANY)], {n_inputs: 0}


def _mod_kernel(c_ref, w_ref, b_ref, o_ref):
    c = c_ref[...]
    a = (c * jax.nn.sigmoid(c)).astype(BF16)
    w = w_ref[...].astype(BF16)
    o_ref[...] = jnp.dot(a, w, preferred_element_type=F32) + b_ref[...]


def _modulation(c_all, mod_w, mod_b, tn=MOD_COLS):
    depth, d, n = mod_w.shape
    return pl.pallas_call(
        _mod_kernel,
        grid=(depth, n // tn),
        in_specs=[
            pl.BlockSpec((MOD_ROWS, d), lambda l, j: (0, 0)),
            pl.BlockSpec((None, d, tn), lambda l, j: (l, 0, j)),
            pl.BlockSpec((None, 1, tn), lambda l, j: (l, 0, j)),
        ],
        out_specs=pl.BlockSpec((None, MOD_ROWS, tn), lambda l, j: (l, 0, j)),
        out_shape=jax.ShapeDtypeStruct((depth, MOD_ROWS, n), F32),
        compiler_params=_cparams(("arbitrary", "arbitrary"), VMEM_MODULATION),
        name="modulation",
    )(c_all, mod_w, mod_b.reshape(depth, 1, n))


def _adaln_rows(x_ref, ada_ref, h_ref, r_ref, row0):
    rows, d = x_ref.shape
    inv_d = 1.0 / d

    def stats(c, carry):
        r = pl.multiple_of(c * STAT_ROWS, STAT_ROWS)
        x = x_ref[pl.ds(r, STAT_ROWS), :]
        ms = jnp.sum(x * x, axis=-1, keepdims=True) * inv_d
        r_ref[pl.ds(r, STAT_ROWS), :] = jnp.broadcast_to(lax.rsqrt(ms + EPS), (STAT_ROWS, LANES))
        return carry

    def scale(c, carry):
        r = pl.multiple_of(c * NORM_ROWS, NORM_ROWS)
        rinv = jnp.concatenate([r_ref[pl.ds(r, NORM_ROWS), :]] * (NORM_COLS // LANES), axis=1)
        for c0 in range(0, d, NORM_COLS):
            cs = slice(c0, c0 + NORM_COLS)
            gain = jnp.concatenate([ada_ref[0, 0, :, cs]] * (NORM_ROWS // 8), axis=0)
            shift = jnp.concatenate([ada_ref[0, 1, :, cs]] * (NORM_ROWS // 8), axis=0)
            x = x_ref[pl.ds(r, NORM_ROWS), cs]
            h_ref[pl.ds(pl.multiple_of(row0 + r, NORM_ROWS), NORM_ROWS), cs] = (
                (x * rinv) * gain + shift).astype(h_ref.dtype)
        return carry

    lax.fori_loop(0, rows // STAT_ROWS, stats, 0)
    lax.fori_loop(0, rows // NORM_ROWS, scale, 0, unroll=2)


def _as_bf16(w):
    return w if w.dtype == BF16 else w.astype(BF16)


def _row_halves(ref):
    half = ref.shape[0] // 2
    return (slice(0, half), slice(half, 2 * half))


def _adaln_specs(d, tile0):
    n_pro = IN_TILE // PRO_ROWS
    per = SEQ_BLOCK // IN_TILE
    return n_pro, [
        pl.BlockSpec((PRO_ROWS, d), lambda i, j: (i * n_pro + jnp.minimum(j, n_pro - 1), 0)),
        pl.BlockSpec((1, 2, 8, d), lambda i, j: ((i + tile0) // per, 0, 0, 0)),
    ]


def _swiglu_kernel(x_ref, ada_ref, wg_ref, wu_ref, *rest, n_pro, has_cast):
    h_ref, r_ref = rest[-2:]
    o_ref = rest[-4] if has_cast else rest[-3]
    j = pl.program_id(1)

    if has_cast:
        rest[-3][...] = rest[0][...].astype(BF16)

    @pl.when(j < n_pro)
    def _():
        _adaln_rows(x_ref, ada_ref, h_ref, r_ref, j * PRO_ROWS)

    @pl.when(j >= n_pro)
    def _():
        wg = _as_bf16(wg_ref[...])
        wu = _as_bf16(wu_ref[...])
        for rows in _row_halves(h_ref):
            h = h_ref[rows, :]
            gate = jnp.dot(h, wg, preferred_element_type=F32)
            up = jnp.dot(h, wu, preferred_element_type=F32)
            o_ref[rows, :] = ((gate * jax.nn.sigmoid(gate)) * up).astype(o_ref.dtype)


def _ffn_in(x, ada, w_in, w_sel, *, n_rows, out_row0, out_rows, w_out=None, alias=None, tn=FFN_COLS):
    tm = IN_TILE
    d = x.shape[1]
    dff = w_in.shape[1] // 2
    nj = dff // tn
    tile0 = out_row0 // tm
    n_pro, specs = _adaln_specs(d, tile0)
    steps = n_pro + nj
    grid = (n_rows // tm, steps)

    def wcol(j):
        return jnp.maximum(j - n_pro, 0)

    in_specs = specs + [
        pl.BlockSpec((d, tn), lambda i, j: (w_sel, wcol(j))),
        pl.BlockSpec((d, tn), lambda i, j: (w_sel, wcol(j) + nj)),
    ]
    args = [x, ada, w_in, w_in]
    out_specs = [pl.BlockSpec((tm, tn), lambda i, j: (i + tile0, wcol(j)))]
    out_shape = [jax.ShapeDtypeStruct((out_rows, dff), BF16)]
    if w_out is not None:
        cast_rows = CAST_ROWS
        while dff // cast_rows > grid[0] * steps:
            cast_rows *= 2
        n_blk = dff // cast_rows

        def cast_blk(i, j):
            return jnp.minimum(i * steps + j, n_blk - 1)

        in_specs.append(pl.BlockSpec((cast_rows, d), lambda i, j: (w_sel * n_blk + cast_blk(i, j), 0)))
        args.append(w_out)
        out_specs.append(pl.BlockSpec((cast_rows, d), lambda i, j: (cast_blk(i, j), 0)))
        out_shape.append(jax.ShapeDtypeStruct((dff, d), BF16))
    extra, extra_specs, aliases = _alias_args(alias, len(args))
    outs = pl.pallas_call(
        functools.partial(_swiglu_kernel, n_pro=n_pro, has_cast=w_out is not None),
        grid=grid,
        in_specs=in_specs + extra_specs,
        out_specs=out_specs,
        out_shape=out_shape,
        scratch_shapes=[pltpu.VMEM((tm, d), BF16), pltpu.VMEM((PRO_ROWS, LANES), F32)],
        input_output_aliases=aliases,
        compiler_params=_cparams(("arbitrary", "arbitrary"), VMEM_ADALN_MATMUL),
        name="ffn_in",
    )(*args, *extra)
    return outs if w_out is not None else outs[0]


def _headnorm_kernel(x_ref, ada_ref, w_ref, hg_ref, o_ref, h_ref, r_ref, *, n_pro, norm_tiles):
    j = pl.program_id(1)

    @pl.when(j < n_pro)
    def _():
        _adaln_rows(x_ref, ada_ref, h_ref, r_ref, j * PRO_ROWS)

    @pl.when((j >= n_pro) & (j < n_pro + norm_tiles))
    def _():
        w = _as_bf16(w_ref[...])
        tn = o_ref.shape[1]
        col = pl.multiple_of((j - n_pro) * tn, tn)
        for rows in _row_halves(h_ref):
            y = jnp.dot(h_ref[rows, :], w, preferred_element_type=F32)
            for c in range(0, tn, HEAD_DIM):
                sl = slice(c, c + HEAD_DIM)
                yc = y[:, sl]
                ms = jnp.mean(yc * yc, axis=-1, keepdims=True)
                gain = hg_ref[:, pl.ds(pl.multiple_of(col + c, HEAD_DIM), HEAD_DIM)]
                o_ref[rows, sl] = (yc * lax.rsqrt(ms + EPS)) * gain

    @pl.when(j >= n_pro + norm_tiles)
    def _():
        w = _as_bf16(w_ref[...])
        for rows in _row_halves(h_ref):
            o_ref[rows, :] = jnp.dot(h_ref[rows, :], w, preferred_element_type=F32)


def _proj_in(x, ada, w, head_gain, *, tn, norm_cols, col_perm=None):
    tm = IN_TILE
    t, d = x.shape
    n = w.shape[1]
    n_pro, specs = _adaln_specs(d, 0)
    if col_perm is None:
        col_perm = lambda c: c

    def wcol(j):
        return jnp.maximum(j - n_pro, 0)

    return pl.pallas_call(
        functools.partial(_headnorm_kernel, n_pro=n_pro, norm_tiles=norm_cols // tn),
        grid=(t // tm, n_pro + n // tn),
        in_specs=specs + [
            pl.BlockSpec((d, tn), lambda i, j: (0, col_perm(wcol(j)))),
            pl.BlockSpec((1, n), lambda i, j: (0, 0)),
        ],
        out_specs=pl.BlockSpec((tm, tn), lambda i, j: (i, wcol(j))),
        out_shape=jax.ShapeDtypeStruct((t, n), F32),
        scratch_shapes=[pltpu.VMEM((tm, d), BF16), pltpu.VMEM((PRO_ROWS, LANES), F32)],
        compiler_params=_cparams(("arbitrary", "arbitrary"), VMEM_ADALN_MATMUL),
        name="proj_in",
    )(x, ada, w, head_gain)


def _resid_kernel(*refs, n_parts, gate_row, mult):
    a_refs = refs[:n_parts]
    w_refs = refs[n_parts:2 * n_parts]
    x_ref, mod_ref = refs[2 * n_parts:2 * n_parts + 2]
    o_ref = refs[-1]
    tn = o_ref.shape[1]
    col = pl.multiple_of(pl.program_id(1) * tn, tn)
    gate = mod_ref[0, gate_row:gate_row + 1, pl.ds(col, tn)]
    if mult != 1.0:
        gate = mult * gate
    ws = [_as_bf16(w_ref[...]) for w_ref in w_refs]
    for rows in _row_halves(o_ref):
        acc = jnp.dot(a_refs[0][rows, :], ws[0], preferred_element_type=F32)
        for a_ref, w in zip(a_refs[1:], ws[1:]):
            acc = acc + jnp.dot(a_ref[rows, :], w, preferred_element_type=F32)
        o_ref[rows, :] = x_ref[rows, :] + gate * acc


def _matmul_resid(parts, w, x, mod, *, gate_row, mult, tm, tn, single_buffer_a, n_rows, a_row0, x_row0,
                  out_row0, out_rows, vmem_mib, w_sel=0, alias=None):
    n_tiles, a_tile0, x_tile0, out_tile0 = n_rows // tm, a_row0 // tm, x_row0 // tm, out_row0 // tm
    d = w.shape[1]
    per = SEQ_BLOCK // tm
    a_mode = dict(pipeline_mode=pl.Buffered(1)) if single_buffer_a else {}
    k_total = sum(a.shape[1] for a in parts)
    in_specs, w_specs, off = [], [], w_sel * k_total
    for a in parts:
        k = a.shape[1]
        assert off % k == 0
        in_specs.append(pl.BlockSpec((tm, k), lambda i, j: (i + a_tile0, 0), **a_mode))
        w_specs.append(pl.BlockSpec((k, tn), functools.partial(lambda i, j, kb: (kb, j), kb=off // k)))
        off += k
    in_specs += w_specs + [
        pl.BlockSpec((tm, tn), lambda i, j: (i + x_tile0, j)),
        pl.BlockSpec((1, N_MOD, d), lambda i, j: ((i + a_tile0) // per, 0, 0)),
    ]
    extra, extra_specs, aliases = _alias_args(alias, len(in_specs))
    return pl.pallas_call(
        functools.partial(_resid_kernel, n_parts=len(parts), gate_row=gate_row, mult=mult),
        grid=(n_tiles, d // tn),
        in_specs=in_specs + extra_specs,
        out_specs=pl.BlockSpec((tm, tn), lambda i, j: (i + out_tile0, j)),
        out_shape=jax.ShapeDtypeStruct((out_rows, d), F32),
        input_output_aliases=aliases,
        compiler_params=_cparams(("arbitrary", "arbitrary"), vmem_mib),
        name="matmul_resid",
    )(*parts, *([w] * len(parts)), x, mod, *extra)


def _pool_kernel(u_ref, w_ref, s_ref, o_ref, *, n_sample_blocks, prompt_seq):
    i = pl.program_id(0)
    g = pl.program_id(1)
    rows = u_ref.shape[0]
    seq_len = jnp.where(i < n_sample_blocks, rows, prompt_seq)
    pos = lax.broadcasted_iota(jnp.int32, (rows, 1), 0) & (seq_len - 1)

    def shifted(x, k):
        src = pos - k
        return jnp.where((src >= 0) & (src < seq_len), pltpu.roll(x, k % rows, 0), 0.0)

    for gi, win in enumerate(POOL_WINDOWS):
        @pl.when(g == gi)
        def _(win=win):
            half = win // 2
            u = u_ref[...]
            ahead, behind, span = u, u, 1
            while span < half:
                ahead = ahead + shifted(ahead, -span)
                behind = behind + shifted(behind, span)
                span *= 2
            acc = ahead + shifted(behind, 1)
            cnt = (jnp.minimum(pos + half, seq_len) - jnp.maximum(pos - half, 0)).astype(F32)
            diff = (acc / cnt - u).astype(BF16)
            y = jnp.dot(diff, w_ref[...], preferred_element_type=F32)
            o_ref[...] = (y * s_ref[...]).astype(o_ref.dtype)


def _pool_mixer(proj, pool_w, pool_scale, *, u_col, n_sample_blocks, prompt_seq):
    t = proj.shape[0]
    n_groups, cg, _ = pool_w.shape
    g0 = u_col // cg
    return pl.pallas_call(
        functools.partial(_pool_kernel, n_sample_blocks=n_sample_blocks, prompt_seq=prompt_seq),
        grid=(t // SEQ_BLOCK, n_groups),
        in_specs=[
            pl.BlockSpec((SEQ_BLOCK, cg), lambda i, g: (i, g0 + g)),
            pl.BlockSpec((None, cg, cg), lambda i, g: (g, 0, 0)),
            pl.BlockSpec((1, cg), lambda i, g: (0, g)),
        ],
        out_specs=pl.BlockSpec((SEQ_BLOCK, cg), lambda i, g: (i, g)),
        out_shape=jax.ShapeDtypeStruct((t, n_groups * cg), BF16),
        compiler_params=_cparams(("arbitrary", "arbitrary"), VMEM_POOL),
        name="pool_mixer",
    )(proj, pool_w, pool_scale)


def _qkt(q, k):
    return lax.dot_general(q, k, (((1,), (1,)), ((), ())), preferred_element_type=F32)


def _with_ones(v):
    return jnp.concatenate([v.astype(BF16), jnp.ones(v.shape, BF16)], axis=1)


def _softmax_pv(s_list, v_list, sink=None):
    tiles = [s[:, c:c + LANES] for s in s_list for c in range(0, s.shape[1], LANES)]
    mt = tiles[0]
    for tile in tiles[1:]:
        mt = jnp.maximum(mt, tile)
    m = mt.max(axis=-1, keepdims=True)
    if sink is not None:
        m = jnp.maximum(m, sink)
    acc = None
    for s, v in zip(s_list, v_list):
        part = jnp.dot(jnp.exp2(s - m).astype(BF16), v, preferred_element_type=F32)
        acc = part if acc is None else acc + part
    o, l = acc[:, :HEAD_DIM], acc[:, HEAD_DIM:]
    if sink is not None:
        l = l + jnp.exp2(sink - m)
    return o / l


def _dense_attn_kernel(*refs, n_kv, groups, has_sink):
    if has_sink:
        sink_ref, q_ref, k_ref, v_ref = refs[:4]
    else:
        q_ref, k_ref, v_ref = refs[:3]
    o_ref, nk_ref, nv_ref = refs[-3:]
    seq = q_ref.shape[0]
    qscale = HEAD_DIM ** -0.5 * LOG2E
    nk_ref[...] = k_ref[...]
    nv_ref[...] = v_ref[...]
    for h in range(n_kv):
        ks = slice(h * HEAD_DIM, (h + 1) * HEAD_DIM)
        k = k_ref[:, ks].astype(BF16)
        v = _with_ones(v_ref[:, ks])
        qs = [q_ref[:, (h * groups + g) * HEAD_DIM:(h * groups + g + 1) * HEAD_DIM] for g in range(groups)]
        q = ((qs[0] if groups == 1 else jnp.concatenate(qs, axis=0)) * qscale).astype(BF16)
        s = _qkt(q, k)
        sink = None
        if has_sink:
            sink = jnp.concatenate(
                [jnp.full((seq, 1), sink_ref[h * groups + g] * LOG2E, F32) for g in range(groups)], axis=0)
        o = _softmax_pv([s], [v], sink)
        for g in range(groups):
            c0 = (h * groups + g) * HEAD_DIM
            o_ref[:, c0:c0 + HEAD_DIM] = o[g * seq:(g + 1) * seq].astype(o_ref.dtype)


def _dense_attention(proj, sink, o_buf, *, row0, n_seq, seq, q_col, k_col, v_col, n_kv, groups):
    dq = n_kv * groups * HEAD_DIM
    dkv = n_kv * HEAD_DIM
    rb = row0 // seq
    in_specs = [
        pl.BlockSpec((seq, dq), lambda b: (rb + b, q_col // dq)),
        pl.BlockSpec((seq, dkv), lambda b: (rb + b, k_col // dkv)),
        pl.BlockSpec((seq, dkv), lambda b: (rb + b, v_col // dkv)),
    ]
    args = [proj, proj, proj]
    if sink is not None:
        in_specs = [pl.BlockSpec(memory_space=pltpu.SMEM)] + in_specs
        args = [sink] + args
    extra, extra_specs, aliases = _alias_args(o_buf, len(args))
    kv_shape = jax.ShapeDtypeStruct((n_seq * seq, dkv), F32)
    return pl.pallas_call(
        functools.partial(_dense_attn_kernel, n_kv=n_kv, groups=groups, has_sink=sink is not None),
        grid=(n_seq,),
        in_specs=in_specs + extra_specs,
        out_specs=[
            pl.BlockSpec((seq, dq), lambda b: (rb + b, 0)),
            pl.BlockSpec((seq, dkv), lambda b: (b, 0)),
            pl.BlockSpec((seq, dkv), lambda b: (b, 0)),
        ],
        out_shape=[jax.ShapeDtypeStruct(o_buf.shape, o_buf.dtype), kv_shape, kv_shape],
        input_output_aliases=aliases,
        compiler_params=_cparams(("arbitrary",), VMEM_ATTENTION),
        name="dense_attention",
    )(*args, *extra)


def _na_block_start(qb, rows):
    return min(max(qb * NA_QROWS - NA_ROWS // 2, 0), rows - NA_KROWS)


def _na_build_bias(rb_ref, bias_ref, rows):
    n_qb = rows // NA_QROWS
    qc = lax.broadcasted_iota(jnp.int32, (GRID_W, LANES), 0)
    lane = lax.broadcasted_iota(jnp.int32, (GRID_W, LANES), 1)
    kc = lane & (GRID_W - 1)
    ws = jnp.clip(qc - NA_COLS // 2, 0, GRID_W - NA_COLS)
    col_valid = (kc >= ws) & (kc < ws + NA_COLS)
    first_half = lane < GRID_W
    for kind, qb in enumerate((0, 1, n_qb - 1)):
        k_start = _na_block_start(qb, rows)
        for a in range(NA_QROWS):
            r = qb * NA_QROWS + a
            rs = min(max(r - NA_ROWS // 2, 0), rows - NA_ROWS)
            for pair in range(NA_KROWS // 2):
                vec = None
                valid = None
                for half in range(2):
                    kr = k_start + 2 * pair + half
                    if not rs <= kr < rs + NA_ROWS:
                        continue
                    drow = kr - r + NA_ROWS - 1
                    piece = pltpu.roll(rb_ref[0, drow:drow + 1, :], (half * GRID_W - (NA_COLS - 1)) % LANES, 1)
                    vec = piece if vec is None else vec + piece
                    hv = first_half if half == 0 else jnp.logical_not(first_half)
                    valid = hv if valid is None else jnp.logical_or(valid, hv)
                dst = (kind, slice(a * GRID_W, (a + 1) * GRID_W), slice(pair * LANES, (pair + 1) * LANES))
                if vec is None:
                    bias_ref[dst] = jnp.full((GRID_W, LANES), NEG, F32)
                else:
                    toeplitz = pltpu.roll(jnp.broadcast_to(vec, (GRID_W, LANES)), 0, 1, stride=1, stride_axis=0)
                    bias_ref[dst] = jnp.where(valid & col_valid, toeplitz * LOG2E, NEG)


def _na_kernel(q_ref, k_ref, v_ref, ck_ref, cv_ref, rb_ref, o_ref, kb_ref, vb_ref, bias_ref):
    rows = q_ref.shape[0] // GRID_W
    n_qb = rows // NA_QROWS
    qn = NA_QROWS * GRID_W
    kn = NA_KROWS * GRID_W
    qscale = HEAD_DIM ** -0.5 * LOG2E

    @pl.when(pl.program_id(1) == 0)
    def _():
        _na_build_bias(rb_ref, bias_ref, rows)

    kb_ref[...] = k_ref[...].astype(BF16)
    vb_ref[...] = _with_ones(v_ref[...])
    ck = ck_ref[0].astype(BF16)
    cv = _with_ones(cv_ref[0])
    for qb in range(n_qb):
        k0 = _na_block_start(qb, rows) * GRID_W
        kind = 0 if qb == 0 else (2 if qb == n_qb - 1 else 1)
        q = (q_ref[qb * qn:(qb + 1) * qn, :] * qscale).astype(BF16)
        s_loc = _qkt(q, kb_ref[k0:k0 + kn, :]) + bias_ref[kind]
        s_ctx = _qkt(q, ck)
        o = _softmax_pv([s_loc, s_ctx], [vb_ref[k0:k0 + kn, :], cv])
        o_ref[qb * qn:(qb + 1) * qn, :] = o.astype(o_ref.dtype)


def _na_attention(proj, ck, cv, rel_bias, *, n_batch, seq, n_heads, q_col, k_col, v_col):
    qn, kn = NA_QROWS * GRID_W, NA_KROWS * GRID_W
    qo, ko, vo = q_col // HEAD_DIM, k_col // HEAD_DIM, v_col // HEAD_DIM
    rb = jnp.zeros((n_heads, 2 * NA_ROWS, LANES), F32).at[:, :2 * NA_ROWS - 1, :2 * NA_COLS - 1].set(rel_bias)
    return pl.pallas_call(
        _na_kernel,
        grid=(n_heads, n_batch),
        in_specs=[
            pl.BlockSpec((seq, HEAD_DIM), lambda h, b: (b, qo + h)),
            pl.BlockSpec((seq, HEAD_DIM), lambda h, b: (b, ko + h)),
            pl.BlockSpec((seq, HEAD_DIM), lambda h, b: (b, vo + h)),
            pl.BlockSpec((1, ck.shape[1], HEAD_DIM), lambda h, b: (b, 0, h)),
            pl.BlockSpec((1, cv.shape[1], HEAD_DIM), lambda h, b: (b, 0, h)),
            pl.BlockSpec((1, 2 * NA_ROWS, LANES), lambda h, b: (h, 0, 0)),
        ],
        out_specs=pl.BlockSpec((seq, HEAD_DIM), lambda h, b: (b, h)),
        out_shape=jax.ShapeDtypeStruct((proj.shape[0], n_heads * HEAD_DIM), BF16),
        scratch_shapes=[pltpu.VMEM((seq, HEAD_DIM), BF16), pltpu.VMEM((seq, 2 * HEAD_DIM), BF16),
                        pltpu.VMEM((3, qn, kn), F32)],
        compiler_params=_cparams(("arbitrary", "arbitrary"), VMEM_ATTENTION),
        name="na_attention",
    )(proj, proj, proj, ck, cv, rb)


def _rope_tables(seq):
    half = HEAD_DIM // 2
    quarter = half // 2
    t = jnp.arange(seq)
    freqs = ROPE_BASE ** (-jnp.arange(quarter, dtype=F32) * 2.0 / half)
    ang_r = (t // GRID_W).astype(F32)[:, None] * freqs[None]
    ang_c = (t % GRID_W).astype(F32)[:, None] * freqs[None]
    cos = jnp.concatenate([jnp.cos(ang_r)] * 2 + [jnp.cos(ang_c)] * 2, axis=-1)
    sin = jnp.concatenate([-jnp.sin(ang_r), jnp.sin(ang_r), -jnp.sin(ang_c), jnp.sin(ang_c)], axis=-1)
    return cos, sin


def _rope(x, cos, sin):
    quarter = HEAD_DIM // 4
    lane = lax.broadcasted_iota(jnp.int32, x.shape, 1)
    first = (lane & (2 * quarter - 1)) < quarter
    partner = jnp.where(first, pltpu.roll(x, HEAD_DIM - quarter, 1), pltpu.roll(x, quarter, 1))
    return x * cos + partner * sin


def _win_kernel(sink_ref, q_ref, k_ref, v_ref, ck_ref, cv_ref, rope_ref, o_ref, kb_ref, vb_ref, mask_ref,
                *, groups):
    hkv = pl.program_id(1)
    seq = q_ref.shape[0]
    span = C_BLOCK + 2 * C_WINDOW
    n_blocks = seq // C_BLOCK

    def key_start(i):
        return min(max(i * C_BLOCK - C_WINDOW, 0), seq - span)

    @pl.when((pl.program_id(0) == 0) & (hkv == 0))
    def _():
        qrow = lax.broadcasted_iota(jnp.int32, (groups * C_BLOCK, span), 0) & (C_BLOCK - 1)
        kcol = lax.broadcasted_iota(jnp.int32, (groups * C_BLOCK, span), 1)
        for kind, i in enumerate((0, 1, n_blocks - 1)):
            valid = jnp.abs(kcol - qrow + (key_start(i) - i * C_BLOCK)) <= C_WINDOW
            mask_ref[kind] = jnp.where(valid, 0.0, NEG)

    kb_ref[...] = _rope(k_ref[...], rope_ref[0], rope_ref[1]).astype(BF16)
    vb_ref[...] = _with_ones(v_ref[...])
    ck = ck_ref[0].astype(BF16)
    cv = _with_ones(cv_ref[0])
    sink = jnp.concatenate(
        [jnp.full((C_BLOCK, 1), sink_ref[hkv * groups + g] * LOG2E, F32) for g in range(groups)], axis=0)
    for i in range(n_blocks):
        q0 = i * C_BLOCK
        k0 = key_start(i)
        kind = 0 if i == 0 else (2 if i == n_blocks - 1 else 1)
        cos = rope_ref[2, q0:q0 + C_BLOCK, :]
        sin = rope_ref[3, q0:q0 + C_BLOCK, :]
        q = jnp.concatenate(
            [_rope(q_ref[q0:q0 + C_BLOCK, g * HEAD_DIM:(g + 1) * HEAD_DIM], cos, sin) for g in range(groups)],
            axis=0).astype(BF16)
        s_loc = _qkt(q, kb_ref[k0:k0 + span, :]) + mask_ref[kind]
        s_ctx = _qkt(q, ck)
        o = _softmax_pv([s_loc, s_ctx], [vb_ref[k0:k0 + span, :], cv], sink)
        for g in range(groups):
            o_ref[q0:q0 + C_BLOCK, g * HEAD_DIM:(g + 1) * HEAD_DIM] = (
                o[g * C_BLOCK:(g + 1) * C_BLOCK].astype(o_ref.dtype))


def _win_attention(proj, ck, cv, sink, *, n_batch, seq, n_kv, groups, k_col, v_col):
    cos, sin = _rope_tables(seq)
    qscale = HEAD_DIM ** -0.5 * LOG2E
    rope = jnp.stack([cos, sin, cos * qscale, sin * qscale])
    ko, vo = k_col // HEAD_DIM, v_col // HEAD_DIM
    gw = groups * HEAD_DIM
    span = C_BLOCK + 2 * C_WINDOW
    return pl.pallas_call(
        functools.partial(_win_kernel, groups=groups),
        grid=(n_batch, n_kv),
        in_specs=[
            pl.BlockSpec(memory_space=pltpu.SMEM),
            pl.BlockSpec((seq, gw), lambda b, h: (b, h)),
            pl.BlockSpec((seq, HEAD_DIM), lambda b, h: (b, ko + h)),
            pl.BlockSpec((seq, HEAD_DIM), lambda b, h: (b, vo + h)),
            pl.BlockSpec((1, ck.shape[1], HEAD_DIM), lambda b, h: (b, 0, h)),
            pl.BlockSpec((1, cv.shape[1], HEAD_DIM), lambda b, h: (b, 0, h)),
            pl.BlockSpec((4, seq, HEAD_DIM), lambda b, h: (0, 0, 0)),
        ],
        out_specs=pl.BlockSpec((seq, gw), lambda b, h: (b, h)),
        out_shape=jax.ShapeDtypeStruct((proj.shape[0], n_kv * gw), BF16),
        scratch_shapes=[pltpu.VMEM((seq, HEAD_DIM), BF16), pltpu.VMEM((seq, 2 * HEAD_DIM), BF16),
                        pltpu.VMEM((3, groups * C_BLOCK, span), F32)],
        compiler_params=_cparams(("arbitrary", "arbitrary"), VMEM_ATTENTION),
        name="win_attention",
    )(sink, proj, proj, proj, ck, cv, rope)


def kernel(x_prompt, x_sample, cache_na_k, cache_na_v, cache_win_k, cache_win_v, c, c_ctx,
           norm_g, mod_w, mod_b, ffn_w_in, ffn_w_out, ab_w_in, pool_w, pool_scale,
           na_q_g, na_k_g, na_rel_bias, ab_w_out, win_w_in, win_q_g, win_k_g, win_sink, win_w_out):
    n_prompt, seq, d = x_prompt.shape
    n_dec, dec_seq, _ = x_sample.shape
    depth = norm_g.shape[0]
    assert dec_seq == SEQ_BLOCK and SEQ_BLOCK % seq == 0 and (n_prompt * seq) % SEQ_BLOCK == 0
    assert n_dec + 1 <= MOD_ROWS
    ts, tp = n_dec * dec_seq, n_prompt * seq
    t = ts + tp
    n_prompt_blocks = tp // SEQ_BLOCK

    c_all = jnp.concatenate([c, c_ctx[None], jnp.zeros((MOD_ROWS - n_dec - 1, d), F32)], axis=0)
    m_all = _modulation(c_all, mod_w, mod_b).reshape(depth, MOD_ROWS, N_MOD, d)
    mod = jnp.concatenate(
        [m_all[:, :n_dec], jnp.broadcast_to(m_all[:, n_dec:n_dec + 1], (depth, n_prompt_blocks, N_MOD, d))], axis=1)

    def ada(li, sub):
        m = mod[li]
        rows = jnp.stack([norm_g[li, sub][None] * (1.0 + m[:, 3 * sub + 1]), m[:, 3 * sub]], axis=1)
        return jnp.broadcast_to(rows[:, :, None, :], rows.shape[:2] + (8, d))

    n_ffn = ffn_w_in.shape[1]
    ffn_in_w = ffn_w_in.reshape(depth * n_ffn * d, -1)
    ffn_out_w = ffn_w_out.reshape(-1, d)
    xs2, xp2 = x_sample.reshape(ts, d), x_prompt.reshape(tp, d)

    def ffn_in(x_src, li, which, sub, cast=True, **kw):
        return _ffn_in(x_src, ada(li, sub), ffn_in_w, li * n_ffn + which, w_out=ffn_out_w if cast else None, **kw)

    def ffn_out(hidden, w_out, x_src, li, sub, **kw):
        return _matmul_resid([hidden], w_out, x_src, mod[li], gate_row=3 * sub + 2, mult=0.5,
                             tm=ROW_TILE, tn=FFN_COLS, single_buffer_a=False, vmem_mib=VMEM_FFN_OUT, **kw)

    def ffn(x, li, which, sub):
        hidden, w_out = ffn_in(x, li, which, sub, n_rows=t, out_row0=0, out_rows=t)
        return ffn_out(hidden, w_out, x, li, sub, n_rows=t, a_row0=0, x_row0=0, out_row0=0, out_rows=t)

    na_k, na_v, win_k, win_v = [], [], [], []
    x = None
    for li in range(depth):
        if li == 0:
            hidden, w_out = ffn_in(xs2, 0, 0, 0, n_rows=ts, out_row0=0, out_rows=t)
            hidden = ffn_in(xp2, 0, 0, 0, cast=False, n_rows=tp, out_row0=ts, out_rows=t, alias=hidden)
            x = ffn_out(hidden, w_out, xs2, 0, 0, n_rows=ts, a_row0=0, x_row0=0, out_row0=0, out_rows=t)
            x = ffn_out(hidden, w_out, xp2, 0, 0, n_rows=tp, a_row0=ts, x_row0=0, out_row0=ts, out_rows=t, alias=x)
        else:
            x = ffn(x, li, 0, 0)
        j = li // 2
        if li % 2 == 0:
            n_heads = cache_na_k.shape[3]
            d_na = n_heads * HEAD_DIM
            d_pool = ab_w_in.shape[2] - 3 * d_na
            tn = PROJ_COLS
            qk_t, pool_t = 2 * d_na // tn, d_pool // tn

            def col_perm(c):
                return jnp.where(c < qk_t, c + pool_t, jnp.where(c < qk_t + pool_t, c - qk_t, c))

            head_gain = jnp.concatenate(
                [jnp.tile(na_q_g[j], n_heads), jnp.tile(na_k_g[j], n_heads), jnp.ones((d_pool + d_na,), F32)])[None]
            q_col, k_col, u_col, v_col = 0, d_na, 2 * d_na, 2 * d_na + d_pool
            proj = _proj_in(x, ada(li, 1), ab_w_in[j], head_gain, tn=tn, norm_cols=2 * d_na, col_perm=col_perm)
            y_pool = _pool_mixer(proj, pool_w[j].astype(BF16), pool_scale[j][None], u_col=u_col,
                                 n_sample_blocks=n_dec, prompt_seq=seq)
            o = _na_attention(proj, cache_na_k[:, j].reshape(n_dec, -1, d_na),
                              cache_na_v[:, j].reshape(n_dec, -1, d_na), na_rel_bias[j], n_batch=n_dec, seq=dec_seq,
                              n_heads=n_heads, q_col=q_col, k_col=k_col, v_col=v_col)
            o, new_k, new_v = _dense_attention(proj, None, o, row0=ts, n_seq=n_prompt, seq=seq, q_col=q_col,
                                               k_col=k_col, v_col=v_col, n_kv=n_heads, groups=1)
            parts, w_out = [y_pool, o], ab_w_out[j].astype(BF16)
            na_k.append(new_k.reshape(n_prompt, seq, n_heads, HEAD_DIM))
            na_v.append(new_v.reshape(n_prompt, seq, n_heads, HEAD_DIM))
        else:
            n_kv = cache_win_k.shape[3]
            dkv = n_kv * HEAD_DIM
            groups = d // dkv
            head_gain = jnp.concatenate(
                [jnp.tile(win_q_g[j], n_kv * groups), jnp.tile(win_k_g[j], n_kv), jnp.ones((dkv,), F32)])[None]
            proj = _proj_in(x, ada(li, 1), win_w_in[j], head_gain,
                            tn=PROJ_COLS if (d + dkv) % PROJ_COLS == 0 else PROJ_COLS // 2, norm_cols=d + dkv)
            o = _win_attention(proj, cache_win_k[:, j].reshape(n_dec, -1, dkv),
                               cache_win_v[:, j].reshape(n_dec, -1, dkv), win_sink[j], n_batch=n_dec,
                               seq=dec_seq, n_kv=n_kv, groups=groups, k_col=d, v_col=d + dkv)
            o, new_k, new_v = _dense_attention(proj, win_sink[j], o, row0=ts, n_seq=n_prompt, seq=seq, q_col=0,
                                               k_col=d, v_col=d + dkv, n_kv=n_kv, groups=groups)
            parts, w_out = [o], win_w_out[j].astype(BF16)
            win_k.append(new_k.reshape(n_prompt, seq, n_kv, HEAD_DIM))
            win_v.append(new_v.reshape(n_prompt, seq, n_kv, HEAD_DIM))
        x = _matmul_resid(parts, w_out, x, mod[li], gate_row=5, mult=1.0, tm=ROW_TILE, tn=PROJ_COLS,
                          single_buffer_a=False, vmem_mib=VMEM_MIX_OUT, n_rows=t, a_row0=0, x_row0=0, out_row0=0, out_rows=t)
        if li < depth - 1:
            x = ffn(x, li, 1, 2)

    li = depth - 1
    hidden, w_out = ffn_in(x, li, 1, 2, n_rows=t, out_row0=0, out_rows=t)
    ys = ffn_out(hidden, w_out, x, li, 2, n_rows=ts, a_row0=0, x_row0=0, out_row0=0, out_rows=ts)
    yp = ffn_out(hidden, w_out, x, li, 2, n_rows=tp, a_row0=ts, x_row0=ts, out_row0=0, out_rows=tp)
    return (yp.reshape(n_prompt, seq, d), ys.reshape(n_dec, dec_seq, d), jnp.stack(na_k, axis=1),
            jnp.stack(na_v, axis=1), jnp.stack(win_k, axis=1), jnp.stack(win_v, axis=1))
```

```python
import functools

import jax
import jax.numpy as jnp
from jax import lax
from jax.experimental import pallas as pl
from jax.experimental.pallas import tpu as pltpu

EPS = 1e-6
NEG = -1e30
LOG2E = 1.4426950408889634
HEAD_DIM = 128
LANES = 128
GRID_W = 64
POOL_WINDOWS = (2, 4, 8, 16)
NA_ROWS = 8
NA_COLS = 16
C_WINDOW = 128
C_BLOCK = 128
ROPE_BASE = 10000.0
N_SUB = 3
N_MOD = 3 * N_SUB

SEQ_BLOCK = 2048
ROW_TILE = 1024
IN_TILE = 2048
PRO_ROWS = 256
CAST_ROWS = 64
STAT_ROWS = 128
NORM_ROWS = 16
NORM_COLS = 1024
NA_QROWS = 4
NA_KROWS = 12
MOD_ROWS = 16
MOD_COLS = 512
FFN_COLS = 256
PROJ_COLS = 512
VMEM_MODULATION = 40
VMEM_ADALN_MATMUL = 56
VMEM_FFN_OUT = 62
VMEM_MIX_OUT = 56
VMEM_POOL = 48
VMEM_ATTENTION = 32
MIB = 1024 * 1024
BF16 = jnp.bfloat16
F32 = jnp.float32


def _cparams(sem, vmem_mib):
    return pltpu.CompilerParams(dimension_semantics=sem, vmem_limit_bytes=vmem_mib * MIB)


def _alias_args(alias, n_inputs):
    if alias is None:
        return [], [], {}
    return [alias], [pl.BlockSpec(memory_space=pl.ANY)], {n_inputs: 0}


def _mod_kernel(c_ref, w_ref, b_ref, o_ref):
    c = c_ref[...]
    a = (c * jax.nn.sigmoid(c)).astype(BF16)
    w = w_ref[...].astype(BF16)
    o_ref[...] = jnp.dot(a, w, preferred_element_type=F32) + b_ref[...]


def _modulation(c_all, mod_w, mod_b, tn=MOD_COLS):
    depth, d, n = mod_w.shape
    return pl.pallas_call(
        _mod_kernel,
        grid=(depth, n // tn),
        in_specs=[
            pl.BlockSpec((MOD_ROWS, d), lambda l, j: (0, 0)),
            pl.BlockSpec((None, d, tn), lambda l, j: (l, 0, j)),
            pl.BlockSpec((None, 1, tn), lambda l, j: (l, 0, j)),
        ],
        out_specs=pl.BlockSpec((None, MOD_ROWS, tn), lambda l, j: (l, 0, j)),
        out_shape=jax.ShapeDtypeStruct((depth, MOD_ROWS, n), F32),
        compiler_params=_cparams(("arbitrary", "arbitrary"), VMEM_MODULATION),
        name="modulation",
    )(c_all, mod_w, mod_b.reshape(depth, 1, n))


def _adaln_rows(x_ref, ada_ref, h_ref, r_ref, row0):
    rows, d = x_ref.shape
    inv_d = 1.0 / d

    def stats(c, carry):
        r = pl.multiple_of(c * STAT_ROWS, STAT_ROWS)
        x = x_ref[pl.ds(r, STAT_ROWS), :]
        ms = jnp.sum(x * x, axis=-1, keepdims=True) * inv_d
        r_ref[pl.ds(r, STAT_ROWS), :] = jnp.broadcast_to(lax.rsqrt(ms + EPS), (STAT_ROWS, LANES))
        return carry

    def scale(c, carry):
        r = pl.multiple_of(c * NORM_ROWS, NORM_ROWS)
        rinv = jnp.concatenate([r_ref[pl.ds(r, NORM_ROWS), :]] * (NORM_COLS // LANES), axis=1)
        for c0 in range(0, d, NORM_COLS):
            cs = slice(c0, c0 + NORM_COLS)
            gain = jnp.concatenate([ada_ref[0, 0, :, cs]] * (NORM_ROWS // 8), axis=0)
            shift = jnp.concatenate([ada_ref[0, 1, :, cs]] * (NORM_ROWS // 8), axis=0)
            x = x_ref[pl.ds(r, NORM_ROWS), cs]
            h_ref[pl.ds(pl.multiple_of(row0 + r, NORM_ROWS), NORM_ROWS), cs] = (
                (x * rinv) * gain + shift).astype(h_ref.dtype)
        return carry

    lax.fori_loop(0, rows // STAT_ROWS, stats, 0)
    lax.fori_loop(0, rows // NORM_ROWS, scale, 0, unroll=2)


def _as_bf16(w):
    return w if w.dtype == BF16 else w.astype(BF16)


def _row_halves(ref):
    half = ref.shape[0] // 2
    return (slice(0, half), slice(half, 2 * half))


def _adaln_specs(d, tile0):
    n_pro = IN_TILE // PRO_ROWS
    per = SEQ_BLOCK // IN_TILE
    return n_pro, [
        pl.BlockSpec((PRO_ROWS, d), lambda i, j: (i * n_pro + jnp.minimum(j, n_pro - 1), 0)),
        pl.BlockSpec((1, 2, 8, d), lambda i, j: ((i + tile0) // per, 0, 0, 0)),
    ]


def _swiglu_kernel(x_ref, ada_ref, wg_ref, wu_ref, *rest, n_pro, has_cast):
    h_ref, r_ref = rest[-2:]
    o_ref = rest[-4] if has_cast else rest[-3]
    j = pl.program_id(1)

    if has_cast:
        rest[-3][...] = rest[0][...].astype(BF16)

    @pl.when(j < n_pro)
    def _():
        _adaln_rows(x_ref, ada_ref, h_ref, r_ref, j * PRO_ROWS)

    @pl.when(j >= n_pro)
    def _():
        wg = _as_bf16(wg_ref[...])
        wu = _as_bf16(wu_ref[...])
        for rows in _row_halves(h_ref):
            h = h_ref[rows, :]
            gate = jnp.dot(h, wg, preferred_element_type=F32)
            up = jnp.dot(h, wu, preferred_element_type=F32)
            o_ref[rows, :] = ((gate * jax.nn.sigmoid(gate)) * up).astype(o_ref.dtype)


def _ffn_in(x, ada, w_in, w_sel, *, n_rows, out_row0, out_rows, w_out=None, alias=None, tn=FFN_COLS):
    tm = IN_TILE
    d = x.shape[1]
    dff = w_in.shape[1] // 2
    nj = dff // tn
    tile0 = out_row0 // tm
    n_pro, specs = _adaln_specs(d, tile0)
    steps = n_pro + nj
    grid = (n_rows // tm, steps)

    def wcol(j):
        return jnp.maximum(j - n_pro, 0)

    in_specs = specs + [
        pl.BlockSpec((d, tn), lambda i, j: (w_sel, wcol(j))),
        pl.BlockSpec((d, tn), lambda i, j: (w_sel, wcol(j) + nj)),
    ]
    args = [x, ada, w_in, w_in]
    out_specs = [pl.BlockSpec((tm, tn), lambda i, j: (i + tile0, wcol(j)))]
    out_shape = [jax.ShapeDtypeStruct((out_rows, dff), BF16)]
    if w_out is not None:
        cast_rows = CAST_ROWS
        while dff // cast_rows > grid[0] * steps:
            cast_rows *= 2
        n_blk = dff // cast_rows

        def cast_blk(i, j):
            return jnp.minimum(i * steps + j, n_blk - 1)

        in_specs.append(pl.BlockSpec((cast_rows, d), lambda i, j: (w_sel * n_blk + cast_blk(i, j), 0)))
        args.append(w_out)
        out_specs.append(pl.BlockSpec((cast_rows, d), lambda i, j: (cast_blk(i, j), 0)))
        out_shape.append(jax.ShapeDtypeStruct((dff, d), BF16))
    extra, extra_specs, aliases = _alias_args(alias, len(args))
    outs = pl.pallas_call(
        functools.partial(_swiglu_kernel, n_pro=n_pro, has_cast=w_out is not None),
        grid=grid,
        in_specs=in_specs + extra_specs,
        out_specs=out_specs,
        out_shape=out_shape,
        scratch_shapes=[pltpu.VMEM((tm, d), BF16), pltpu.VMEM((PRO_ROWS, LANES), F32)],
        input_output_aliases=aliases,
        compiler_params=_cparams(("arbitrary", "arbitrary"), VMEM_ADALN_MATMUL),
        name="ffn_in",
    )(*args, *extra)
    return outs if w_out is not None else outs[0]


def _headnorm_kernel(x_ref, ada_ref, w_ref, hg_ref, o_ref, h_ref, r_ref, *, n_pro, norm_tiles):
    j = pl.program_id(1)

    @pl.when(j < n_pro)
    def _():
        _adaln_rows(x_ref, ada_ref, h_ref, r_ref, j * PRO_ROWS)

    @pl.when((j >= n_pro) & (j < n_pro + norm_tiles))
    def _():
        w = _as_bf16(w_ref[...])
        tn = o_ref.shape[1]
        col = pl.multiple_of((j - n_pro) * tn, tn)
        for rows in _row_halves(h_ref):
            y = jnp.dot(h_ref[rows, :], w, preferred_element_type=F32)
            for c in range(0, tn, HEAD_DIM):
                sl = slice(c, c + HEAD_DIM)
                yc = y[:, sl]
                ms = jnp.mean(yc * yc, axis=-1, keepdims=True)
                gain = hg_ref[:, pl.ds(pl.multiple_of(col + c, HEAD_DIM), HEAD_DIM)]
                o_ref[rows, sl] = (yc * lax.rsqrt(ms + EPS)) * gain

    @pl.when(j >= n_pro + norm_tiles)
    def _():
        w = _as_bf16(w_ref[...])
        for rows in _row_halves(h_ref):
            o_ref[rows, :] = jnp.dot(h_ref[rows, :], w, preferred_element_type=F32)


def _proj_in(x, ada, w, head_gain, *, tn, norm_cols, col_perm=None):
    tm = IN_TILE
    t, d = x.shape
    n = w.shape[1]
    n_pro, specs = _adaln_specs(d, 0)
    if col_perm is None:
        col_perm = lambda c: c

    def wcol(j):
        return jnp.maximum(j - n_pro, 0)

    return pl.pallas_call(
        functools.partial(_headnorm_kernel, n_pro=n_pro, norm_tiles=norm_cols // tn),
        grid=(t // tm, n_pro + n // tn),
        in_specs=specs + [
            pl.BlockSpec((d, tn), lambda i, j: (0, col_perm(wcol(j)))),
            pl.BlockSpec((1, n), lambda i, j: (0, 0)),
        ],
        out_specs=pl.BlockSpec((tm, tn), lambda i, j: (i, wcol(j))),
        out_shape=jax.ShapeDtypeStruct((t, n), F32),
        scratch_shapes=[pltpu.VMEM((tm, d), BF16), pltpu.VMEM((PRO_ROWS, LANES), F32)],
        compiler_params=_cparams(("arbitrary", "arbitrary"), VMEM_ADALN_MATMUL),
        name="proj_in",
    )(x, ada, w, head_gain)


def _resid_kernel(*refs, n_parts, gate_row, mult):
    a_refs = refs[:n_parts]
    w_refs = refs[n_parts:2 * n_parts]
    x_ref, mod_ref = refs[2 * n_parts:2 * n_parts + 2]
    o_ref = refs[-1]
    gate = mod_ref[0, gate_row:gate_row + 1, :]
    if mult != 1.0:
        gate = mult * gate
    ws = [_as_bf16(w_ref[...]) for w_ref in w_refs]
    for rows in _row_halves(o_ref):
        acc = jnp.dot(a_refs[0][rows, :], ws[0], preferred_element_type=F32)
        for a_ref, w in zip(a_refs[1:], ws[1:]):
            acc = acc + jnp.dot(a_ref[rows, :], w, preferred_element_type=F32)
        o_ref[rows, :] = x_ref[rows, :] + gate * acc


def _matmul_resid(parts, w, x, mod, *, gate_row, mult, tm, tn, single_buffer_a, n_rows, a_row0, x_row0,
                  out_row0, out_rows, vmem_mib, w_sel=0, alias=None):
    n_tiles, a_tile0, x_tile0, out_tile0 = n_rows // tm, a_row0 // tm, x_row0 // tm, out_row0 // tm
    d = w.shape[1]
    per = SEQ_BLOCK // tm
    a_mode = dict(pipeline_mode=pl.Buffered(1)) if single_buffer_a else {}
    k_total = sum(a.shape[1] for a in parts)
    in_specs, w_specs, off = [], [], w_sel * k_total
    for a in parts:
        k = a.shape[1]
        assert off % k == 0
        in_specs.append(pl.BlockSpec((tm, k), lambda i, j: (i + a_tile0, 0), **a_mode))
        w_specs.append(pl.BlockSpec((k, tn), functools.partial(lambda i, j, kb: (kb, j), kb=off // k)))
        off += k
    in_specs += w_specs + [
        pl.BlockSpec((tm, tn), lambda i, j: (i + x_tile0, j)),
        pl.BlockSpec((1, N_MOD, tn), lambda i, j: ((i + a_tile0) // per, 0, j)),
    ]
    out_spec = pl.BlockSpec((tm, tn), lambda i, j: (i + out_tile0, j))
    n_in = len(in_specs)
    extra, _, aliases = _alias_args(alias, n_in)
    body = functools.partial(_resid_kernel, n_parts=len(parts), gate_row=gate_row, mult=mult)

    def outer(*refs):
        pltpu.emit_pipeline(body, grid=(n_tiles, d // tn), in_specs=in_specs, out_specs=[out_spec])(
            *refs[:n_in], refs[-1])

    any_spec = pl.BlockSpec(memory_space=pl.ANY)
    return pl.pallas_call(
        outer,
        in_specs=[any_spec] * (n_in + len(extra)),
        out_specs=any_spec,
        out_shape=jax.ShapeDtypeStruct((out_rows, d), F32),
        input_output_aliases=aliases,
        compiler_params=pltpu.CompilerParams(vmem_limit_bytes=vmem_mib * MIB),
        name="matmul_resid",
    )(*parts, *([w] * len(parts)), x, mod, *extra)


def _pool_kernel(u_ref, w_ref, s_ref, o_ref, *, n_sample_blocks, prompt_seq):
    i = pl.program_id(0)
    g = pl.program_id(1)
    rows = u_ref.shape[0]
    seq_len = jnp.where(i < n_sample_blocks, rows, prompt_seq)
    pos = lax.broadcasted_iota(jnp.int32, (rows, 1), 0) & (seq_len - 1)

    def shifted(x, k):
        src = pos - k
        return jnp.where((src >= 0) & (src < seq_len), pltpu.roll(x, k % rows, 0), 0.0)

    for gi, win in enumerate(POOL_WINDOWS):
        @pl.when(g == gi)
        def _(win=win):
            half = win // 2
            u = u_ref[...]
            ahead, behind, span = u, u, 1
            while span < half:
                ahead = ahead + shifted(ahead, -span)
                behind = behind + shifted(behind, span)
                span *= 2
            acc = ahead + shifted(behind, 1)
            cnt = (jnp.minimum(pos + half, seq_len) - jnp.maximum(pos - half, 0)).astype(F32)
            diff = (acc / cnt - u).astype(BF16)
            y = jnp.dot(diff, w_ref[...], preferred_element_type=F32)
            o_ref[...] = (y * s_ref[...]).astype(o_ref.dtype)


def _pool_mixer(proj, pool_w, pool_scale, *, u_col, n_sample_blocks, prompt_seq):
    t = proj.shape[0]
    n_groups, cg, _ = pool_w.shape
    g0 = u_col // cg
    return pl.pallas_call(
        functools.partial(_pool_kernel, n_sample_blocks=n_sample_blocks, prompt_seq=prompt_seq),
        grid=(t // SEQ_BLOCK, n_groups),
        in_specs=[
            pl.BlockSpec((SEQ_BLOCK, cg), lambda i, g: (i, g0 + g)),
            pl.BlockSpec((None, cg, cg), lambda i, g: (g, 0, 0)),
            pl.BlockSpec((1, cg), lambda i, g: (0, g)),
        ],
        out_specs=pl.BlockSpec((SEQ_BLOCK, cg), lambda i, g: (i, g)),
        out_shape=jax.ShapeDtypeStruct((t, n_groups * cg), BF16),
        compiler_params=_cparams(("arbitrary", "arbitrary"), VMEM_POOL),
        name="pool_mixer",
    )(proj, pool_w, pool_scale)


def _qkt(q, k):
    return lax.dot_general(q, k, (((1,), (1,)), ((), ())), preferred_element_type=F32)


def _with_ones(v):
    return jnp.concatenate([v.astype(BF16), jnp.ones(v.shape, BF16)], axis=1)


def _softmax_pv(s_list, v_list, sink=None):
    tiles = [s[:, c:c + LANES] for s in s_list for c in range(0, s.shape[1], LANES)]
    mt = tiles[0]
    for tile in tiles[1:]:
        mt = jnp.maximum(mt, tile)
    m = mt.max(axis=-1, keepdims=True)
    if sink is not None:
        m = jnp.maximum(m, sink)
    acc = None
    for s, v in zip(s_list, v_list):
        part = jnp.dot(jnp.exp2(s - m).astype(BF16), v, preferred_element_type=F32)
        acc = part if acc is None else acc + part
    o, l = acc[:, :HEAD_DIM], acc[:, HEAD_DIM:]
    if sink is not None:
        l = l + jnp.exp2(sink - m)
    return o / l


def _dense_attn_kernel(*refs, n_kv, groups, has_sink):
    if has_sink:
        sink_ref, q_ref, k_ref, v_ref = refs[:4]
    else:
        q_ref, k_ref, v_ref = refs[:3]
    o_ref, nk_ref, nv_ref = refs[-3:]
    seq = q_ref.shape[0]
    qscale = HEAD_DIM ** -0.5 * LOG2E
    nk_ref[...] = k_ref[...]
    nv_ref[...] = v_ref[...]
    for h in range(n_kv):
        ks = slice(h * HEAD_DIM, (h + 1) * HEAD_DIM)
        k = k_ref[:, ks].astype(BF16)
        v = _with_ones(v_ref[:, ks])
        qs = [q_ref[:, (h * groups + g) * HEAD_DIM:(h * groups + g + 1) * HEAD_DIM] for g in range(groups)]
        q = ((qs[0] if groups == 1 else jnp.concatenate(qs, axis=0)) * qscale).astype(BF16)
        s = _qkt(q, k)
        sink = None
        if has_sink:
            sink = jnp.concatenate(
                [jnp.full((seq, 1), sink_ref[h * groups + g] * LOG2E, F32) for g in range(groups)], axis=0)
        o = _softmax_pv([s], [v], sink)
        for g in range(groups):
            c0 = (h * groups + g) * HEAD_DIM
            o_ref[:, c0:c0 + HEAD_DIM] = o[g * seq:(g + 1) * seq].astype(o_ref.dtype)


def _dense_attention(proj, sink, o_buf, *, row0, n_seq, seq, q_col, k_col, v_col, n_kv, groups):
    dq = n_kv * groups * HEAD_DIM
    dkv = n_kv * HEAD_DIM
    rb = row0 // seq
    in_specs = [
        pl.BlockSpec((seq, dq), lambda b: (rb + b, q_col // dq)),
        pl.BlockSpec((seq, dkv), lambda b: (rb + b, k_col // dkv)),
        pl.BlockSpec((seq, dkv), lambda b: (rb + b, v_col // dkv)),
    ]
    args = [proj, proj, proj]
    if sink is not None:
        in_specs = [pl.BlockSpec(memory_space=pltpu.SMEM)] + in_specs
        args = [sink] + args
    extra, extra_specs, aliases = _alias_args(o_buf, len(args))
    kv_shape = jax.ShapeDtypeStruct((n_seq * seq, dkv), F32)
    return pl.pallas_call(
        functools.partial(_dense_attn_kernel, n_kv=n_kv, groups=groups, has_sink=sink is not None),
        grid=(n_seq,),
        in_specs=in_specs + extra_specs,
        out_specs=[
            pl.BlockSpec((seq, dq), lambda b: (rb + b, 0)),
            pl.BlockSpec((seq, dkv), lambda b: (b, 0)),
            pl.BlockSpec((seq, dkv), lambda b: (b, 0)),
        ],
        out_shape=[jax.ShapeDtypeStruct(o_buf.shape, o_buf.dtype), kv_shape, kv_shape],
        input_output_aliases=aliases,
        compiler_params=_cparams(("arbitrary",), VMEM_ATTENTION),
        name="dense_attention",
    )(*args, *extra)


def _na_block_start(qb, rows):
    return min(max(qb * NA_QROWS - NA_ROWS // 2, 0), rows - NA_KROWS)


def _na_build_bias(rb_ref, bias_ref, rows):
    n_qb = rows // NA_QROWS
    qc = lax.broadcasted_iota(jnp.int32, (GRID_W, LANES), 0)
    lane = lax.broadcasted_iota(jnp.int32, (GRID_W, LANES), 1)
    kc = lane & (GRID_W - 1)
    ws = jnp.clip(qc - NA_COLS // 2, 0, GRID_W - NA_COLS)
    col_valid = (kc >= ws) & (kc < ws + NA_COLS)
    first_half = lane < GRID_W
    for kind, qb in enumerate((0, 1, n_qb - 1)):
        k_start = _na_block_start(qb, rows)
        for a in range(NA_QROWS):
            r = qb * NA_QROWS + a
            rs = min(max(r - NA_ROWS // 2, 0), rows - NA_ROWS)
            for pair in range(NA_KROWS // 2):
                vec = None
                valid = None
                for half in range(2):
                    kr = k_start + 2 * pair + half
                    if not rs <= kr < rs + NA_ROWS:
                        continue
                    drow = kr - r + NA_ROWS - 1
                    piece = pltpu.roll(rb_ref[0, drow:drow + 1, :], (half * GRID_W - (NA_COLS - 1)) % LANES, 1)
                    vec = piece if vec is None else vec + piece
                    hv = first_half if half == 0 else jnp.logical_not(first_half)
                    valid = hv if valid is None else jnp.logical_or(valid, hv)
                dst = (kind, slice(a * GRID_W, (a + 1) * GRID_W), slice(pair * LANES, (pair + 1) * LANES))
                if vec is None:
                    bias_ref[dst] = jnp.full((GRID_W, LANES), NEG, F32)
                else:
                    toeplitz = pltpu.roll(jnp.broadcast_to(vec, (GRID_W, LANES)), 0, 1, stride=1, stride_axis=0)
                    bias_ref[dst] = jnp.where(valid & col_valid, toeplitz * LOG2E, NEG)


def _na_kernel(q_ref, k_ref, v_ref, ck_ref, cv_ref, rb_ref, o_ref, kb_ref, vb_ref, bias_ref):
    rows = q_ref.shape[0] // GRID_W
    n_qb = rows // NA_QROWS
    qn = NA_QROWS * GRID_W
    kn = NA_KROWS * GRID_W
    qscale = HEAD_DIM ** -0.5 * LOG2E

    @pl.when(pl.program_id(1) == 0)
    def _():
        _na_build_bias(rb_ref, bias_ref, rows)

    kb_ref[...] = k_ref[...].astype(BF16)
    vb_ref[...] = _with_ones(v_ref[...])
    ck = ck_ref[0].astype(BF16)
    cv = _with_ones(cv_ref[0])
    for qb in range(n_qb):
        k0 = _na_block_start(qb, rows) * GRID_W
        kind = 0 if qb == 0 else (2 if qb == n_qb - 1 else 1)
        q = (q_ref[qb * qn:(qb + 1) * qn, :] * qscale).astype(BF16)
        s_loc = _qkt(q, kb_ref[k0:k0 + kn, :]) + bias_ref[kind]
        s_ctx = _qkt(q, ck)
        o = _softmax_pv([s_loc, s_ctx], [vb_ref[k0:k0 + kn, :], cv])
        o_ref[qb * qn:(qb + 1) * qn, :] = o.astype(o_ref.dtype)


def _na_attention(proj, ck, cv, rel_bias, *, n_batch, seq, n_heads, q_col, k_col, v_col):
    qn, kn = NA_QROWS * GRID_W, NA_KROWS * GRID_W
    qo, ko, vo = q_col // HEAD_DIM, k_col // HEAD_DIM, v_col // HEAD_DIM
    rb = jnp.zeros((n_heads, 2 * NA_ROWS, LANES), F32).at[:, :2 * NA_ROWS - 1, :2 * NA_COLS - 1].set(rel_bias)
    return pl.pallas_call(
        _na_kernel,
        grid=(n_heads, n_batch),
        in_specs=[
            pl.BlockSpec((seq, HEAD_DIM), lambda h, b: (b, qo + h)),
            pl.BlockSpec((seq, HEAD_DIM), lambda h, b: (b, ko + h)),
            pl.BlockSpec((seq, HEAD_DIM), lambda h, b: (b, vo + h)),
            pl.BlockSpec((1, ck.shape[1], HEAD_DIM), lambda h, b: (b, 0, h)),
            pl.BlockSpec((1, cv.shape[1], HEAD_DIM), lambda h, b: (b, 0, h)),
            pl.BlockSpec((1, 2 * NA_ROWS, LANES), lambda h, b: (h, 0, 0)),
        ],
        out_specs=pl.BlockSpec((seq, HEAD_DIM), lambda h, b: (b, h)),
        out_shape=jax.ShapeDtypeStruct((proj.shape[0], n_heads * HEAD_DIM), BF16),
        scratch_shapes=[pltpu.VMEM((seq, HEAD_DIM), BF16), pltpu.VMEM((seq, 2 * HEAD_DIM), BF16),
                        pltpu.VMEM((3, qn, kn), F32)],
        compiler_params=_cparams(("arbitrary", "arbitrary"), VMEM_ATTENTION),
        name="na_attention",
    )(proj, proj, proj, ck, cv, rb)


def _rope_tables(seq):
    half = HEAD_DIM // 2
    quarter = half // 2
    t = jnp.arange(seq)
    freqs = ROPE_BASE ** (-jnp.arange(quarter, dtype=F32) * 2.0 / half)
    ang_r = (t // GRID_W).astype(F32)[:, None] * freqs[None]
    ang_c = (t % GRID_W).astype(F32)[:, None] * freqs[None]
    cos = jnp.concatenate([jnp.cos(ang_r)] * 2 + [jnp.cos(ang_c)] * 2, axis=-1)
    sin = jnp.concatenate([-jnp.sin(ang_r), jnp.sin(ang_r), -jnp.sin(ang_c), jnp.sin(ang_c)], axis=-1)
    return cos, sin


def _rope(x, cos, sin):
    quarter = HEAD_DIM // 4
    lane = lax.broadcasted_iota(jnp.int32, x.shape, 1)
    first = (lane & (2 * quarter - 1)) < quarter
    partner = jnp.where(first, pltpu.roll(x, HEAD_DIM - quarter, 1), pltpu.roll(x, quarter, 1))
    return x * cos + partner * sin


def _win_kernel(sink_ref, q_ref, k_ref, v_ref, ck_ref, cv_ref, rope_ref, o_ref, kb_ref, vb_ref, mask_ref,
                *, groups):
    hkv = pl.program_id(1)
    seq = q_ref.shape[0]
    span = C_BLOCK + 2 * C_WINDOW
    n_blocks = seq // C_BLOCK

    def key_start(i):
        return min(max(i * C_BLOCK - C_WINDOW, 0), seq - span)

    @pl.when((pl.program_id(0) == 0) & (hkv == 0))
    def _():
        qrow = lax.broadcasted_iota(jnp.int32, (groups * C_BLOCK, span), 0) & (C_BLOCK - 1)
        kcol = lax.broadcasted_iota(jnp.int32, (groups * C_BLOCK, span), 1)
        for kind, i in enumerate((0, 1, n_blocks - 1)):
            valid = jnp.abs(kcol - qrow + (key_start(i) - i * C_BLOCK)) <= C_WINDOW
            mask_ref[kind] = jnp.where(valid, 0.0, NEG)

    kb_ref[...] = _rope(k_ref[...], rope_ref[0], rope_ref[1]).astype(BF16)
    vb_ref[...] = _with_ones(v_ref[...])
    ck = ck_ref[0].astype(BF16)
    cv = _with_ones(cv_ref[0])
    sink = jnp.concatenate(
        [jnp.full((C_BLOCK, 1), sink_ref[hkv * groups + g] * LOG2E, F32) for g in range(groups)], axis=0)
    for i in range(n_blocks):
        q0 = i * C_BLOCK
        k0 = key_start(i)
        kind = 0 if i == 0 else (2 if i == n_blocks - 1 else 1)
        cos = rope_ref[2, q0:q0 + C_BLOCK, :]
        sin = rope_ref[3, q0:q0 + C_BLOCK, :]
        q = jnp.concatenate(
            [_rope(q_ref[q0:q0 + C_BLOCK, g * HEAD_DIM:(g + 1) * HEAD_DIM], cos, sin) for g in range(groups)],
            axis=0).astype(BF16)
        s_loc = _qkt(q, kb_ref[k0:k0 + span, :]) + mask_ref[kind]
        s_ctx = _qkt(q, ck)
        o = _softmax_pv([s_loc, s_ctx], [vb_ref[k0:k0 + span, :], cv], sink)
        for g in range(groups):
            o_ref[q0:q0 + C_BLOCK, g * HEAD_DIM:(g + 1) * HEAD_DIM] = (
                o[g * C_BLOCK:(g + 1) * C_BLOCK].astype(o_ref.dtype))


def _win_attention(proj, ck, cv, sink, *, n_batch, seq, n_kv, groups, k_col, v_col):
    cos, sin = _rope_tables(seq)
    qscale = HEAD_DIM ** -0.5 * LOG2E
    rope = jnp.stack([cos, sin, cos * qscale, sin * qscale])
    ko, vo = k_col // HEAD_DIM, v_col // HEAD_DIM
    gw = groups * HEAD_DIM
    span = C_BLOCK + 2 * C_WINDOW
    return pl.pallas_call(
        functools.partial(_win_kernel, groups=groups),
        grid=(n_batch, n_kv),
        in_specs=[
            pl.BlockSpec(memory_space=pltpu.SMEM),
            pl.BlockSpec((seq, gw), lambda b, h: (b, h)),
            pl.BlockSpec((seq, HEAD_DIM), lambda b, h: (b, ko + h)),
            pl.BlockSpec((seq, HEAD_DIM), lambda b, h: (b, vo + h)),
            pl.BlockSpec((1, ck.shape[1], HEAD_DIM), lambda b, h: (b, 0, h)),
            pl.BlockSpec((1, cv.shape[1], HEAD_DIM), lambda b, h: (b, 0, h)),
            pl.BlockSpec((4, seq, HEAD_DIM), lambda b, h: (0, 0, 0)),
        ],
        out_specs=pl.BlockSpec((seq, gw), lambda b, h: (b, h)),
        out_shape=jax.ShapeDtypeStruct((proj.shape[0], n_kv * gw), BF16),
        scratch_shapes=[pltpu.VMEM((seq, HEAD_DIM), BF16), pltpu.VMEM((seq, 2 * HEAD_DIM), BF16),
                        pltpu.VMEM((3, groups * C_BLOCK, span), F32)],
        compiler_params=_cparams(("arbitrary", "arbitrary"), VMEM_ATTENTION),
        name="win_attention",
    )(sink, proj, proj, proj, ck, cv, rope)


def kernel(x_prompt, x_sample, cache_na_k, cache_na_v, cache_win_k, cache_win_v, c, c_ctx,
           norm_g, mod_w, mod_b, ffn_w_in, ffn_w_out, ab_w_in, pool_w, pool_scale,
           na_q_g, na_k_g, na_rel_bias, ab_w_out, win_w_in, win_q_g, win_k_g, win_sink, win_w_out):
    n_prompt, seq, d = x_prompt.shape
    n_dec, dec_seq, _ = x_sample.shape
    depth = norm_g.shape[0]
    assert dec_seq == SEQ_BLOCK and SEQ_BLOCK % seq == 0 and (n_prompt * seq) % SEQ_BLOCK == 0
    assert n_dec + 1 <= MOD_ROWS
    ts, tp = n_dec * dec_seq, n_prompt * seq
    t = ts + tp
    n_prompt_blocks = tp // SEQ_BLOCK

    c_all = jnp.concatenate([c, c_ctx[None], jnp.zeros((MOD_ROWS - n_dec - 1, d), F32)], axis=0)
    m_all = _modulation(c_all, mod_w, mod_b).reshape(depth, MOD_ROWS, N_MOD, d)
    mod = jnp.concatenate(
        [m_all[:, :n_dec], jnp.broadcast_to(m_all[:, n_dec:n_dec + 1], (depth, n_prompt_blocks, N_MOD, d))], axis=1)

    def ada(li, sub):
        m = mod[li]
        rows = jnp.stack([norm_g[li, sub][None] * (1.0 + m[:, 3 * sub + 1]), m[:, 3 * sub]], axis=1)
        return jnp.broadcast_to(rows[:, :, None, :], rows.shape[:2] + (8, d))

    n_ffn = ffn_w_in.shape[1]
    ffn_in_w = ffn_w_in.reshape(depth * n_ffn * d, -1)
    ffn_out_w = ffn_w_out.reshape(-1, d)
    xs2, xp2 = x_sample.reshape(ts, d), x_prompt.reshape(tp, d)

    def ffn_in(x_src, li, which, sub, cast=True, **kw):
        return _ffn_in(x_src, ada(li, sub), ffn_in_w, li * n_ffn + which, w_out=ffn_out_w if cast else None, **kw)

    def ffn_out(hidden, w_out, x_src, li, sub, **kw):
        return _matmul_resid([hidden], w_out, x_src, mod[li], gate_row=3 * sub + 2, mult=0.5,
                             tm=ROW_TILE, tn=FFN_COLS, single_buffer_a=False, vmem_mib=VMEM_FFN_OUT, **kw)

    def ffn(x, li, which, sub):
        hidden, w_out = ffn_in(x, li, which, sub, n_rows=t, out_row0=0, out_rows=t)
        return ffn_out(hidden, w_out, x, li, sub, n_rows=t, a_row0=0, x_row0=0, out_row0=0, out_rows=t)

    na_k, na_v, win_k, win_v = [], [], [], []
    x = None
    for li in range(depth):
        if li == 0:
            hidden, w_out = ffn_in(xs2, 0, 0, 0, n_rows=ts, out_row0=0, out_rows=t)
            hidden = ffn_in(xp2, 0, 0, 0, cast=False, n_rows=tp, out_row0=ts, out_rows=t, alias=hidden)
            x = ffn_out(hidden, w_out, xs2, 0, 0, n_rows=ts, a_row0=0, x_row0=0, out_row0=0, out_rows=t)
            x = ffn_out(hidden, w_out, xp2, 0, 0, n_rows=tp, a_row0=ts, x_row0=0, out_row0=ts, out_rows=t, alias=x)
        else:
            x = ffn(x, li, 0, 0)
        j = li // 2
        if li % 2 == 0:
            n_heads = cache_na_k.shape[3]
            d_na = n_heads * HEAD_DIM
            d_pool = ab_w_in.shape[2] - 3 * d_na
            tn = PROJ_COLS
            qk_t, pool_t = 2 * d_na // tn, d_pool // tn

            def col_perm(c):
                return jnp.where(c < qk_t, c + pool_t, jnp.where(c < qk_t + pool_t, c - qk_t, c))

            head_gain = jnp.concatenate(
                [jnp.tile(na_q_g[j], n_heads), jnp.tile(na_k_g[j], n_heads), jnp.ones((d_pool + d_na,), F32)])[None]
            q_col, k_col, u_col, v_col = 0, d_na, 2 * d_na, 2 * d_na + d_pool
            proj = _proj_in(x, ada(li, 1), ab_w_in[j], head_gain, tn=tn, norm_cols=2 * d_na, col_perm=col_perm)
            y_pool = _pool_mixer(proj, pool_w[j].astype(BF16), pool_scale[j][None], u_col=u_col,
                                 n_sample_blocks=n_dec, prompt_seq=seq)
            o = _na_attention(proj, cache_na_k[:, j].reshape(n_dec, -1, d_na),
                              cache_na_v[:, j].reshape(n_dec, -1, d_na), na_rel_bias[j], n_batch=n_dec, seq=dec_seq,
                              n_heads=n_heads, q_col=q_col, k_col=k_col, v_col=v_col)
            o, new_k, new_v = _dense_attention(proj, None, o, row0=ts, n_seq=n_prompt, seq=seq, q_col=q_col,
                                               k_col=k_col, v_col=v_col, n_kv=n_heads, groups=1)
            parts, w_out = [y_pool, o], ab_w_out[j].astype(BF16)
            na_k.append(new_k.reshape(n_prompt, seq, n_heads, HEAD_DIM))
            na_v.append(new_v.reshape(n_prompt, seq, n_heads, HEAD_DIM))
        else:
            n_kv = cache_win_k.shape[3]
            dkv = n_kv * HEAD_DIM
            groups = d // dkv
            head_gain = jnp.concatenate(
                [jnp.tile(win_q_g[j], n_kv * groups), jnp.tile(win_k_g[j], n_kv), jnp.ones((dkv,), F32)])[None]
            proj = _proj_in(x, ada(li, 1), win_w_in[j], head_gain,
                            tn=PROJ_COLS if (d + dkv) % PROJ_COLS == 0 else PROJ_COLS // 2, norm_cols=d + dkv)
            o = _win_attention(proj, cache_win_k[:, j].reshape(n_dec, -1, dkv),
                               cache_win_v[:, j].reshape(n_dec, -1, dkv), win_sink[j], n_batch=n_dec,
                               seq=dec_seq, n_kv=n_kv, groups=groups, k_col=d, v_col=d + dkv)
            o, new_k, new_v = _dense_attention(proj, win_sink[j], o, row0=ts, n_seq=n_prompt, seq=seq, q_col=0,
                                               k_col=d, v_col=d + dkv, n_kv=n_kv, groups=groups)
            parts, w_out = [o], win_w_out[j].astype(BF16)
            win_k.append(new_k.reshape(n_prompt, seq, n_kv, HEAD_DIM))
            win_v.append(new_v.reshape(n_prompt, seq, n_kv, HEAD_DIM))
        x = _matmul_resid(parts, w_out, x, mod[li], gate_row=5, mult=1.0, tm=ROW_TILE, tn=PROJ_COLS,
                          single_buffer_a=False, vmem_mib=VMEM_MIX_OUT, n_rows=t, a_row0=0, x_row0=0, out_row0=0, out_rows=t)
        if li < depth - 1:
            x = ffn(x, li, 1, 2)

    li = depth - 1
    hidden, w_out = ffn_in(x, li, 1, 2, n_rows=t, out_row0=0, out_rows=t)
    ys = ffn_out(hidden, w_out, x, li, 2, n_rows=ts, a_row0=0, x_row0=0, out_row0=0, out_rows=ts)
    yp = ffn_out(hidden, w_out, x, li, 2, n_rows=tp, a_row0=ts, x_row0=ts, out_row0=0, out_rows=tp)
    return (yp.reshape(n_prompt, seq, d), ys.reshape(n_dec, dec_seq, d), jnp.stack(na_k, axis=1),
            jnp.stack(na_v, axis=1), jnp.stack(win_k, axis=1), jnp.stack(win_v, axis=1))
```
